```python
import jax, jax.numpy as jnp
from jax import lax
import numpy as np

D_MODEL = 2048
BATCH = 8
SEQ = 8192
DEPTH = 1

N_ATTN_HEADS = 8
HEAD_DIM = 128
D_ATTN = N_ATTN_HEADS * HEAD_DIM
D_CONV = D_MODEL - D_ATTN
D_MIX = D_ATTN + D_CONV
D_IN = 3 * D_ATTN + 2 * D_CONV
DILATED_BRANCHES = ((128, 1), (512, 4), (2048, 16))
BLK = 128
CONV_WIDTH = 31
D_FF = ((-(-8 * D_MODEL // 3) + 255) // 256) * 256
D_PLE = 256
EPS = 1e-6

kernel_name = "hymba_dilated_attn_conformer_conv_hybrid"


def rmsnorm(x, g):
    xf = x.astype(jnp.float32)
    y = xf * lax.rsqrt(jnp.mean(xf * xf, axis=-1, keepdims=True) + EPS)
    return (y * g.astype(jnp.float32)).astype(x.dtype)


def layernorm(x, g, b):
    xf = x.astype(jnp.float32)
    mu = jnp.mean(xf, axis=-1, keepdims=True)
    var = jnp.mean(jnp.square(xf - mu), axis=-1, keepdims=True)
    y = (xf - mu) * lax.rsqrt(var + EPS)
    return (y * g.astype(jnp.float32) + b.astype(jnp.float32)).astype(x.dtype)


def dilated_branch(q, k, v, window, dilation):
    B, S, H, Dh = q.shape
    span = dilation * BLK
    L = -(-S // span) * span
    nb = L // span
    pad = ((0, 0), (0, L - S), (0, 0), (0, 0))

    def blocks(a):
        return jnp.pad(a, pad).reshape(B, nb, BLK, dilation, H, Dh)

    def with_prev(a):
        prev = jnp.pad(a, ((0, 0), (1, 0), (0, 0), (0, 0), (0, 0), (0, 0)))[:, :-1]
        return jnp.concatenate([prev, a], axis=2)

    qb = blocks(q)
    kk = with_prev(blocks(k))
    vv = with_prev(blocks(v))
    s = jnp.einsum('bnqrhd,bnkrhd->bnrhqk', qb, kk).astype(jnp.float32) * (Dh ** -0.5)
    qi = jnp.arange(BLK)[:, None]
    kj = jnp.arange(2 * BLK)[None, :]
    dist = qi + BLK - kj
    band = (dist >= 0) & (dist <= window // dilation)
    first = (jnp.arange(nb)[:, None, None] > 0) | (kj[None] >= BLK)
    mask = band[None] & first
    s = jnp.where(mask[None, :, None, None], s, -jnp.inf)
    m = jnp.max(s, axis=-1, keepdims=True)
    e = jnp.exp(s - m)
    den = jnp.sum(e, axis=-1, keepdims=True)
    o = jnp.einsum('bnrhqk,bnkrhd->bnqrhd', (e / den).astype(v.dtype), vv)
    o = o.reshape(B, L, H, Dh)[:, :S]
    lse = (m + jnp.log(den))[..., 0]
    lse = lse.transpose(0, 1, 4, 2, 3).reshape(B, L, H)[:, :S]
    return o, lse


def dilated_attention(q, k, v):
    outs, lses = [], []
    for window, dilation in DILATED_BRANCHES:
        o, lse = dilated_branch(q, k, v, window, dilation)
        outs.append(o)
        lses.append(lse)
    w = jax.nn.softmax(jnp.stack(lses, axis=0), axis=0)
    o = jnp.sum(w[..., None] * jnp.stack(outs, axis=0).astype(jnp.float32), axis=0)
    return o.astype(q.dtype)


def conformer_conv(cv, cg, w_dw, b_dw, g_ln, b_ln):
    u = cv * jax.nn.sigmoid(cg)
    u = lax.conv_general_dilated(u, w_dw.astype(u.dtype), window_strides=(1,),
                                 padding=[(CONV_WIDTH - 1, 0)],
                                 dimension_numbers=('NWC', 'WIO', 'NWC'),
                                 feature_group_count=D_CONV) + b_dw
    return jax.nn.silu(layernorm(u, g_ln, b_ln))


def _fwd_setup_inputs(seed: int = 0) -> dict:
    key = jax.random.key(seed)
    ks = jax.random.split(key, 20)

    def nrm(k, shape, scale):
        return jax.random.normal(k, shape, jnp.float32) * scale

    def gain(k, n):
        return 1.0 + 0.02 * jax.random.normal(k, (DEPTH, n), jnp.float32)

    return {
        "x": nrm(ks[0], (BATCH, SEQ, D_MODEL), 1.0),
        "p": nrm(ks[1], (DEPTH, BATCH, SEQ, D_PLE), 1.0),
        "g_mix": gain(ks[2], D_MODEL),
        "w_in": nrm(ks[3], (DEPTH, D_MODEL, D_IN), D_MODEL ** -0.5),
        "w_dw": nrm(ks[4], (DEPTH, CONV_WIDTH, 1, D_CONV), CONV_WIDTH ** -0.5),
        "b_dw": nrm(ks[5], (DEPTH, D_CONV), 0.02),
        "g_conv_ln": gain(ks[6], D_CONV),
        "b_conv_ln": nrm(ks[7], (DEPTH, D_CONV), 0.02),
        "w_out": nrm(ks[8], (DEPTH, D_MIX, D_MODEL), D_MIX ** -0.5),
        "g_ffn": gain(ks[9], D_MODEL),
        "w_gate": nrm(ks[10], (DEPTH, D_MODEL, D_FF), D_MODEL ** -0.5),
        "w_up": nrm(ks[11], (DEPTH, D_MODEL, D_FF), D_MODEL ** -0.5),
        "w_down": nrm(ks[12], (DEPTH, D_FF, D_MODEL), D_FF ** -0.5),
        "g_ple": gain(ks[13], D_MODEL),
        "w_pgate": nrm(ks[14], (DEPTH, D_MODEL, D_MODEL), D_MODEL ** -0.5),
        "b_pgate": nrm(ks[15], (DEPTH, D_MODEL), 0.02),
        "w_ple": nrm(ks[16], (DEPTH, D_PLE, D_MODEL), D_PLE ** -0.5),
        "g_final": 1.0 + 0.02 * jax.random.normal(ks[17], (D_MODEL,), jnp.float32),
    }


def _fwd_reference(x, p, g_mix, w_in, w_dw, b_dw, g_conv_ln, b_conv_ln, w_out, g_ffn,
              w_gate, w_up, w_down, g_ple, w_pgate, b_pgate, w_ple, g_final):
    B, S, _ = x.shape
    h = x
    for i in range(DEPTH):
        a = rmsnorm(h, g_mix[i])
        z = a @ w_in[i]
        q, k, v, cv, cg = jnp.split(
            z, [D_ATTN, 2 * D_ATTN, 3 * D_ATTN, 3 * D_ATTN + D_CONV], axis=-1)
        hd = (B, S, N_ATTN_HEADS, HEAD_DIM)
        o_attn = dilated_attention(q.reshape(hd), k.reshape(hd), v.reshape(hd)).reshape(B, S, D_ATTN)
        o_conv = conformer_conv(cv, cg, w_dw[i], b_dw[i], g_conv_ln[i], b_conv_ln[i])
        h = h + jnp.concatenate([o_attn, o_conv], axis=-1) @ w_out[i]
        f = rmsnorm(h, g_ffn[i])
        h = h + (jax.nn.silu(f @ w_gate[i]) * (f @ w_up[i])) @ w_down[i]
        gte = jax.nn.sigmoid(rmsnorm(h, g_ple[i]) @ w_pgate[i] + b_pgate[i])
        h = h + (p[i] @ w_ple[i]) * gte
    return rmsnorm(h, g_final)


import jax as _jax
import jax.numpy as _jnp

TWIN_FORMAT = 'train_step'
FWD_PARAMS = ['x', 'p', 'g_mix', 'w_in', 'w_dw', 'b_dw', 'g_conv_ln', 'b_conv_ln', 'w_out', 'g_ffn', 'w_gate', 'w_up', 'w_down', 'g_ple', 'w_pgate', 'b_pgate', 'w_ple', 'g_final']
TWIN_WEIGHTS = ['g_mix', 'w_in', 'w_dw', 'b_dw', 'g_conv_ln', 'b_conv_ln', 'w_out', 'g_ffn', 'w_gate', 'w_up', 'w_down', 'g_ple', 'w_pgate', 'b_pgate', 'w_ple', 'g_final']
TWIN_DIFF_INPUT = 'x'
TWIN_INPUTS = ['x', 'p', 'g_mix', 'w_in', 'w_dw', 'b_dw', 'g_conv_ln', 'b_conv_ln', 'w_out', 'g_ffn', 'w_gate', 'w_up', 'w_down', 'g_ple', 'w_pgate', 'b_pgate', 'w_ple', 'g_final', 'loss_target', 'm_g_mix', 'm_w_in', 'm_w_dw', 'm_b_dw', 'm_g_conv_ln', 'm_b_conv_ln', 'm_w_out', 'm_g_ffn', 'm_w_gate', 'm_w_up', 'm_w_down', 'm_g_ple', 'm_w_pgate', 'm_b_pgate', 'm_w_ple', 'm_g_final', 'v_g_mix', 'v_w_in', 'v_w_dw', 'v_b_dw', 'v_g_conv_ln', 'v_b_conv_ln', 'v_w_out', 'v_g_ffn', 'v_w_gate', 'v_w_up', 'v_w_down', 'v_g_ple', 'v_w_pgate', 'v_b_pgate', 'v_w_ple', 'v_g_final']
TWIN_OUTPUTS = ['loss', 'grad_x', 'grad_g_mix', 'grad_w_in', 'grad_w_dw', 'grad_b_dw', 'grad_g_conv_ln', 'grad_b_conv_ln', 'grad_w_out', 'grad_g_ffn', 'grad_w_gate', 'grad_w_up', 'grad_w_down', 'grad_g_ple', 'grad_w_pgate', 'grad_b_pgate', 'grad_w_ple', 'grad_g_final', 'delta_g_mix', 'delta_w_in', 'delta_w_dw', 'delta_b_dw', 'delta_g_conv_ln', 'delta_b_conv_ln', 'delta_w_out', 'delta_g_ffn', 'delta_w_gate', 'delta_w_up', 'delta_w_down', 'delta_g_ple', 'delta_w_pgate', 'delta_b_pgate', 'delta_w_ple', 'delta_g_final', 'new_m_g_mix', 'new_m_w_in', 'new_m_w_dw', 'new_m_b_dw', 'new_m_g_conv_ln', 'new_m_b_conv_ln', 'new_m_w_out', 'new_m_g_ffn', 'new_m_w_gate', 'new_m_w_up', 'new_m_w_down', 'new_m_g_ple', 'new_m_w_pgate', 'new_m_b_pgate', 'new_m_w_ple', 'new_m_g_final', 'new_v_g_mix', 'new_v_w_in', 'new_v_w_dw', 'new_v_b_dw', 'new_v_g_conv_ln', 'new_v_b_conv_ln', 'new_v_w_out', 'new_v_g_ffn', 'new_v_w_gate', 'new_v_w_up', 'new_v_w_down', 'new_v_g_ple', 'new_v_w_pgate', 'new_v_b_pgate', 'new_v_w_ple', 'new_v_g_final']
TWIN_LEAF_KINDS = {'loss': 'loss', 'grad_x': 'grad_x', 'grad_g_mix': 'grad_w', 'grad_w_in': 'grad_w', 'grad_w_dw': 'grad_w', 'grad_b_dw': 'grad_w', 'grad_g_conv_ln': 'grad_w', 'grad_b_conv_ln': 'grad_w', 'grad_w_out': 'grad_w', 'grad_g_ffn': 'grad_w', 'grad_w_gate': 'grad_w', 'grad_w_up': 'grad_w', 'grad_w_down': 'grad_w', 'grad_g_ple': 'grad_w', 'grad_w_pgate': 'grad_w', 'grad_b_pgate': 'grad_w', 'grad_w_ple': 'grad_w', 'grad_g_final': 'grad_w', 'delta_g_mix': 'delta_w', 'delta_w_in': 'delta_w', 'delta_w_dw': 'delta_w', 'delta_b_dw': 'delta_w', 'delta_g_conv_ln': 'delta_w', 'delta_b_conv_ln': 'delta_w', 'delta_w_out': 'delta_w', 'delta_g_ffn': 'delta_w', 'delta_w_gate': 'delta_w', 'delta_w_up': 'delta_w', 'delta_w_down': 'delta_w', 'delta_g_ple': 'delta_w', 'delta_w_pgate': 'delta_w', 'delta_b_pgate': 'delta_w', 'delta_w_ple': 'delta_w', 'delta_g_final': 'delta_w', 'new_m_g_mix': 'new_m', 'new_m_w_in': 'new_m', 'new_m_w_dw': 'new_m', 'new_m_b_dw': 'new_m', 'new_m_g_conv_ln': 'new_m', 'new_m_b_conv_ln': 'new_m', 'new_m_w_out': 'new_m', 'new_m_g_ffn': 'new_m', 'new_m_w_gate': 'new_m', 'new_m_w_up': 'new_m', 'new_m_w_down': 'new_m', 'new_m_g_ple': 'new_m', 'new_m_w_pgate': 'new_m', 'new_m_b_pgate': 'new_m', 'new_m_w_ple': 'new_m', 'new_m_g_final': 'new_m', 'new_v_g_mix': 'new_v', 'new_v_w_in': 'new_v', 'new_v_w_dw': 'new_v', 'new_v_b_dw': 'new_v', 'new_v_g_conv_ln': 'new_v', 'new_v_b_conv_ln': 'new_v', 'new_v_w_out': 'new_v', 'new_v_g_ffn': 'new_v', 'new_v_w_gate': 'new_v', 'new_v_w_up': 'new_v', 'new_v_w_down': 'new_v', 'new_v_g_ple': 'new_v', 'new_v_w_pgate': 'new_v', 'new_v_b_pgate': 'new_v', 'new_v_w_ple': 'new_v', 'new_v_g_final': 'new_v'}


def _forward(args):
    return _fwd_reference(*[args[k] for k in FWD_PARAMS])


def _output_shape():
    def fwd():
        inp = _fwd_setup_inputs(0)
        return _fwd_reference(*[inp[k] for k in FWD_PARAMS])
    out = _jax.eval_shape(fwd)
    return out.shape, out.dtype

N_MICROBATCH = 1
ADAM_LR = 0.001
ADAM_B1 = 0.9
ADAM_B2 = 0.999
ADAM_EPS = 1e-08
ADAM_WD = 0.01
ADAM_STEP = 10
PER_EXAMPLE_BATCH_AXIS = {'x': 0, 'p': 1, 'loss_target': 0}
SHARED_INPUTS = []
_WEIGHT_DTYPES = {'g_mix': _jnp.float32, 'w_in': _jnp.float32, 'w_dw': _jnp.float32, 'b_dw': _jnp.float32, 'g_conv_ln': _jnp.float32, 'b_conv_ln': _jnp.float32, 'w_out': _jnp.float32, 'g_ffn': _jnp.float32, 'w_gate': _jnp.float32, 'w_up': _jnp.float32, 'w_down': _jnp.float32, 'g_ple': _jnp.float32, 'w_pgate': _jnp.float32, 'b_pgate': _jnp.float32, 'w_ple': _jnp.float32, 'g_final': _jnp.float32}
MOMENT_SCALE = {'g_mix': 6.240838e-02, 'w_in': 3.971256e-02, 'w_dw': 7.324180e-02, 'b_dw': 1.465793e-01, 'g_conv_ln': 8.726979e-02, 'b_conv_ln': 7.684419e-02, 'w_out': 5.363846e-02, 'g_ffn': 8.432258e-02, 'w_gate': 3.498763e-02, 'w_up': 3.393968e-02, 'w_down': 5.627547e-02, 'g_ple': 1.996521e-02, 'w_pgate': 2.019332e-02, 'b_pgate': 5.149109e-02, 'w_ple': 5.084177e-02, 'g_final': 3.197155e+01}


def _to_microbatches(a, axis):
    t = _jnp.moveaxis(a, axis, 0)
    t = t.reshape((N_MICROBATCH, t.shape[0] // N_MICROBATCH) + t.shape[1:])
    return _jnp.moveaxis(t, 1, axis + 1)


def setup_inputs(seed: int = 0) -> dict:
    inp = _fwd_setup_inputs(seed)
    key = _jax.random.fold_in(_jax.random.key(seed), 7919)
    shape, _ = _output_shape()
    out = dict(inp)
    out["loss_target"] = _jax.random.normal(_jax.random.fold_in(key, 0), shape, _jnp.float32)
    for i, name in enumerate(TWIN_WEIGHTS):
        w = inp[name].astype(_jnp.float32)
        if MOMENT_SCALE is None:
            s = _jnp.sqrt(_jnp.mean(_jnp.square(w)) + 1e-30)
        else:
            s = MOMENT_SCALE[name]
        km, kv = _jax.random.split(_jax.random.fold_in(key, i + 1))
        out[name] = w
        out["m_" + name] = s * _jax.random.normal(km, w.shape, _jnp.float32)
        out["v_" + name] = (s * s) * _jax.random.uniform(kv, w.shape, _jnp.float32, 0.5, 1.5)
    if N_MICROBATCH > 1:
        for name, axis in PER_EXAMPLE_BATCH_AXIS.items():
            out[name] = _to_microbatches(out[name], axis)
    return {'x': out['x'], 'p': out['p'], 'g_mix': out['g_mix'], 'w_in': out['w_in'], 'w_dw': out['w_dw'], 'b_dw': out['b_dw'], 'g_conv_ln': out['g_conv_ln'], 'b_conv_ln': out['b_conv_ln'], 'w_out': out['w_out'], 'g_ffn': out['g_ffn'], 'w_gate': out['w_gate'], 'w_up': out['w_up'], 'w_down': out['w_down'], 'g_ple': out['g_ple'], 'w_pgate': out['w_pgate'], 'b_pgate': out['b_pgate'], 'w_ple': out['w_ple'], 'g_final': out['g_final'], 'loss_target': out['loss_target'], 'm_g_mix': out['m_g_mix'], 'm_w_in': out['m_w_in'], 'm_w_dw': out['m_w_dw'], 'm_b_dw': out['m_b_dw'], 'm_g_conv_ln': out['m_g_conv_ln'], 'm_b_conv_ln': out['m_b_conv_ln'], 'm_w_out': out['m_w_out'], 'm_g_ffn': out['m_g_ffn'], 'm_w_gate': out['m_w_gate'], 'm_w_up': out['m_w_up'], 'm_w_down': out['m_w_down'], 'm_g_ple': out['m_g_ple'], 'm_w_pgate': out['m_w_pgate'], 'm_b_pgate': out['m_b_pgate'], 'm_w_ple': out['m_w_ple'], 'm_g_final': out['m_g_final'], 'v_g_mix': out['v_g_mix'], 'v_w_in': out['v_w_in'], 'v_w_dw': out['v_w_dw'], 'v_b_dw': out['v_b_dw'], 'v_g_conv_ln': out['v_g_conv_ln'], 'v_b_conv_ln': out['v_b_conv_ln'], 'v_w_out': out['v_w_out'], 'v_g_ffn': out['v_g_ffn'], 'v_w_gate': out['v_w_gate'], 'v_w_up': out['v_w_up'], 'v_w_down': out['v_w_down'], 'v_g_ple': out['v_g_ple'], 'v_w_pgate': out['v_w_pgate'], 'v_b_pgate': out['v_b_pgate'], 'v_w_ple': out['v_w_ple'], 'v_g_final': out['v_g_final']}


def _loss(weights, diff, rest, loss_target):
    with _jax.named_scope("forward"):
        args = {**rest, TWIN_DIFF_INPUT: diff, **{k: w.astype(_WEIGHT_DTYPES[k]) for k, w in weights.items()}}
        y = _forward(args)
    with _jax.named_scope("loss_head"):
        err = _jnp.square(y.astype(_jnp.float32) - loss_target)
        return 0.5 * _jnp.sum(_jnp.mean(err, axis=-1)) if err.ndim else 0.5 * err


def _adamw(w, g, m, v):
    m = ADAM_B1 * m + (1.0 - ADAM_B1) * g
    v = ADAM_B2 * v + (1.0 - ADAM_B2) * _jnp.square(g)
    m_hat = m / (1.0 - ADAM_B1 ** ADAM_STEP)
    v_hat = v / (1.0 - ADAM_B2 ** ADAM_STEP)
    delta = -ADAM_LR * (m_hat / (_jnp.sqrt(v_hat) + ADAM_EPS) + ADAM_WD * w)
    return delta, m, v


def reference(x, p, g_mix, w_in, w_dw, b_dw, g_conv_ln, b_conv_ln, w_out, g_ffn, w_gate, w_up, w_down, g_ple, w_pgate, b_pgate, w_ple, g_final, loss_target, m_g_mix, m_w_in, m_w_dw, m_b_dw, m_g_conv_ln, m_b_conv_ln, m_w_out, m_g_ffn, m_w_gate, m_w_up, m_w_down, m_g_ple, m_w_pgate, m_b_pgate, m_w_ple, m_g_final, v_g_mix, v_w_in, v_w_dw, v_b_dw, v_g_conv_ln, v_b_conv_ln, v_w_out, v_g_ffn, v_w_gate, v_w_up, v_w_down, v_g_ple, v_w_pgate, v_b_pgate, v_w_ple, v_g_final):
    given = dict(x=x, p=p, g_mix=g_mix, w_in=w_in, w_dw=w_dw, b_dw=b_dw, g_conv_ln=g_conv_ln, b_conv_ln=b_conv_ln, w_out=w_out, g_ffn=g_ffn, w_gate=w_gate, w_up=w_up, w_down=w_down, g_ple=g_ple, w_pgate=w_pgate, b_pgate=b_pgate, w_ple=w_ple, g_final=g_final, loss_target=loss_target, m_g_mix=m_g_mix, m_w_in=m_w_in, m_w_dw=m_w_dw, m_b_dw=m_b_dw, m_g_conv_ln=m_g_conv_ln, m_b_conv_ln=m_b_conv_ln, m_w_out=m_w_out, m_g_ffn=m_g_ffn, m_w_gate=m_w_gate, m_w_up=m_w_up, m_w_down=m_w_down, m_g_ple=m_g_ple, m_w_pgate=m_w_pgate, m_b_pgate=m_b_pgate, m_w_ple=m_w_ple, m_g_final=m_g_final, v_g_mix=v_g_mix, v_w_in=v_w_in, v_w_dw=v_w_dw, v_b_dw=v_b_dw, v_g_conv_ln=v_g_conv_ln, v_b_conv_ln=v_b_conv_ln, v_w_out=v_w_out, v_g_ffn=v_g_ffn, v_w_gate=v_w_gate, v_w_up=v_w_up, v_w_down=v_w_down, v_g_ple=v_g_ple, v_w_pgate=v_w_pgate, v_b_pgate=v_b_pgate, v_w_ple=v_w_ple, v_g_final=v_g_final)
    weights = {n: given[n] for n in TWIN_WEIGHTS}
    shared = {n: given[n] for n in SHARED_INPUTS}
    per_example = {n: given[n] for n in ['x', 'p']}
    grad_fn = _jax.value_and_grad(_loss, argnums=(0, 1))

    def one_microbatch(ex, loss_target):
        ex = dict(ex)
        diff = ex.pop(TWIN_DIFF_INPUT)
        return grad_fn(weights, diff, {**shared, **ex}, loss_target)

    if N_MICROBATCH == 1:
        loss, (grad_w, grad_x) = one_microbatch(per_example, given["loss_target"])
    else:
        def body(carry, xs):
            loss_sum, grad_sum = carry
            l_k, (gw_k, gx_k) = one_microbatch(xs[0], xs[1])
            with _jax.named_scope("update"):
                return (loss_sum + l_k, _jax.tree.map(_jnp.add, grad_sum, gw_k)), gx_k

        init = (_jnp.zeros((), _jnp.float32), _jax.tree.map(_jnp.zeros_like, weights))
        (loss, grad_w), grad_x = _jax.lax.scan(body, init, (per_example, given["loss_target"]))
    with _jax.named_scope("update"):
        delta_w, new_m, new_v = {}, {}, {}
        for n in TWIN_WEIGHTS:
            delta_w[n], new_m[n], new_v[n] = _adamw(weights[n], grad_w[n], given["m_" + n], given["v_" + n])
    return (loss, grad_x, *[grad_w[n] for n in TWIN_WEIGHTS], *[delta_w[n] for n in TWIN_WEIGHTS],
            *[new_m[n] for n in TWIN_WEIGHTS], *[new_v[n] for n in TWIN_WEIGHTS])
```

```python
import functools

import jax
import jax.numpy as jnp
from jax import lax
from jax.experimental import pallas as pl
from jax.experimental.pallas import tpu as pltpu

F32 = jnp.float32
BF16 = jnp.bfloat16

EPS = 1e-6
HEAD_DIM = 128
BLK = 128
DILATIONS = (1, 4, 16)
SUPER = BLK * DILATIONS[-1]
CONV_W = 31
HALO = 32
ADAM_LR, ADAM_B1, ADAM_B2, ADAM_EPS, ADAM_WD, ADAM_STEP = 0.001, 0.9, 0.999, 1e-08, 0.01, 10

V7X_VMEM_BYTES = 64 * 1024 * 1024
VMEM_LIMIT = V7X_VMEM_BYTES * 3 // 4
MESH = pl.DeviceIdType.MESH
ANY = pl.BlockSpec(memory_space=pl.ANY)


def _params(semantics=None, **kw):
    return pltpu.CompilerParams(dimension_semantics=semantics, vmem_limit_bytes=VMEM_LIMIT, **kw)


def _tile(n, want, mult=128):
    if n <= want:
        return n
    for t in range(want - want % mult, 0, -mult):
        if n % t == 0:
            return t
    raise ValueError((n, want, mult))


_DIMS = {"nn": ((1,), (0,)), "nt": ((1,), (1,)), "tn": ((0,), (0,))}


def _matmul(name, groups, mode, out_dtypes, epilogue, extras=(), tm=1024, tn=1024, tk=2048,
            out_col_blocks=None, aliased=None):
    a0, b0 = groups[0][0]
    if mode == "nn":
        (M, K), N = a0.shape, b0.shape[1]
    elif mode == "nt":
        (M, K), N = a0.shape, b0.shape[0]
    else:
        (K, M), N = a0.shape, b0.shape[1]
    tm, tn, tk = _tile(M, tm), _tile(N, tn), _tile(K, tk)
    nk = K // tk
    if mode == "tn":
        a_spec = pl.BlockSpec((tk, tm), lambda i, j, k: (k, i))
    else:
        a_spec = pl.BlockSpec((tm, tk), lambda i, j, k: (i, k))
    if mode == "nt":
        b_spec = pl.BlockSpec((tn, tk), lambda i, j, k: (j, k))
    else:
        b_spec = pl.BlockSpec((tk, tn), lambda i, j, k: (k, j))
    operands, in_specs = [], []
    for grp in groups:
        for a, b in grp:
            operands += [a, b]
            in_specs += [a_spec, b_spec]
    for arr, kind in extras:
        operands.append(arr)
        if kind == "mn":
            in_specs.append(pl.BlockSpec((tm, tn), lambda i, j, k: (i, j)))
        else:
            in_specs.append(pl.BlockSpec((1, tn), lambda i, j, k: (0, j)))
    n_pairs = [len(g) for g in groups]
    n_ex, n_out, n_grp = len(extras), len(out_dtypes), len(groups)
    io_alias = {}
    if aliased is not None:
        assert n_out == 1
        io_alias = {len(operands): 0}
        operands.append(aliased)
        in_specs.append(ANY)
        off = out_col_blocks
        out_shape = [jax.ShapeDtypeStruct(aliased.shape, aliased.dtype)]
        out_specs = [pl.BlockSpec((tm, tn), lambda i, j, k: (i, j + off))]
    else:
        out_shape = [jax.ShapeDtypeStruct((M, N), dt) for dt in out_dtypes]
        out_specs = [pl.BlockSpec((tm, tn), lambda i, j, k: (i, j)) for _ in out_dtypes]
    n_alias = 1 if aliased is not None else 0

    def body(*refs):
        pos = 0
        parts = []
        for g in range(n_grp):
            part = None
            for _ in range(n_pairs[g]):
                a_ref, b_ref = refs[pos], refs[pos + 1]
                pos += 2
                d = lax.dot_general(a_ref[...].astype(BF16), b_ref[...].astype(BF16),
                                    (_DIMS[mode], ((), ())), preferred_element_type=F32)
                part = d if part is None else part + d
            parts.append(part)
        ex_refs = refs[pos:pos + n_ex]
        pos += n_ex + n_alias
        out_refs = refs[pos:pos + n_out]
        acc_refs = refs[pos + n_out:]

        def finish(accs):
            outs = epilogue(accs, [e[...] for e in ex_refs])
            for o_ref, o in zip(out_refs, outs):
                o_ref[...] = o.astype(o_ref.dtype)

        if nk == 1:
            finish(parts)
        else:
            k = pl.program_id(2)

            @pl.when(k == 0)
            def _():
                for acc, part in zip(acc_refs, parts):
                    acc[...] = part

            @pl.when(k > 0)
            def _():
                for acc, part in zip(acc_refs, parts):
                    acc[...] += part

            @pl.when(k == nk - 1)
            def _():
                finish([acc[...] for acc in acc_refs])

    scratch = [pltpu.VMEM((tm, tn), F32) for _ in range(n_grp)] if nk > 1 else []
    res = pl.pallas_call(
        body, name=name, grid=(M // tm, N // tn, nk),
        in_specs=in_specs, out_specs=out_specs, out_shape=out_shape, scratch_shapes=scratch,
        input_output_aliases=io_alias,
        compiler_params=_params(("parallel", "parallel", "arbitrary")),
    )(*operands)
    return res


def _plain(accs, extras):
    return (accs[0],)


def _add_residual(accs, extras):
    return (accs[0] + extras[0],)


def _rowwise(name, fn, rows, consts, row_outs, acc_outs, tb, aliased=None):
    T = rows[0][0].shape[0]
    tb = _tile(T, tb, 16)
    operands = [r[0] for r in rows] + list(consts)
    in_specs = [pl.BlockSpec((tb, c), functools.partial(lambda i, cb: (i, cb), cb=cb)) for _, c, cb in rows]
    in_specs += [pl.BlockSpec(c.shape, functools.partial(lambda i, nd: (0,) * nd, nd=c.ndim)) for c in consts]
    io_alias = {}
    if aliased is not None:
        io_alias = {len(operands): aliased[1]}
        operands.append(aliased[0])
        in_specs.append(ANY)
    out_shape = [jax.ShapeDtypeStruct(s, dt) for s, dt, _, _ in row_outs]
    out_specs = [pl.BlockSpec((tb, c), functools.partial(lambda i, cb: (i, cb), cb=cb)) for _, _, c, cb in row_outs]
    out_shape += [jax.ShapeDtypeStruct(s, F32) for s in acc_outs]
    out_specs += [pl.BlockSpec(s, functools.partial(lambda i, nd: (0,) * nd, nd=len(s))) for s in acc_outs]
    n_rows, n_consts, n_ro, n_alias = len(rows), len(consts), len(row_outs), (1 if aliased is not None else 0)

    def body(*refs):
        row_refs = refs[:n_rows]
        const_refs = refs[n_rows:n_rows + n_consts]
        out_refs = refs[n_rows + n_consts + n_alias:]
        ro, ao = fn([r[...] for r in row_refs], [c[...] for c in const_refs])
        for o_ref, o in zip(out_refs[:n_ro], ro):
            o_ref[...] = o.astype(o_ref.dtype)
        if acc_outs:
            i = pl.program_id(0)

            @pl.when(i == 0)
            def _():
                for a_ref, a in zip(out_refs[n_ro:], ao):
                    a_ref[...] = a

            @pl.when(i > 0)
            def _():
                for a_ref, a in zip(out_refs[n_ro:], ao):
                    a_ref[...] += a

    return pl.pallas_call(
        body, name=name, grid=(T // tb,), in_specs=in_specs, out_specs=out_specs, out_shape=out_shape,
        input_output_aliases=io_alias,
        compiler_params=_params(("arbitrary",) if acc_outs else ("parallel",)),
    )(*operands)


def _colsum(v):
    return jnp.sum(v, axis=0, keepdims=True)


def _rms_fwd(name, x, g, tb=512):
    T, D = x.shape

    def fn(rows, consts):
        xv, gv = rows[0], consts[0]
        r = lax.rsqrt(jnp.mean(xv * xv, axis=-1, keepdims=True) + EPS)
        return ((xv * r) * gv,), ()

    return _rowwise(name, fn, [(x, D, 0)], [g], [((T, D), BF16, D, 0)], [], tb)[0]


def _rms_bwd(name, dy, x, g, resid, tb=256):
    T, D = x.shape

    def fn(rows, consts):
        dyv, xv, rv = rows
        gv = consts[0]
        r = lax.rsqrt(jnp.mean(xv * xv, axis=-1, keepdims=True) + EPS)
        n = xv * r
        dn = dyv * gv
        dx = r * (dn - n * jnp.mean(dn * n, axis=-1, keepdims=True))
        tot = rv + dx
        return (tot, tot), (_colsum(dyv * n),)

    return _rowwise(name, fn, [(dy, D, 0), (x, D, 0), (resid, D, 0)], [g],
                    [((T, D), F32, D, 0), ((T, D), BF16, D, 0)], [(1, D)], tb)


def _tri_masks():
    qi = lax.broadcasted_iota(jnp.int32, (BLK, BLK), 0)
    kj = lax.broadcasted_iota(jnp.int32, (BLK, BLK), 1)
    return kj <= qi, kj >= qi


def _units():
    for dil in DILATIONS:
        for r in range(dil):
            for j in range(SUPER // (BLK * dil)):
                yield dil, r, j


def _dot_nt(a, b):
    return lax.dot_general(a, b, (((1,), (1,)), ((), ())), preferred_element_type=F32)


def _dot_tn(a, b):
    return lax.dot_general(a, b, (((0,), (0,)), ((), ())), preferred_element_type=F32)


def _dot_nn(a, b):
    return lax.dot_general(a, b, (((1,), (0,)), ((), ())), preferred_element_type=F32)


def _attn_fwd(z, cat_width, n_heads):
    T = z.shape[0]
    H = n_heads
    DA = H * HEAD_DIM
    nb = T // SUPER
    scale = HEAD_DIM ** -0.5

    def body(q_ref, k_ref, v_ref, kp_ref, vp_ref, cat_ref, o_ref, lse_ref, ob, lb):
        n = pl.program_id(1)
        own_mask, prev_mask = _tri_masks()
        has_prev = n > 0
        for b, dil in enumerate(DILATIONS):
            for r in range(dil):
                for j in range(SUPER // (BLK * dil)):
                    rows = pl.ds(j * BLK * dil + r, BLK, stride=dil)
                    qb = q_ref[rows, :].astype(BF16)
                    kb = k_ref[rows, :].astype(BF16)
                    vb = v_ref[rows, :].astype(BF16)
                    if j > 0:
                        prows = pl.ds((j - 1) * BLK * dil + r, BLK, stride=dil)
                        kpb = k_ref[prows, :].astype(BF16)
                        vpb = v_ref[prows, :].astype(BF16)
                        pmask = prev_mask
                    else:
                        prows = pl.ds(SUPER - BLK * dil + r, BLK, stride=dil)
                        kpb = kp_ref[prows, :].astype(BF16)
                        vpb = vp_ref[prows, :].astype(BF16)
                        pmask = jnp.logical_and(prev_mask, has_prev)
                    so = jnp.where(own_mask, _dot_nt(qb, kb) * scale, -jnp.inf)
                    sp = jnp.where(pmask, _dot_nt(qb, kpb) * scale, -jnp.inf)
                    m = jnp.maximum(jnp.max(so, axis=-1, keepdims=True), jnp.max(sp, axis=-1, keepdims=True))
                    eo = jnp.exp(so - m)
                    ep = jnp.exp(sp - m)
                    den = jnp.sum(eo, axis=-1, keepdims=True) + jnp.sum(ep, axis=-1, keepdims=True)
                    o = _dot_nn((eo / den).astype(BF16), vb) + _dot_nn((ep / den).astype(BF16), vpb)
                    ob[b, rows, :] = o
                    lb[b, rows, :] = jnp.broadcast_to(m + jnp.log(den), (BLK, HEAD_DIM))
        l0, l1, l2 = lb[0], lb[1], lb[2]
        mx = jnp.maximum(jnp.maximum(l0, l1), l2)
        tot = mx + jnp.log(jnp.exp(l0 - mx) + jnp.exp(l1 - mx) + jnp.exp(l2 - mx))
        o = jnp.exp(l0 - tot) * ob[0] + jnp.exp(l1 - tot) * ob[1] + jnp.exp(l2 - tot) * ob[2]
        o_ref[...] = o
        cat_ref[...] = o.astype(BF16)
        lse_ref[...] = tot

    blk = (SUPER, HEAD_DIM)
    in_specs = [
        pl.BlockSpec(blk, lambda h, n: (n, h)),
        pl.BlockSpec(blk, lambda h, n: (n, H + h)),
        pl.BlockSpec(blk, lambda h, n: (n, 2 * H + h)),
        pl.BlockSpec(blk, lambda h, n: (jnp.maximum(n - 1, 0), H + h)),
        pl.BlockSpec(blk, lambda h, n: (jnp.maximum(n - 1, 0), 2 * H + h)),
    ]
    out_spec = pl.BlockSpec(blk, lambda h, n: (n, h))
    return pl.pallas_call(
        body, name="attn_fwd", grid=(H, nb), in_specs=in_specs, out_specs=[out_spec] * 3,
        out_shape=[jax.ShapeDtypeStruct((T, cat_width), BF16), jax.ShapeDtypeStruct((T, DA), F32),
                   jax.ShapeDtypeStruct((T, DA), F32)],
        scratch_shapes=[pltpu.VMEM((3, SUPER, HEAD_DIM), F32), pltpu.VMEM((3, SUPER, HEAD_DIM), F32)],
        compiler_params=_params(("parallel", "parallel")),
    )(z, z, z, z, z)


def _attn_bwd(z, dcat, o, lse, n_heads):
    T = z.shape[0]
    H = n_heads
    nb = T // SUPER
    scale = HEAD_DIM ** -0.5

    def body(q_ref, k_ref, v_ref, kp_ref, vp_ref, do_ref, o_ref, lse_ref, dz_q, dz_k, dz_v,
             dq_acc, dk_acc, dv_acc, dkp_acc, dvp_acc, dsum):
        i = pl.program_id(1)
        own_mask, prev_mask = _tri_masks()
        has_prev = i < nb - 1

        @pl.when(i == 0)
        def _():
            dk_acc[...] = jnp.zeros_like(dk_acc)
            dv_acc[...] = jnp.zeros_like(dv_acc)

        @pl.when(i > 0)
        def _():
            dk_acc[...] = dkp_acc[...]
            dv_acc[...] = dvp_acc[...]

        dq_acc[...] = jnp.zeros_like(dq_acc)
        dkp_acc[...] = jnp.zeros_like(dkp_acc)
        dvp_acc[...] = jnp.zeros_like(dvp_acc)
        dsum[...] = jnp.broadcast_to(jnp.sum(do_ref[...] * o_ref[...], axis=-1, keepdims=True), (SUPER, HEAD_DIM))
        for dil in DILATIONS:
            for r in range(dil):
                for j in range(SUPER // (BLK * dil)):
                    rows = pl.ds(j * BLK * dil + r, BLK, stride=dil)
                    qb = q_ref[rows, :].astype(BF16)
                    kb = k_ref[rows, :].astype(BF16)
                    vb = v_ref[rows, :].astype(BF16)
                    dob = do_ref[rows, :].astype(BF16)
                    lseb = lse_ref[rows, :]
                    db = dsum[rows, :]
                    if j > 0:
                        prows = pl.ds((j - 1) * BLK * dil + r, BLK, stride=dil)
                        kpb = k_ref[prows, :].astype(BF16)
                        vpb = v_ref[prows, :].astype(BF16)
                        pmask = prev_mask
                        dk_dst, dv_dst = dk_acc, dv_acc
                    else:
                        prows = pl.ds(SUPER - BLK * dil + r, BLK, stride=dil)
                        kpb = kp_ref[prows, :].astype(BF16)
                        vpb = vp_ref[prows, :].astype(BF16)
                        pmask = jnp.logical_and(prev_mask, has_prev)
                        dk_dst, dv_dst = dkp_acc, dvp_acc
                    po = jnp.where(own_mask, jnp.exp(_dot_nt(qb, kb) * scale - lseb), 0.0)
                    pp = jnp.where(pmask, jnp.exp(_dot_nt(qb, kpb) * scale - lseb), 0.0)
                    dso = (po * (_dot_nt(dob, vb) - db) * scale).astype(BF16)
                    dsp = (pp * (_dot_nt(dob, vpb) - db) * scale).astype(BF16)
                    dq_acc[rows, :] += _dot_nn(dso, kb) + _dot_nn(dsp, kpb)
                    dk_acc[rows, :] += _dot_tn(dso, qb)
                    dv_acc[rows, :] += _dot_tn(po.astype(BF16), dob)
                    dk_dst[prows, :] += _dot_tn(dsp, qb)
                    dv_dst[prows, :] += _dot_tn(pp.astype(BF16), dob)
        dz_q[...] = dq_acc[...].astype(BF16)
        dz_k[...] = dk_acc[...].astype(BF16)
        dz_v[...] = dv_acc[...].astype(BF16)

    blk = (SUPER, HEAD_DIM)
    row = lambda i: nb - 1 - i
    in_specs = [
        pl.BlockSpec(blk, lambda h, i: (row(i), h)),
        pl.BlockSpec(blk, lambda h, i: (row(i), H + h)),
        pl.BlockSpec(blk, lambda h, i: (row(i), 2 * H + h)),
        pl.BlockSpec(blk, lambda h, i: (jnp.maximum(row(i) - 1, 0), H + h)),
        pl.BlockSpec(blk, lambda h, i: (jnp.maximum(row(i) - 1, 0), 2 * H + h)),
        pl.BlockSpec(blk, lambda h, i: (row(i), h)),
        pl.BlockSpec(blk, lambda h, i: (row(i), h)),
        pl.BlockSpec(blk, lambda h, i: (row(i), h)),
    ]
    out_spec = pl.BlockSpec(blk, lambda h, i: (row(i), h))
    return pl.pallas_call(
        body, name="attn_bwd", grid=(H, nb), in_specs=in_specs, out_specs=[out_spec] * 3,
        out_shape=[jax.ShapeDtypeStruct((T, H * HEAD_DIM), BF16)] * 3,
        scratch_shapes=[pltpu.VMEM(blk, F32) for _ in range(6)],
        compiler_params=_params(("parallel", "arbitrary")),
    )(z, z, z, z, z, dcat, o, lse)


def _glu(cv, cg):
    return cv * jax.nn.sigmoid(cg)


def _conv_fwd(z, cat, w_dw, b_dw, g_ln, b_ln, col0, tb=512):
    T = z.shape[0]
    DC = w_dw.shape[1]
    tb = _tile(T, tb, HALO)
    hb = tb // HALO
    cat_cb = cat.shape[1] // DC - 1

    def body(cv_ref, cg_ref, cvh_ref, cgh_ref, w_ref, bdw_ref, g_ref, b_ref, cat_in, cat_ref, c_ref, u_ext):
        i = pl.program_id(0)
        halo = _glu(cvh_ref[...], cgh_ref[...])
        u_ext[pl.ds(0, HALO), :] = jnp.where(i > 0, halo, 0.0)
        u_ext[pl.ds(HALO, tb), :] = _glu(cv_ref[...], cg_ref[...])
        acc = jnp.broadcast_to(bdw_ref[...], (tb, DC))
        for j in range(CONV_W):
            acc = acc + w_ref[pl.ds(j, 1), :] * u_ext[pl.ds(HALO - (CONV_W - 1) + j, tb), :]
        c_ref[...] = acc
        mu = jnp.mean(acc, axis=-1, keepdims=True)
        var = jnp.mean(jnp.square(acc - mu), axis=-1, keepdims=True)
        y = (acc - mu) * lax.rsqrt(var + EPS) * g_ref[...] + b_ref[...]
        cat_ref[...] = (y * jax.nn.sigmoid(y)).astype(BF16)

    cur = lambda cb: pl.BlockSpec((tb, DC), lambda i: (i, cb))
    halo = lambda cb: pl.BlockSpec((HALO, DC), lambda i: (jnp.maximum(i * hb - 1, 0), cb))
    whole = lambda a: pl.BlockSpec(a.shape, lambda i: (0, 0))
    return pl.pallas_call(
        body, name="conv_fwd", grid=(T // tb,),
        in_specs=[cur(col0), cur(col0 + 1), halo(col0), halo(col0 + 1), whole(w_dw), whole(b_dw), whole(g_ln),
                  whole(b_ln), ANY],
        out_specs=[pl.BlockSpec((tb, DC), lambda i: (i, cat_cb)), pl.BlockSpec((tb, DC), lambda i: (i, 0))],
        out_shape=[jax.ShapeDtypeStruct(cat.shape, cat.dtype), jax.ShapeDtypeStruct((T, DC), F32)],
        scratch_shapes=[pltpu.VMEM((tb + HALO, DC), F32)],
        input_output_aliases={8: 0},
        compiler_params=_params(("parallel",)),
    )(z, z, z, z, w_dw, b_dw, g_ln, b_ln, cat)


def _conv_ln_bwd(dcat, c, g_ln, b_ln, tb=256):
    T, DC = c.shape
    d_cb = dcat.shape[1] // DC - 1

    def fn(rows, consts):
        dov, cv_ = rows
        gv, bv = consts
        mu = jnp.mean(cv_, axis=-1, keepdims=True)
        xc = cv_ - mu
        rstd = lax.rsqrt(jnp.mean(jnp.square(xc), axis=-1, keepdims=True) + EPS)
        ln = xc * rstd
        y = ln * gv + bv
        sg = jax.nn.sigmoid(y)
        dy = dov * (sg * (1.0 + y * (1.0 - sg)))
        dln = dy * gv
        dc = rstd * (dln - jnp.mean(dln, axis=-1, keepdims=True) - ln * jnp.mean(dln * ln, axis=-1, keepdims=True))
        return (dc,), (_colsum(dy * ln), _colsum(dy), _colsum(dc))

    return _rowwise("conv_ln_bwd", fn, [(dcat, DC, d_cb), (c, DC, 0)], [g_ln, b_ln],
                    [((T, DC), F32, DC, 0)], [(1, DC)] * 3, tb)


def _conv_bwd(z, dc, w_dw, col0, tb=512):
    T, DC = dc.shape
    tb = _tile(T, tb, HALO)
    hb = tb // HALO
    nblk = T // tb

    def body(cv_ref, cg_ref, cvh_ref, cgh_ref, dc_ref, dcn_ref, w_ref, dcv_ref, dcg_ref, dw_ref, u_ext, dc_ext):
        i = pl.program_id(0)
        cv, cg = cv_ref[...], cg_ref[...]
        sg = jax.nn.sigmoid(cg)
        u_ext[pl.ds(0, HALO), :] = jnp.where(i > 0, _glu(cvh_ref[...], cgh_ref[...]), 0.0)
        u_ext[pl.ds(HALO, tb), :] = cv * sg
        dcv_ = dc_ref[...]
        dc_ext[pl.ds(0, tb), :] = dcv_
        dc_ext[pl.ds(tb, HALO), :] = jnp.where(i < nblk - 1, dcn_ref[...], 0.0)
        @pl.when(i == 0)
        def _():
            dw_ref[...] = jnp.zeros_like(dw_ref)

        du = jnp.zeros((tb, DC), F32)
        for j in range(CONV_W):
            du = du + w_ref[pl.ds(j, 1), :] * dc_ext[pl.ds(CONV_W - 1 - j, tb), :]
            dw_ref[pl.ds(j, 1), :] += _colsum(u_ext[pl.ds(HALO - (CONV_W - 1) + j, tb), :] * dcv_)
        dcv_ref[...] = (du * sg).astype(BF16)
        dcg_ref[...] = (du * cv * sg * (1.0 - sg)).astype(BF16)

    cur = lambda cb: pl.BlockSpec((tb, DC), lambda i: (i, cb))
    halo = lambda cb: pl.BlockSpec((HALO, DC), lambda i: (jnp.maximum(i * hb - 1, 0), cb))
    nxt = pl.BlockSpec((HALO, DC), lambda i: (jnp.minimum((i + 1) * hb, T // HALO - 1), 0))
    return pl.pallas_call(
        body, name="conv_bwd", grid=(nblk,),
        in_specs=[cur(col0), cur(col0 + 1), halo(col0), halo(col0 + 1), cur(0), nxt,
                  pl.BlockSpec(w_dw.shape, lambda i: (0, 0))],
        out_specs=[cur(0), cur(0), pl.BlockSpec((HALO, DC), lambda i: (0, 0))],
        out_shape=[jax.ShapeDtypeStruct((T, DC), BF16), jax.ShapeDtypeStruct((T, DC), BF16),
                   jax.ShapeDtypeStruct((HALO, DC), F32)],
        scratch_shapes=[pltpu.VMEM((tb + HALO, DC), F32), pltpu.VMEM((tb + HALO, DC), F32)],
        compiler_params=_params(("arbitrary",)),
    )(z, z, z, z, dc, dc, w_dw)


def _loss_head(h3, target, g_final, gte, e, tb=256):
    T, D = h3.shape

    def fn(rows, consts):
        hv, tv, gt, ev = rows
        gv = consts[0]
        gt = gt.astype(F32)
        ev = ev.astype(F32)
        r = lax.rsqrt(jnp.mean(hv * hv, axis=-1, keepdims=True) + EPS)
        n = hv * r
        diff = n * gv - tv
        loss = 0.5 * jnp.sum(jnp.mean(jnp.square(diff), axis=-1, keepdims=True), axis=0, keepdims=True)
        dy = diff * (1.0 / D)
        dn = dy * gv
        dh = r * (dn - n * jnp.mean(dn * n, axis=-1, keepdims=True))
        du4 = dh * ev * gt * (1.0 - gt)
        return (dh, dh * gt, du4), (_colsum(dy * n), _colsum(du4), jnp.broadcast_to(loss, (1, 128)))

    return _rowwise("loss_head", fn, [(h3, D, 0), (target, D, 0), (gte, D, 0), (e, D, 0)], [g_final],
                    [((T, D), F32, D, 0), ((T, D), BF16, D, 0), ((T, D), BF16, D, 0)],
                    [(1, D), (1, D), (1, 128)], tb)


def _me():
    return lax.axis_index("x"), lax.axis_index("y"), lax.axis_index("c")


def _region(ref, axis, shard, half, n_shards=4):
    R, C = ref.shape
    if axis == 1:
        cs, hr = C // n_shards, R // 2
        return ref.at[pl.ds(half * hr, hr), pl.ds(shard * cs, cs)]
    rs = R // n_shards
    hr = rs // 2
    return ref.at[pl.ds(shard * rs + half * hr, hr), :]


def _gather_weights(shards, axes, small):
    n = len(shards)

    def body(*refs):
        shard_refs = refs[:n]
        small_ref = refs[n]
        out_refs = refs[n + 1:2 * n + 1]
        small_out = refs[2 * n + 1]
        send, recv, fsend, frecv, lsem = refs[2 * n + 2:]
        x, y, c = _me()
        me_s = 2 * x + y
        sibling = (x, y, 1 - c)
        chips = [(1 - x, y), (x, 1 - y), (1 - x, 1 - y)]

        def half_of(k):
            r = shard_refs[k].shape[0] // 2
            return shard_refs[k].at[pl.ds(c * r, r), :]

        def shard_dst(k, s):
            R, C = out_refs[k].shape
            if axes[k] == 1:
                return out_refs[k].at[:, pl.ds(s * (C // 4), C // 4)]
            return out_refs[k].at[pl.ds(s * (R // 4), R // 4), :]

        locals_ = [pltpu.make_async_copy(shard_refs[k], shard_dst(k, me_s), lsem.at[k]) for k in range(n)]
        cs = small_ref.shape[1]
        locals_.append(pltpu.make_async_copy(small_ref, small_out.at[:, pl.ds(me_s * cs, cs)], lsem.at[n]))
        for cp in locals_:
            cp.start()
        firsts = []
        for k in range(n):
            for j, (px, py) in enumerate(chips):
                firsts.append(pltpu.make_async_remote_copy(
                    src_ref=half_of(k), dst_ref=_region(out_refs[k], axes[k], me_s, c),
                    send_sem=send.at[k, j], recv_sem=recv.at[k, j], device_id=(px, py, c), device_id_type=MESH))
        for j, (px, py) in enumerate(chips):
            firsts.append(pltpu.make_async_remote_copy(
                src_ref=small_ref, dst_ref=small_out.at[:, pl.ds(me_s * cs, cs)],
                send_sem=send.at[n, j], recv_sem=recv.at[n, j], device_id=(px, py, c), device_id_type=MESH))
        for cp in firsts:
            cp.start()
        relays = []
        for k in range(n):
            for j, (px, py) in enumerate(chips):
                landed = _region(out_refs[k], axes[k], 2 * px + py, c)
                pltpu.make_async_remote_copy(
                    src_ref=half_of(k), dst_ref=landed, send_sem=send.at[k, j], recv_sem=recv.at[k, j],
                    device_id=(px, py, c), device_id_type=MESH).wait_recv()
                relay = pltpu.make_async_remote_copy(
                    src_ref=landed, dst_ref=landed, send_sem=fsend.at[k, j], recv_sem=frecv.at[k, j],
                    device_id=sibling, device_id_type=MESH)
                relay.start()
                relays.append(relay)
        for j, (px, py) in enumerate(chips):
            pltpu.make_async_remote_copy(
                src_ref=small_ref, dst_ref=small_out.at[:, pl.ds((2 * px + py) * cs, cs)],
                send_sem=send.at[n, j], recv_sem=recv.at[n, j], device_id=(px, py, c),
                device_id_type=MESH).wait_recv()
        for k in range(n):
            for j, (px, py) in enumerate(chips):
                theirs = _region(out_refs[k], axes[k], 2 * px + py, 1 - c)
                pltpu.make_async_remote_copy(
                    src_ref=theirs, dst_ref=theirs, send_sem=fsend.at[k, j], recv_sem=frecv.at[k, j],
                    device_id=sibling, device_id_type=MESH).wait_recv()
        for cp in firsts + relays:
            cp.wait_send()
        for cp in locals_:
            cp.wait()

    def full_shape(k):
        R, C = shards[k].shape
        return (R, 4 * C) if axes[k] == 1 else (4 * R, C)

    out_shape = [jax.ShapeDtypeStruct(full_shape(k), shards[k].dtype) for k in range(n)]
    out_shape.append(jax.ShapeDtypeStruct((small.shape[0], 4 * small.shape[1]), small.dtype))
    return pl.pallas_call(
        body, name="gather_weights", in_specs=[ANY] * (n + 1), out_specs=[ANY] * (n + 1), out_shape=out_shape,
        scratch_shapes=[pltpu.SemaphoreType.DMA((n + 1, 3)), pltpu.SemaphoreType.DMA((n + 1, 3)),
                        pltpu.SemaphoreType.DMA((n, 3)), pltpu.SemaphoreType.DMA((n, 3)),
                        pltpu.SemaphoreType.DMA((n + 1,))],
        compiler_params=_params(),
    )(*shards, small)


def _scatter_grads(grads, axes, small):
    n = len(grads)

    def piece_shape(k):
        R, C = grads[k].shape
        return (R // 2, C // 4) if axes[k] == 1 else (R // 8, C)

    def body(*refs):
        g_refs = refs[:n]
        small_ref = refs[n]
        out_refs = refs[n + 1:2 * n + 1]
        small_out = refs[2 * n + 1]
        send, recv, lsem = refs[2 * n + 2:]
        x, y, c = _me()
        me = 4 * x + 2 * y + c
        copies = []
        for k in range(n + 1):
            src = small_ref if k == n else _region(g_refs[k], axes[k], 2 * x + y, c)
            dst = small_out.at[me] if k == n else out_refs[k].at[me]
            copies.append(pltpu.make_async_copy(src, dst, lsem.at[k]))
        for cp in copies:
            cp.start()
        sends = []
        for m in range(1, 8):
            px, py, pc = x ^ (m >> 2), y ^ ((m >> 1) & 1), c ^ (m & 1)
            for k in range(n + 1):
                src = small_ref if k == n else _region(g_refs[k], axes[k], 2 * px + py, pc)
                dst = small_out.at[me] if k == n else out_refs[k].at[me]
                sends.append(pltpu.make_async_remote_copy(
                    src_ref=src, dst_ref=dst, send_sem=send.at[k, m - 1], recv_sem=recv.at[k, m - 1],
                    device_id=(px, py, pc), device_id_type=MESH))
        for cp in sends:
            cp.start()
        for m in range(1, 8):
            px, py, pc = x ^ (m >> 2), y ^ ((m >> 1) & 1), c ^ (m & 1)
            them = 4 * px + 2 * py + pc
            for k in range(n + 1):
                src = small_ref if k == n else _region(g_refs[k], axes[k], 2 * x + y, c)
                dst = small_out.at[them] if k == n else out_refs[k].at[them]
                pltpu.make_async_remote_copy(
                    src_ref=src, dst_ref=dst, send_sem=send.at[k, m - 1], recv_sem=recv.at[k, m - 1],
                    device_id=(px, py, pc), device_id_type=MESH).wait_recv()
        for cp in sends:
            cp.wait_send()
        for cp in copies:
            cp.wait()

    out_shape = [jax.ShapeDtypeStruct((8,) + piece_shape(k), grads[k].dtype) for k in range(n)]
    out_shape.append(jax.ShapeDtypeStruct((8,) + small.shape, small.dtype))
    return pl.pallas_call(
        body, name="scatter_grads", in_specs=[ANY] * (n + 1), out_specs=[ANY] * (n + 1), out_shape=out_shape,
        scratch_shapes=[pltpu.SemaphoreType.DMA((n + 1, 7)), pltpu.SemaphoreType.DMA((n + 1, 7)),
                        pltpu.SemaphoreType.DMA((n + 1,))],
        compiler_params=_params(),
    )(*grads, small)


def _sum_slots(name, slots, tr=256):
    S, R, C = slots.shape
    tr = _tile(R, tr, 16)

    def body(s_ref, o_ref):
        acc = s_ref[0].astype(F32)
        for s in range(1, S):
            acc = acc + s_ref[s].astype(F32)
        o_ref[...] = acc

    return pl.pallas_call(
        body, name=name, grid=(R // tr,), in_specs=[pl.BlockSpec((S, tr, C), lambda i: (0, i, 0))],
        out_specs=pl.BlockSpec((tr, C), lambda i: (i, 0)), out_shape=jax.ShapeDtypeStruct((R, C), F32),
        compiler_params=_params(("parallel",)),
    )(slots)


def _join_halves(pieces, axes):
    n = len(pieces)

    def body(*refs):
        p_refs = refs[:n]
        out_refs = refs[n:2 * n]
        send, recv, lsem = refs[2 * n:]
        x, y, c = _me()
        sibling = (x, y, 1 - c)
        copies, sends = [], []
        for k in range(n):
            hr = p_refs[k].shape[0]
            mine = out_refs[k].at[pl.ds(c * hr, hr), :]
            copies.append(pltpu.make_async_copy(p_refs[k], mine, lsem.at[k]))
            sends.append(pltpu.make_async_remote_copy(
                src_ref=p_refs[k], dst_ref=mine, send_sem=send.at[k], recv_sem=recv.at[k],
                device_id=sibling, device_id_type=MESH))
        for cp in copies + sends:
            cp.start()
        for k in range(n):
            hr = p_refs[k].shape[0]
            theirs = out_refs[k].at[pl.ds((1 - c) * hr, hr), :]
            pltpu.make_async_remote_copy(
                src_ref=p_refs[k], dst_ref=theirs, send_sem=send.at[k], recv_sem=recv.at[k],
                device_id=sibling, device_id_type=MESH).wait_recv()
        for cp in sends:
            cp.wait_send()
        for cp in copies:
            cp.wait()

    out_shape = [jax.ShapeDtypeStruct((2 * p.shape[0], p.shape[1]), p.dtype) for p in pieces]
    return pl.pallas_call(
        body, name="join_halves", in_specs=[ANY] * n, out_specs=[ANY] * n, out_shape=out_shape,
        scratch_shapes=[pltpu.SemaphoreType.DMA((n,)), pltpu.SemaphoreType.DMA((n,)), pltpu.SemaphoreType.DMA((n,))],
        compiler_params=_params(),
    )(*pieces)


def _adamw(name, w, g, m, v, tr=256):
    R, C = w.shape
    tr = _tile(R, tr, 8)
    c1 = 1.0 - ADAM_B1 ** ADAM_STEP
    c2 = 1.0 - ADAM_B2 ** ADAM_STEP

    def body(w_ref, g_ref, m_ref, v_ref, d_ref, nm_ref, nv_ref):
        gv = g_ref[...]
        nm = ADAM_B1 * m_ref[...] + (1.0 - ADAM_B1) * gv
        nv = ADAM_B2 * v_ref[...] + (1.0 - ADAM_B2) * jnp.square(gv)
        d_ref[...] = -ADAM_LR * ((nm / c1) / (jnp.sqrt(nv / c2) + ADAM_EPS) + ADAM_WD * w_ref[...])
        nm_ref[...] = nm
        nv_ref[...] = nv

    spec = pl.BlockSpec((tr, C), lambda i: (i, 0))
    return pl.pallas_call(
        body, name=name, grid=(R // tr,), in_specs=[spec] * 4, out_specs=[spec] * 3,
        out_shape=[jax.ShapeDtypeStruct((R, C), F32)] * 3, compiler_params=_params(("parallel",)),
    )(w, g, m, v)


def kernel(x, p, g_mix, w_in, w_dw, b_dw, g_conv_ln, b_conv_ln, w_out, g_ffn, w_gate, w_up, w_down, g_ple, w_pgate, b_pgate, w_ple, g_final, loss_target, m_g_mix, m_w_in, m_w_dw, m_b_dw, m_g_conv_ln, m_b_conv_ln, m_w_out, m_g_ffn, m_w_gate, m_w_up, m_w_down, m_g_ple, m_w_pgate, m_b_pgate, m_w_ple, m_g_final, v_g_mix, v_w_in, v_w_dw, v_b_dw, v_g_conv_ln, v_b_conv_ln, v_w_out, v_g_ffn, v_w_gate, v_w_up, v_w_down, v_g_ple, v_w_pgate, v_b_pgate, v_w_ple, v_g_final):
    T, D = x.shape[1], x.shape[2]
    DC = b_dw.shape[1]
    DA = D - DC
    H = DA // HEAD_DIM
    DP = p.shape[3]
    assert T % SUPER == 0 and DA == DC
    x2 = x.reshape(T, D)
    p2 = p.reshape(T, DP)
    tgt = loss_target.reshape(T, D)
    g_final2 = g_final.reshape(1, D)

    big = dict(w_in=(w_in[0], 1), w_out=(w_out[0], 0), w_gate=(w_gate[0], 1), w_up=(w_up[0], 1),
               w_down=(w_down[0], 0), w_pgate=(w_pgate[0], 0), w_ple=(w_ple[0], 1))
    names = list(big)
    axes = [big[k][1] for k in names]
    dw_shard = jnp.pad(w_dw.reshape(CONV_W, -1), ((0, HALO - CONV_W), (0, 0)))
    gathered = _gather_weights([big[k][0].astype(BF16) for k in names], axes, dw_shard)
    W = dict(zip(names, gathered[:-1]))
    w_dw_full = gathered[-1]

    a = _rms_fwd("rms_mix", x2, g_mix)
    z = _matmul("mm_in", [[(a, W["w_in"])]], "nn", [F32], _plain)[0]
    cat, o_attn, lse = _attn_fwd(z, D, H)
    cat, conv_c = _conv_fwd(z, cat, w_dw_full, b_dw, g_conv_ln, b_conv_ln, 3 * DA // DC)
    h1 = _matmul("mm_out", [[(cat, W["w_out"])]], "nn", [F32], _add_residual, extras=[(x2, "mn")])[0]
    f = _rms_fwd("rms_ffn", h1, g_ffn)

    def swiglu(accs, extras):
        gt, up = accs
        return gt, up, (gt * jax.nn.sigmoid(gt)) * up

    gate, up, act = _matmul("mm_gate_up", [[(f, W["w_gate"])], [(f, W["w_up"])]], "nn", [BF16, BF16, BF16],
                            swiglu, tm=512, tn=512)
    h2 = _matmul("mm_down", [[(act, W["w_down"])]], "nn", [F32], _add_residual, extras=[(h1, "mn")], tk=1408)[0]
    n2 = _rms_fwd("rms_ple", h2, g_ple)
    e = _matmul("mm_ple", [[(p2, W["w_ple"])]], "nn", [F32], _plain)[0]

    def ple_gate(accs, extras):
        bias, hv, ev = extras
        gt = jax.nn.sigmoid(accs[0] + bias)
        return hv + ev * gt, gt, ev

    h3, gte, e16 = _matmul("mm_pgate", [[(n2, W["w_pgate"])]], "nn", [F32, BF16, BF16], ple_gate,
                           extras=[(b_pgate, "n"), (h2, "mn"), (e, "mn")], tm=512)

    dh3, de, du4, dg_final, db_pgate, loss_part = _loss_head(h3, tgt, g_final2, gte, e16)
    gw = {}
    gw["w_ple"] = _matmul("mm_dw_ple", [[(p2, de)]], "tn", [BF16], _plain, tk=1024)[0]
    gw["w_pgate"] = _matmul("mm_dw_pgate", [[(n2, du4)]], "tn", [BF16], _plain, tk=1024)[0]
    dn2 = _matmul("mm_dn2", [[(du4, W["w_pgate"])]], "nt", [F32], _plain)[0]
    dh2, dh2_16, dg_ple = _rms_bwd("rms_ple_bwd", dn2, h2, g_ple, dh3)
    gw["w_down"] = _matmul("mm_dw_down", [[(act, dh2_16)]], "tn", [BF16], _plain, tm=1408, tk=1024)[0]

    def swiglu_bwd(accs, extras):
        gt, up = extras[0].astype(F32), extras[1].astype(F32)
        sg = jax.nn.sigmoid(gt)
        dact = accs[0]
        return dact * up * (sg * (1.0 + gt * (1.0 - sg))), dact * (gt * sg)

    dgate, dup = _matmul("mm_dact", [[(dh2_16, W["w_down"])]], "nt", [BF16, BF16], swiglu_bwd,
                         extras=[(gate, "mn"), (up, "mn")], tm=512, tn=512)
    gw["w_gate"] = _matmul("mm_dw_gate", [[(f, dgate)]], "tn", [BF16], _plain, tn=1408, tk=1024)[0]
    gw["w_up"] = _matmul("mm_dw_up", [[(f, dup)]], "tn", [BF16], _plain, tn=1408, tk=1024)[0]
    df = _matmul("mm_df", [[(dgate, W["w_gate"]), (dup, W["w_up"])]], "nt", [F32], _plain, tk=1408)[0]
    dh1, dh1_16, dg_ffn = _rms_bwd("rms_ffn_bwd", df, h1, g_ffn, dh2)
    gw["w_out"] = _matmul("mm_dw_out", [[(cat, dh1_16)]], "tn", [BF16], _plain, tk=1024)[0]
    dcat = _matmul("mm_dcat", [[(dh1_16, W["w_out"])]], "nt", [F32], _plain)[0]
    dc, dg_ln, db_ln, db_dw = _conv_ln_bwd(dcat, conv_c, g_conv_ln, b_conv_ln)
    dcv, dcg, dw_dw = _conv_bwd(z, dc, w_dw_full, 3 * DA // DC)
    dq, dk, dv = _attn_bwd(z, dcat, o_attn, lse, H)
    dz = jnp.concatenate([dq, dk, dv, dcv, dcg], axis=1)
    gw["w_in"] = _matmul("mm_dw_in", [[(a, dz)]], "tn", [BF16], _plain, tk=1024)[0]
    da = _matmul("mm_da", [[(dz, W["w_in"])]], "nt", [F32], _plain, tk=1024)[0]
    grad_x, _, dg_mix = _rms_bwd("rms_mix_bwd", da, x2, g_mix, dh1)

    wide = [dg_mix, dg_ffn, dg_ple, db_pgate, dg_final,
            jnp.concatenate([db_dw, dg_ln], axis=1), jnp.concatenate([db_ln, jnp.zeros_like(db_ln)], axis=1),
            jnp.pad(loss_part, ((0, 0), (0, D - 128))),
            dw_dw.reshape(HALO * DC // D, D)]
    small = jnp.concatenate(wide, axis=0)
    n_small = small.shape[0]
    small = jnp.pad(small, ((0, -n_small % 8), (0, 0)))
    slots = _scatter_grads([gw[k] for k in names], axes, small)
    pieces = [_sum_slots("sum_" + k, s) for k, s in zip(names, slots[:-1])]
    small_sum = _sum_slots("sum_small", slots[-1])
    g_big = dict(zip(names, _join_halves(pieces, axes)))

    grads, deltas, new_m, new_v = {}, {}, {}, {}
    moments = dict(w_in=(m_w_in, v_w_in), w_out=(m_w_out, v_w_out), w_gate=(m_w_gate, v_w_gate),
                   w_up=(m_w_up, v_w_up), w_down=(m_w_down, v_w_down), w_pgate=(m_w_pgate, v_w_pgate),
                   w_ple=(m_w_ple, v_w_ple))
    for k in names:
        w_shard = big[k][0]
        d_, m_, v_ = _adamw("adamw_" + k, w_shard, g_big[k], moments[k][0][0], moments[k][1][0])
        grads[k], deltas[k], new_m[k], new_v[k] = g_big[k][None], d_[None], m_[None], v_[None]

    half = lambda r, lo: small_sum[r:r + 1, lo * DC:(lo + 1) * DC]
    vec = dict(g_mix=small_sum[0:1], g_ffn=small_sum[1:2], g_ple=small_sum[2:3], b_pgate=small_sum[3:4],
               g_final=small_sum[4:5], b_dw=half(5, 0), g_conv_ln=half(5, 1), b_conv_ln=half(6, 0))
    loss = small_sum[7, 0]
    dw_dw_sum = small_sum[8:8 + HALO * DC // D].reshape(HALO, DC)
    s_me = 2 * lax.axis_index("x") + lax.axis_index("y")
    cs = w_dw.shape[3]
    vec["w_dw"] = lax.dynamic_slice(dw_dw_sum, (0, s_me * cs), (CONV_W, cs))
    small_w = dict(g_mix=(g_mix, m_g_mix, v_g_mix), g_ffn=(g_ffn, m_g_ffn, v_g_ffn), g_ple=(g_ple, m_g_ple, v_g_ple),
                   b_pgate=(b_pgate, m_b_pgate, v_b_pgate), g_final=(g_final, m_g_final, v_g_final),
                   b_dw=(b_dw, m_b_dw, v_b_dw), g_conv_ln=(g_conv_ln, m_g_conv_ln, v_g_conv_ln),
                   b_conv_ln=(b_conv_ln, m_b_conv_ln, v_b_conv_ln), w_dw=(w_dw, m_w_dw, v_w_dw))
    for k, (w_, m_, v_) in small_w.items():
        shape = w_.shape
        g2 = vec[k]
        to2 = lambda t: t.reshape(g2.shape)
        d_, nm_, nv_ = _adamw("adamw_" + k, to2(w_), g2, to2(m_), to2(v_))
        grads[k], deltas[k], new_m[k], new_v[k] = (t.reshape(shape) for t in (g2, d_, nm_, nv_))

    order = ["g_mix", "w_in", "w_dw", "b_dw", "g_conv_ln", "b_conv_ln", "w_out", "g_ffn", "w_gate", "w_up", "w_down",
             "g_ple", "w_pgate", "b_pgate", "w_ple", "g_final"]
    return (loss, grad_x.reshape(x.shape), *[grads[k] for k in order], *[deltas[k] for k in order],
            *[new_m[k] for k in order], *[new_v[k] for k in order])
```

```python
import functools

import jax
import jax.numpy as jnp
from jax import lax
from jax.experimental import pallas as pl
from jax.experimental.pallas import tpu as pltpu

F32 = jnp.float32
BF16 = jnp.bfloat16

EPS = 1e-6
HEAD_DIM = 128
BLK = 128
DILATIONS = (1, 4, 16)
SUPER = BLK * DILATIONS[-1]
CONV_W = 31
HALO = 32
ADAM_LR, ADAM_B1, ADAM_B2, ADAM_EPS, ADAM_WD, ADAM_STEP = 0.001, 0.9, 0.999, 1e-08, 0.01, 10

V7X_VMEM_BYTES = 64 * 1024 * 1024
VMEM_LIMIT = V7X_VMEM_BYTES * 3 // 4
MESH = pl.DeviceIdType.MESH
ANY = pl.BlockSpec(memory_space=pl.ANY)
HBM = pl.BlockSpec(memory_space=pltpu.HBM)
SEM = pl.BlockSpec(memory_space=pltpu.SEMAPHORE)
EFFECT = pltpu.SideEffectType.DATAFLOW_SIDE_EFFECTING
TOKEN = (8, 128)


def _params(semantics=None, **kw):
    return pltpu.CompilerParams(dimension_semantics=semantics, vmem_limit_bytes=VMEM_LIMIT, **kw)


def _tile(n, want, mult=128):
    if n <= want:
        return n
    for t in range(want - want % mult, 0, -mult):
        if n % t == 0:
            return t
    raise ValueError((n, want, mult))


_DIMS = {"nn": ((1,), (0,)), "nt": ((1,), (1,)), "tn": ((0,), (0,))}


def _matmul(name, groups, mode, out_dtypes, epilogue, extras=(), tm=1024, tn=1024, tk=2048, deps=()):
    a0, b0 = groups[0][0]
    if mode == "nn":
        (M, K), N = a0.shape, b0.shape[1]
    elif mode == "nt":
        (M, K), N = a0.shape, b0.shape[0]
    else:
        (K, M), N = a0.shape, b0.shape[1]
    tm, tn, tk = _tile(M, tm), _tile(N, tn), _tile(K, tk)
    nk = K // tk
    if mode == "tn":
        a_spec = pl.BlockSpec((tk, tm), lambda i, j, k: (k, i))
    else:
        a_spec = pl.BlockSpec((tm, tk), lambda i, j, k: (i, k))
    if mode == "nt":
        b_spec = pl.BlockSpec((tn, tk), lambda i, j, k: (j, k))
    else:
        b_spec = pl.BlockSpec((tk, tn), lambda i, j, k: (k, j))
    operands, in_specs = [], []
    for grp in groups:
        for a, b in grp:
            operands += [a, b]
            in_specs += [a_spec, b_spec]
    for arr, kind in extras:
        operands.append(arr)
        if kind == "mn":
            in_specs.append(pl.BlockSpec((tm, tn), lambda i, j, k: (i, j)))
        else:
            in_specs.append(pl.BlockSpec((1, tn), lambda i, j, k: (0, j)))
    for tok in deps:
        operands.append(tok)
        in_specs.append(pl.BlockSpec(TOKEN, lambda i, j, k: (0, 0)))
    n_pairs = [len(g) for g in groups]
    n_ex, n_out, n_grp, n_dep = len(extras), len(out_dtypes), len(groups), len(deps)
    out_shape = [jax.ShapeDtypeStruct((M, N), dt) for dt in out_dtypes]
    out_specs = [pl.BlockSpec((tm, tn), lambda i, j, k: (i, j)) for _ in out_dtypes]

    def body(*refs):
        pos = 0
        parts = []
        for g in range(n_grp):
            part = None
            for _ in range(n_pairs[g]):
                a_ref, b_ref = refs[pos], refs[pos + 1]
                pos += 2
                d = lax.dot_general(a_ref[...].astype(BF16), b_ref[...].astype(BF16),
                                    (_DIMS[mode], ((), ())), preferred_element_type=F32)
                part = d if part is None else part + d
            parts.append(part)
        ex_refs = refs[pos:pos + n_ex]
        pos += n_ex + n_dep
        out_refs = refs[pos:pos + n_out]
        acc_refs = refs[pos + n_out:]

        def finish(accs):
            outs = epilogue(accs, [e[...] for e in ex_refs])
            for o_ref, o in zip(out_refs, outs):
                o_ref[...] = o.astype(o_ref.dtype)

        if nk == 1:
            finish(parts)
        else:
            k = pl.program_id(2)

            @pl.when(k == 0)
            def _():
                for acc, part in zip(acc_refs, parts):
                    acc[...] = part

            @pl.when(k > 0)
            def _():
                for acc, part in zip(acc_refs, parts):
                    acc[...] += part

            @pl.when(k == nk - 1)
            def _():
                finish([acc[...] for acc in acc_refs])

    scratch = [pltpu.VMEM((tm, tn), F32) for _ in range(n_grp)] if nk > 1 else []
    return pl.pallas_call(
        body, name=name, grid=(M // tm, N // tn, nk),
        in_specs=in_specs, out_specs=out_specs, out_shape=out_shape, scratch_shapes=scratch,
        compiler_params=_params(("parallel", "parallel", "arbitrary")),
    )(*operands)


def _plain(accs, extras):
    return (accs[0],)


def _add_residual(accs, extras):
    return (accs[0] + extras[0],)


def _rowwise(name, fn, rows, consts, row_outs, acc_outs, tb, deps=()):
    T = rows[0][0].shape[0]
    tb = _tile(T, tb, 16)
    operands = [r[0] for r in rows] + list(consts) + list(deps)
    in_specs = [pl.BlockSpec((tb, c), functools.partial(lambda i, cb: (i, cb), cb=cb)) for _, c, cb in rows]
    in_specs += [pl.BlockSpec(c.shape, functools.partial(lambda i, nd: (0,) * nd, nd=c.ndim)) for c in consts]
    in_specs += [pl.BlockSpec(TOKEN, lambda i: (0, 0)) for _ in deps]
    out_shape = [jax.ShapeDtypeStruct(s, dt) for s, dt, _, _ in row_outs]
    out_specs = [pl.BlockSpec((tb, c), functools.partial(lambda i, cb: (i, cb), cb=cb)) for _, _, c, cb in row_outs]
    out_shape += [jax.ShapeDtypeStruct(s, F32) for s in acc_outs]
    out_specs += [pl.BlockSpec(s, functools.partial(lambda i, nd: (0,) * nd, nd=len(s))) for s in acc_outs]
    n_rows, n_consts, n_ro, n_dep = len(rows), len(consts), len(row_outs), len(deps)

    def body(*refs):
        row_refs = refs[:n_rows]
        const_refs = refs[n_rows:n_rows + n_consts]
        out_refs = refs[n_rows + n_consts + n_dep:]
        ro, ao = fn([r[...] for r in row_refs], [c[...] for c in const_refs])
        for o_ref, o in zip(out_refs[:n_ro], ro):
            o_ref[...] = o.astype(o_ref.dtype)
        if acc_outs:
            i = pl.program_id(0)

            @pl.when(i == 0)
            def _():
                for a_ref, a in zip(out_refs[n_ro:], ao):
                    a_ref[...] = a

            @pl.when(i > 0)
            def _():
                for a_ref, a in zip(out_refs[n_ro:], ao):
                    a_ref[...] += a

    return pl.pallas_call(
        body, name=name, grid=(T // tb,), in_specs=in_specs, out_specs=out_specs, out_shape=out_shape,
        compiler_params=_params(("arbitrary",) if acc_outs else ("parallel",)),
    )(*operands)


def _colsum(v):
    return jnp.sum(v, axis=0, keepdims=True)


def _rms_fwd(name, x, g, tb=512, deps=()):
    T, D = x.shape

    def fn(rows, consts):
        xv, gv = rows[0], consts[0]
        r = lax.rsqrt(jnp.mean(xv * xv, axis=-1, keepdims=True) + EPS)
        return ((xv * r) * gv,), ()

    return _rowwise(name, fn, [(x, D, 0)], [g], [((T, D), BF16, D, 0)], [], tb, deps=deps)[0]


def _rms_bwd(name, dy, x, g, resid, tb=256):
    T, D = x.shape

    def fn(rows, consts):
        dyv, xv, rv = rows
        gv = consts[0]
        r = lax.rsqrt(jnp.mean(xv * xv, axis=-1, keepdims=True) + EPS)
        n = xv * r
        dn = dyv * gv
        dx = r * (dn - n * jnp.mean(dn * n, axis=-1, keepdims=True))
        tot = rv + dx
        return (tot, tot), (_colsum(dyv * n),)

    return _rowwise(name, fn, [(dy, D, 0), (x, D, 0), (resid, D, 0)], [g],
                    [((T, D), F32, D, 0), ((T, D), BF16, D, 0)], [(1, D)], tb)


def _tri_masks():
    qi = lax.broadcasted_iota(jnp.int32, (BLK, BLK), 0)
    kj = lax.broadcasted_iota(jnp.int32, (BLK, BLK), 1)
    return kj <= qi, kj >= qi


def _dot_nt(a, b):
    return lax.dot_general(a, b, (((1,), (1,)), ((), ())), preferred_element_type=F32)


def _dot_tn(a, b):
    return lax.dot_general(a, b, (((0,), (0,)), ((), ())), preferred_element_type=F32)


def _dot_nn(a, b):
    return lax.dot_general(a, b, (((1,), (0,)), ((), ())), preferred_element_type=F32)


def _attn_fwd(z, cat_width, n_heads):
    T = z.shape[0]
    H = n_heads
    DA = H * HEAD_DIM
    nb = T // SUPER
    scale = HEAD_DIM ** -0.5

    def body(q_ref, k_ref, v_ref, kp_ref, vp_ref, cat_ref, o_ref, lse_ref, ob, lb):
        n = pl.program_id(1)
        own_mask, prev_mask = _tri_masks()
        has_prev = n > 0
        for b, dil in enumerate(DILATIONS):
            for r in range(dil):
                for j in range(SUPER // (BLK * dil)):
                    rows = pl.ds(j * BLK * dil + r, BLK, stride=dil)
                    qb = q_ref[rows, :].astype(BF16)
                    kb = k_ref[rows, :].astype(BF16)
                    vb = v_ref[rows, :].astype(BF16)
                    if j > 0:
                        prows = pl.ds((j - 1) * BLK * dil + r, BLK, stride=dil)
                        kpb = k_ref[prows, :].astype(BF16)
                        vpb = v_ref[prows, :].astype(BF16)
                        pmask = prev_mask
                    else:
                        prows = pl.ds(SUPER - BLK * dil + r, BLK, stride=dil)
                        kpb = kp_ref[prows, :].astype(BF16)
                        vpb = vp_ref[prows, :].astype(BF16)
                        pmask = jnp.logical_and(prev_mask, has_prev)
                    so = jnp.where(own_mask, _dot_nt(qb, kb) * scale, -jnp.inf)
                    sp = jnp.where(pmask, _dot_nt(qb, kpb) * scale, -jnp.inf)
                    m = jnp.maximum(jnp.max(so, axis=-1, keepdims=True), jnp.max(sp, axis=-1, keepdims=True))
                    eo = jnp.exp(so - m)
                    ep = jnp.exp(sp - m)
                    den = jnp.sum(eo, axis=-1, keepdims=True) + jnp.sum(ep, axis=-1, keepdims=True)
                    o = _dot_nn((eo / den).astype(BF16), vb) + _dot_nn((ep / den).astype(BF16), vpb)
                    ob[b, rows, :] = o
                    lb[b, rows, :] = jnp.broadcast_to(m + jnp.log(den), (BLK, HEAD_DIM))
        l0, l1, l2 = lb[0], lb[1], lb[2]
        mx = jnp.maximum(jnp.maximum(l0, l1), l2)
        tot = mx + jnp.log(jnp.exp(l0 - mx) + jnp.exp(l1 - mx) + jnp.exp(l2 - mx))
        o = jnp.exp(l0 - tot) * ob[0] + jnp.exp(l1 - tot) * ob[1] + jnp.exp(l2 - tot) * ob[2]
        o_ref[...] = o
        cat_ref[...] = o.astype(BF16)
        lse_ref[...] = tot

    blk = (SUPER, HEAD_DIM)
    in_specs = [
        pl.BlockSpec(blk, lambda h, n: (n, h)),
        pl.BlockSpec(blk, lambda h, n: (n, H + h)),
        pl.BlockSpec(blk, lambda h, n: (n, 2 * H + h)),
        pl.BlockSpec(blk, lambda h, n: (jnp.maximum(n - 1, 0), H + h)),
        pl.BlockSpec(blk, lambda h, n: (jnp.maximum(n - 1, 0), 2 * H + h)),
    ]
    out_spec = pl.BlockSpec(blk, lambda h, n: (n, h))
    return pl.pallas_call(
        body, name="attn_fwd", grid=(H, nb), in_specs=in_specs, out_specs=[out_spec] * 3,
        out_shape=[jax.ShapeDtypeStruct((T, cat_width), BF16), jax.ShapeDtypeStruct((T, DA), F32),
                   jax.ShapeDtypeStruct((T, DA), F32)],
        scratch_shapes=[pltpu.VMEM((3, SUPER, HEAD_DIM), F32), pltpu.VMEM((3, SUPER, HEAD_DIM), F32)],
        compiler_params=_params(("parallel", "parallel")),
    )(z, z, z, z, z)


def _attn_bwd(z, dcat, o, lse, n_heads):
    T = z.shape[0]
    H = n_heads
    nb = T // SUPER
    scale = HEAD_DIM ** -0.5

    def body(q_ref, k_ref, v_ref, kp_ref, vp_ref, do_ref, o_ref, lse_ref, dz_q, dz_k, dz_v,
             dq_acc, dk_acc, dv_acc, dkp_acc, dvp_acc, dsum):
        i = pl.program_id(1)
        own_mask, prev_mask = _tri_masks()
        has_prev = i < nb - 1

        @pl.when(i == 0)
        def _():
            dk_acc[...] = jnp.zeros_like(dk_acc)
            dv_acc[...] = jnp.zeros_like(dv_acc)

        @pl.when(i > 0)
        def _():
            dk_acc[...] = dkp_acc[...]
            dv_acc[...] = dvp_acc[...]

        dq_acc[...] = jnp.zeros_like(dq_acc)
        dkp_acc[...] = jnp.zeros_like(dkp_acc)
        dvp_acc[...] = jnp.zeros_like(dvp_acc)
        dsum[...] = jnp.broadcast_to(jnp.sum(do_ref[...] * o_ref[...], axis=-1, keepdims=True), (SUPER, HEAD_DIM))
        for dil in DILATIONS:
            for r in range(dil):
                for j in range(SUPER // (BLK * dil)):
                    rows = pl.ds(j * BLK * dil + r, BLK, stride=dil)
                    qb = q_ref[rows, :].astype(BF16)
                    kb = k_ref[rows, :].astype(BF16)
                    vb = v_ref[rows, :].astype(BF16)
                    dob = do_ref[rows, :].astype(BF16)
                    lseb = lse_ref[rows, :]
                    db = dsum[rows, :]
                    if j > 0:
                        prows = pl.ds((j - 1) * BLK * dil + r, BLK, stride=dil)
                        kpb = k_ref[prows, :].astype(BF16)
                        vpb = v_ref[prows, :].astype(BF16)
                        pmask = prev_mask
                        dk_dst, dv_dst = dk_acc, dv_acc
                    else:
                        prows = pl.ds(SUPER - BLK * dil + r, BLK, stride=dil)
                        kpb = kp_ref[prows, :].astype(BF16)
                        vpb = vp_ref[prows, :].astype(BF16)
                        pmask = jnp.logical_and(prev_mask, has_prev)
                        dk_dst, dv_dst = dkp_acc, dvp_acc
                    po = jnp.where(own_mask, jnp.exp(_dot_nt(qb, kb) * scale - lseb), 0.0)
                    pp = jnp.where(pmask, jnp.exp(_dot_nt(qb, kpb) * scale - lseb), 0.0)
                    dso = (po * (_dot_nt(dob, vb) - db) * scale).astype(BF16)
                    dsp = (pp * (_dot_nt(dob, vpb) - db) * scale).astype(BF16)
                    dq_acc[rows, :] += _dot_nn(dso, kb) + _dot_nn(dsp, kpb)
                    dk_acc[rows, :] += _dot_tn(dso, qb)
                    dv_acc[rows, :] += _dot_tn(po.astype(BF16), dob)
                    dk_dst[prows, :] += _dot_tn(dsp, qb)
                    dv_dst[prows, :] += _dot_tn(pp.astype(BF16), dob)
        dz_q[...] = dq_acc[...].astype(BF16)
        dz_k[...] = dk_acc[...].astype(BF16)
        dz_v[...] = dv_acc[...].astype(BF16)

    blk = (SUPER, HEAD_DIM)
    row = lambda i: nb - 1 - i
    in_specs = [
        pl.BlockSpec(blk, lambda h, i: (row(i), h)),
        pl.BlockSpec(blk, lambda h, i: (row(i), H + h)),
        pl.BlockSpec(blk, lambda h, i: (row(i), 2 * H + h)),
        pl.BlockSpec(blk, lambda h, i: (jnp.maximum(row(i) - 1, 0), H + h)),
        pl.BlockSpec(blk, lambda h, i: (jnp.maximum(row(i) - 1, 0), 2 * H + h)),
        pl.BlockSpec(blk, lambda h, i: (row(i), h)),
        pl.BlockSpec(blk, lambda h, i: (row(i), h)),
        pl.BlockSpec(blk, lambda h, i: (row(i), h)),
    ]
    out_spec = pl.BlockSpec(blk, lambda h, i: (row(i), h))
    return pl.pallas_call(
        body, name="attn_bwd", grid=(H, nb), in_specs=in_specs, out_specs=[out_spec] * 3,
        out_shape=[jax.ShapeDtypeStruct((T, H * HEAD_DIM), BF16)] * 3,
        scratch_shapes=[pltpu.VMEM(blk, F32) for _ in range(6)],
        compiler_params=_params(("parallel", "arbitrary")),
    )(z, z, z, z, z, dcat, o, lse)


def _glu(cv, cg):
    return cv * jax.nn.sigmoid(cg)


def _conv_fwd(z, cat, w_dw, b_dw, g_ln, b_ln, col0, tb=512):
    T = z.shape[0]
    DC = w_dw.shape[1]
    tb = _tile(T, tb, HALO)
    hb = tb // HALO
    cat_cb = cat.shape[1] // DC - 1

    def body(cv_ref, cg_ref, cvh_ref, cgh_ref, w_ref, bdw_ref, g_ref, b_ref, cat_in, cat_ref, c_ref, u_ext):
        i = pl.program_id(0)
        halo = _glu(cvh_ref[...], cgh_ref[...])
        u_ext[pl.ds(0, HALO), :] = jnp.where(i > 0, halo, 0.0)
        u_ext[pl.ds(HALO, tb), :] = _glu(cv_ref[...], cg_ref[...])
        acc = jnp.broadcast_to(bdw_ref[...], (tb, DC))
        for j in range(CONV_W):
            acc = acc + w_ref[pl.ds(j, 1), :] * u_ext[pl.ds(HALO - (CONV_W - 1) + j, tb), :]
        c_ref[...] = acc
        mu = jnp.mean(acc, axis=-1, keepdims=True)
        var = jnp.mean(jnp.square(acc - mu), axis=-1, keepdims=True)
        y = (acc - mu) * lax.rsqrt(var + EPS) * g_ref[...] + b_ref[...]
        cat_ref[...] = (y * jax.nn.sigmoid(y)).astype(BF16)

    cur = lambda cb: pl.BlockSpec((tb, DC), lambda i: (i, cb))
    halo = lambda cb: pl.BlockSpec((HALO, DC), lambda i: (jnp.maximum(i * hb - 1, 0), cb))
    whole = lambda a: pl.BlockSpec(a.shape, lambda i: (0, 0))
    return pl.pallas_call(
        body, name="conv_fwd", grid=(T // tb,),
        in_specs=[cur(col0), cur(col0 + 1), halo(col0), halo(col0 + 1), whole(w_dw), whole(b_dw), whole(g_ln),
                  whole(b_ln), ANY],
        out_specs=[pl.BlockSpec((tb, DC), lambda i: (i, cat_cb)), pl.BlockSpec((tb, DC), lambda i: (i, 0))],
        out_shape=[jax.ShapeDtypeStruct(cat.shape, cat.dtype), jax.ShapeDtypeStruct((T, DC), F32)],
        scratch_shapes=[pltpu.VMEM((tb + HALO, DC), F32)],
        input_output_aliases={8: 0},
        compiler_params=_params(("parallel",)),
    )(z, z, z, z, w_dw, b_dw, g_ln, b_ln, cat)


def _conv_ln_bwd(dcat, c, g_ln, b_ln, tb=256, deps=()):
    T, DC = c.shape
    d_cb = dcat.shape[1] // DC - 1

    def fn(rows, consts):
        dov, cv_ = rows
        gv, bv = consts
        mu = jnp.mean(cv_, axis=-1, keepdims=True)
        xc = cv_ - mu
        rstd = lax.rsqrt(jnp.mean(jnp.square(xc), axis=-1, keepdims=True) + EPS)
        ln = xc * rstd
        y = ln * gv + bv
        sg = jax.nn.sigmoid(y)
        dy = dov * (sg * (1.0 + y * (1.0 - sg)))
        dln = dy * gv
        dc = rstd * (dln - jnp.mean(dln, axis=-1, keepdims=True) - ln * jnp.mean(dln * ln, axis=-1, keepdims=True))
        return (dc,), (_colsum(dy * ln), _colsum(dy), _colsum(dc))

    return _rowwise("conv_ln_bwd", fn, [(dcat, DC, d_cb), (c, DC, 0)], [g_ln, b_ln],
                    [((T, DC), F32, DC, 0)], [(1, DC)] * 3, tb, deps=deps)


def _conv_bwd(z, dc, w_dw, col0, tb=512):
    T, DC = dc.shape
    tb = _tile(T, tb, HALO)
    hb = tb // HALO
    nblk = T // tb

    def body(cv_ref, cg_ref, cvh_ref, cgh_ref, dc_ref, dcn_ref, w_ref, dcv_ref, dcg_ref, dw_ref, u_ext, dc_ext):
        i = pl.program_id(0)
        cv, cg = cv_ref[...], cg_ref[...]
        sg = jax.nn.sigmoid(cg)
        u_ext[pl.ds(0, HALO), :] = jnp.where(i > 0, _glu(cvh_ref[...], cgh_ref[...]), 0.0)
        u_ext[pl.ds(HALO, tb), :] = cv * sg
        dcv_ = dc_ref[...]
        dc_ext[pl.ds(0, tb), :] = dcv_
        dc_ext[pl.ds(tb, HALO), :] = jnp.where(i < nblk - 1, dcn_ref[...], 0.0)

        @pl.when(i == 0)
        def _():
            dw_ref[...] = jnp.zeros_like(dw_ref)

        du = jnp.zeros((tb, DC), F32)
        for j in range(CONV_W):
            du = du + w_ref[pl.ds(j, 1), :] * dc_ext[pl.ds(CONV_W - 1 - j, tb), :]
            dw_ref[pl.ds(j, 1), :] += _colsum(u_ext[pl.ds(HALO - (CONV_W - 1) + j, tb), :] * dcv_)
        dcv_ref[...] = (du * sg).astype(BF16)
        dcg_ref[...] = (du * cv * sg * (1.0 - sg)).astype(BF16)

    cur = lambda cb: pl.BlockSpec((tb, DC), lambda i: (i, cb))
    halo = lambda cb: pl.BlockSpec((HALO, DC), lambda i: (jnp.maximum(i * hb - 1, 0), cb))
    nxt = pl.BlockSpec((HALO, DC), lambda i: (jnp.minimum((i + 1) * hb, T // HALO - 1), 0))
    return pl.pallas_call(
        body, name="conv_bwd", grid=(nblk,),
        in_specs=[cur(col0), cur(col0 + 1), halo(col0), halo(col0 + 1), cur(0), nxt,
                  pl.BlockSpec(w_dw.shape, lambda i: (0, 0))],
        out_specs=[cur(0), cur(0), pl.BlockSpec((HALO, DC), lambda i: (0, 0))],
        out_shape=[jax.ShapeDtypeStruct((T, DC), BF16), jax.ShapeDtypeStruct((T, DC), BF16),
                   jax.ShapeDtypeStruct((HALO, DC), F32)],
        scratch_shapes=[pltpu.VMEM((tb + HALO, DC), F32), pltpu.VMEM((tb + HALO, DC), F32)],
        compiler_params=_params(("arbitrary",)),
    )(z, z, z, z, dc, dc, w_dw)


def _loss_head(h3, target, g_final, gte, e, tb=256):
    T, D = h3.shape

    def fn(rows, consts):
        hv, tv, gt, ev = rows
        gv = consts[0]
        gt = gt.astype(F32)
        ev = ev.astype(F32)
        r = lax.rsqrt(jnp.mean(hv * hv, axis=-1, keepdims=True) + EPS)
        n = hv * r
        diff = n * gv - tv
        loss = 0.5 * jnp.sum(jnp.mean(jnp.square(diff), axis=-1, keepdims=True), axis=0, keepdims=True)
        dy = diff * (1.0 / D)
        dn = dy * gv
        dh = r * (dn - n * jnp.mean(dn * n, axis=-1, keepdims=True))
        du4 = dh * ev * gt * (1.0 - gt)
        return (dh, dh * gt, du4), (_colsum(dy * n), _colsum(du4), jnp.broadcast_to(loss, (1, 128)))

    return _rowwise("loss_head", fn, [(h3, D, 0), (target, D, 0), (gte, D, 0), (e, D, 0)], [g_final],
                    [((T, D), F32, D, 0), ((T, D), BF16, D, 0), ((T, D), BF16, D, 0)],
                    [(1, D), (1, D), (1, 128)], tb)


def _me():
    return lax.axis_index("x"), lax.axis_index("y"), lax.axis_index("c")


def _chips3(x, y):
    return [(1 - x, y), (x, 1 - y), (1 - x, 1 - y)]


def _peers7(x, y, c):
    for m in range(1, 8):
        yield m - 1, (x ^ (m >> 2), y ^ ((m >> 1) & 1), c ^ (m & 1))


def _shard_of(ref, axis, s):
    R, C = ref.shape
    if axis == 1:
        return ref.at[:, pl.ds(s * (C // 4), C // 4)]
    return ref.at[pl.ds(s * (R // 4), R // 4), :]


def _region(ref, axis, shard, half):
    R, C = ref.shape
    if axis == 1:
        cs, hr = C // 4, R // 2
        return ref.at[pl.ds(half * hr, hr), pl.ds(shard * cs, cs)]
    hr = R // 8
    return ref.at[pl.ds(shard * 2 * hr + half * hr, hr), :]


def _gather_now(shard, small):
    R, C = shard.shape
    cs = small.shape[1]

    def body(shard_ref, small_ref, out_ref, small_out, send, recv, fsend, frecv, lsem):
        x, y, c = _me()
        me_s = 2 * x + y
        sibling = (x, y, 1 - c)
        chips = _chips3(x, y)
        half = shard_ref.at[pl.ds(c * (R // 2), R // 2), :]
        locals_ = [pltpu.make_async_copy(shard_ref, _shard_of(out_ref, 1, me_s), lsem.at[0]),
                   pltpu.make_async_copy(small_ref, _shard_of(small_out, 1, me_s), lsem.at[1])]
        for cp in locals_:
            cp.start()
        firsts = []
        for j, (px, py) in enumerate(chips):
            firsts.append(pltpu.make_async_remote_copy(
                src_ref=half, dst_ref=_region(out_ref, 1, me_s, c), send_sem=send.at[0, j], recv_sem=recv.at[0, j],
                device_id=(px, py, c), device_id_type=MESH))
            firsts.append(pltpu.make_async_remote_copy(
                src_ref=small_ref, dst_ref=_shard_of(small_out, 1, me_s), send_sem=send.at[1, j],
                recv_sem=recv.at[1, j], device_id=(px, py, c), device_id_type=MESH))
        for cp in firsts:
            cp.start()
        relays = []
        for j, (px, py) in enumerate(chips):
            landed = _region(out_ref, 1, 2 * px + py, c)
            pltpu.make_async_remote_copy(
                src_ref=half, dst_ref=landed, send_sem=send.at[0, j], recv_sem=recv.at[0, j],
                device_id=(px, py, c), device_id_type=MESH).wait_recv()
            relay = pltpu.make_async_remote_copy(
                src_ref=landed, dst_ref=landed, send_sem=fsend.at[j], recv_sem=frecv.at[j],
                device_id=sibling, device_id_type=MESH)
            relay.start()
            relays.append(relay)
        for j, (px, py) in enumerate(chips):
            pltpu.make_async_remote_copy(
                src_ref=small_ref, dst_ref=_shard_of(small_out, 1, 2 * px + py), send_sem=send.at[1, j],
                recv_sem=recv.at[1, j], device_id=(px, py, c), device_id_type=MESH).wait_recv()
            theirs = _region(out_ref, 1, 2 * px + py, 1 - c)
            pltpu.make_async_remote_copy(
                src_ref=theirs, dst_ref=theirs, send_sem=fsend.at[j], recv_sem=frecv.at[j],
                device_id=sibling, device_id_type=MESH).wait_recv()
        for cp in firsts + relays:
            cp.wait_send()
        for cp in locals_:
            cp.wait()

    return pl.pallas_call(
        body, name="gather_now", in_specs=[ANY, ANY], out_specs=[ANY, ANY],
        out_shape=[jax.ShapeDtypeStruct((R, 4 * C), shard.dtype),
                   jax.ShapeDtypeStruct((small.shape[0], 4 * cs), small.dtype)],
        scratch_shapes=[pltpu.SemaphoreType.DMA((2, 3)), pltpu.SemaphoreType.DMA((2, 3)),
                        pltpu.SemaphoreType.DMA((3,)), pltpu.SemaphoreType.DMA((3,)), pltpu.SemaphoreType.DMA((2,))],
        compiler_params=_params(),
    )(shard, small)


def _exchange_small(small):
    def body(small_ref, out_ref, send, recv, lsem):
        x, y, c = _me()
        me = 4 * x + 2 * y + c
        own = pltpu.make_async_copy(small_ref, out_ref.at[me], lsem)
        own.start()
        sends = [pltpu.make_async_remote_copy(
            src_ref=small_ref, dst_ref=out_ref.at[me], send_sem=send.at[m], recv_sem=recv.at[m],
            device_id=peer, device_id_type=MESH) for m, peer in _peers7(x, y, c)]
        for cp in sends:
            cp.start()
        for m, (px, py, pc) in _peers7(x, y, c):
            pltpu.make_async_remote_copy(
                src_ref=small_ref, dst_ref=out_ref.at[4 * px + 2 * py + pc], send_sem=send.at[m], recv_sem=recv.at[m],
                device_id=(px, py, pc), device_id_type=MESH).wait_recv()
        for cp in sends:
            cp.wait_send()
        own.wait()

    return pl.pallas_call(
        body, name="exchange_small", in_specs=[ANY], out_specs=ANY,
        out_shape=jax.ShapeDtypeStruct((8,) + small.shape, small.dtype),
        scratch_shapes=[pltpu.SemaphoreType.DMA((7,)), pltpu.SemaphoreType.DMA((7,)), pltpu.SemaphoreType.DMA(())],
        compiler_params=_params(),
    )(small)


def _sum_slots(name, slots, tr=256):
    S, R, C = slots.shape
    tr = _tile(R, tr, 16)

    def body(s_ref, o_ref):
        acc = s_ref[0].astype(F32)
        for s in range(1, S):
            acc = acc + s_ref[s].astype(F32)
        o_ref[...] = acc

    return pl.pallas_call(
        body, name=name, grid=(R // tr,), in_specs=[pl.BlockSpec((S, tr, C), lambda i: (0, i, 0))],
        out_specs=pl.BlockSpec((tr, C), lambda i: (i, 0)), out_shape=jax.ShapeDtypeStruct((R, C), F32),
        compiler_params=_params(("parallel",)),
    )(slots)


def _gather_start(name, shard, axis):
    R, C = shard.shape
    s_me = 2 * lax.axis_index("x") + lax.axis_index("y")
    full_shape, off = ((R, 4 * C), (0, s_me * C)) if axis == 1 else ((4 * R, C), (s_me * R, 0))
    full = lax.dynamic_update_slice(lax.empty(full_shape, shard.dtype), shard, off)

    def body(shard_ref, full_ref, send, recv, shard_thru, full_thru, token):
        x, y, c = _me()
        for j, (px, py) in enumerate(_chips3(x, y)):
            pltpu.make_async_remote_copy(
                src_ref=shard_ref, dst_ref=_shard_of(full_ref, axis, 2 * x + y), send_sem=send.at[j],
                recv_sem=recv.at[j], device_id=(px, py, c), device_id_type=MESH).start()
        token[...] = jnp.zeros_like(token)

    return pl.pallas_call(
        body, name=name,
        out_shape=(pltpu.SemaphoreType.DMA((3,)), pltpu.SemaphoreType.DMA((3,)), pltpu.HBM(shard.shape, shard.dtype),
                   pltpu.HBM(full_shape, shard.dtype), jax.ShapeDtypeStruct(TOKEN, F32)),
        in_specs=(HBM, HBM), out_specs=(SEM, SEM, HBM, HBM, pl.BlockSpec(memory_space=pltpu.VMEM)),
        input_output_aliases={0: 2, 1: 3}, compiler_params=pltpu.CompilerParams(has_side_effects=EFFECT),
    )(pltpu.with_memory_space_constraint(shard, pltpu.HBM), pltpu.with_memory_space_constraint(full, pltpu.HBM))


def _gather_wait(name, started, axis, after):
    send, recv, shard_thru, full_thru, _ = started

    def body(shard_ref, full_ref, send, recv, after_ref, shard_dead, full_out):
        x, y, c = _me()
        for j, (px, py) in enumerate(_chips3(x, y)):
            cp = pltpu.make_async_remote_copy(
                src_ref=shard_ref, dst_ref=_shard_of(full_ref, axis, 2 * px + py), send_sem=send.at[j],
                recv_sem=recv.at[j], device_id=(px, py, c), device_id_type=MESH)
            cp.wait_send()
            cp.wait_recv()

    return pl.pallas_call(
        body, name=name,
        out_shape=(pltpu.HBM(shard_thru.shape, shard_thru.dtype), pltpu.HBM(full_thru.shape, full_thru.dtype)),
        in_specs=(HBM, HBM, SEM, SEM, ANY), out_specs=(HBM, HBM), input_output_aliases={0: 0, 1: 1},
        compiler_params=pltpu.CompilerParams(has_side_effects=EFFECT),
    )(shard_thru, full_thru, send, recv, after)[1]


def _piece_shape(shape, axis):
    R, C = shape
    return (R // 2, C // 4) if axis == 1 else (R // 8, C)


def _scatter_start(name, g, axis):
    land_shape = (7,) + _piece_shape(g.shape, axis)

    def body(g_ref, land_ref, send, recv, g_thru, land_thru, token):
        x, y, c = _me()
        for m, (px, py, pc) in _peers7(x, y, c):
            pltpu.make_async_remote_copy(
                src_ref=_region(g_ref, axis, 2 * px + py, pc), dst_ref=land_ref.at[m], send_sem=send.at[m],
                recv_sem=recv.at[m], device_id=(px, py, pc), device_id_type=MESH).start()
        token[...] = jnp.zeros_like(token)

    return pl.pallas_call(
        body, name=name,
        out_shape=(pltpu.SemaphoreType.DMA((7,)), pltpu.SemaphoreType.DMA((7,)), pltpu.HBM(g.shape, g.dtype),
                   pltpu.HBM(land_shape, g.dtype), jax.ShapeDtypeStruct(TOKEN, F32)),
        in_specs=(HBM, HBM), out_specs=(SEM, SEM, HBM, HBM, pl.BlockSpec(memory_space=pltpu.VMEM)),
        input_output_aliases={0: 2, 1: 3}, compiler_params=pltpu.CompilerParams(has_side_effects=EFFECT),
    )(pltpu.with_memory_space_constraint(g, pltpu.HBM),
      pltpu.with_memory_space_constraint(lax.empty(land_shape, g.dtype), pltpu.HBM))


def _scatter_wait(name, started, axis, after):
    send, recv, g_thru, land_thru, _ = started

    def body(g_ref, land_ref, send, recv, after_ref, g_out, land_out):
        x, y, c = _me()
        for m, (px, py, pc) in _peers7(x, y, c):
            cp = pltpu.make_async_remote_copy(
                src_ref=_region(g_ref, axis, 2 * px + py, pc), dst_ref=land_ref.at[m], send_sem=send.at[m],
                recv_sem=recv.at[m], device_id=(px, py, pc), device_id_type=MESH)
            cp.wait_send()
            cp.wait_recv()

    return pl.pallas_call(
        body, name=name,
        out_shape=(pltpu.HBM(g_thru.shape, g_thru.dtype), pltpu.HBM(land_thru.shape, land_thru.dtype)),
        in_specs=(HBM, HBM, SEM, SEM, ANY), out_specs=(HBM, HBM), input_output_aliases={0: 0, 1: 1},
        compiler_params=pltpu.CompilerParams(has_side_effects=EFFECT),
    )(g_thru, land_thru, send, recv, after)


def _own_piece(g, axis):
    x, y, c = _me()
    pr, pc_ = _piece_shape(g.shape, axis)
    if axis == 1:
        return lax.dynamic_slice(g, (c * pr, (2 * x + y) * pc_), (pr, pc_))
    return lax.dynamic_slice(g, ((2 * x + y) * 2 * pr + c * pr, 0), (pr, pc_))


def _sum_pieces(name, own, slots, tr=256):
    S, R, C = slots.shape
    tr = _tile(R, tr, 16)
    nblk = R // tr
    c_arr = lax.axis_index("c").astype(jnp.int32).reshape(1)

    def body(c_ref, own_ref, s_ref, o_ref):
        acc = own_ref[...].astype(F32)
        for s in range(S):
            acc = acc + s_ref[s].astype(F32)
        o_ref[...] = acc

    grid_spec = pltpu.PrefetchScalarGridSpec(
        num_scalar_prefetch=1, grid=(nblk,),
        in_specs=[pl.BlockSpec((tr, C), lambda i, c_ref: (i, 0)), pl.BlockSpec((S, tr, C), lambda i, c_ref: (0, i, 0))],
        out_specs=pl.BlockSpec((tr, C), lambda i, c_ref: (c_ref[0] * nblk + i, 0)))
    return pl.pallas_call(
        body, name=name, grid_spec=grid_spec, out_shape=jax.ShapeDtypeStruct((2 * R, C), F32),
        compiler_params=_params(("parallel",)),
    )(c_arr, own, slots)


def _join_start(name, buf):
    def body(buf_ref, send, recv, buf_thru, token):
        x, y, c = _me()
        hr = buf_ref.shape[0] // 2
        mine = buf_ref.at[pl.ds(c * hr, hr), :]
        pltpu.make_async_remote_copy(src_ref=mine, dst_ref=mine, send_sem=send, recv_sem=recv,
                                     device_id=(x, y, 1 - c), device_id_type=MESH).start()
        token[...] = jnp.zeros_like(token)

    return pl.pallas_call(
        body, name=name,
        out_shape=(pltpu.SemaphoreType.DMA(()), pltpu.SemaphoreType.DMA(()), pltpu.HBM(buf.shape, buf.dtype),
                   jax.ShapeDtypeStruct(TOKEN, F32)),
        in_specs=(HBM,), out_specs=(SEM, SEM, HBM, pl.BlockSpec(memory_space=pltpu.VMEM)),
        input_output_aliases={0: 2}, compiler_params=pltpu.CompilerParams(has_side_effects=EFFECT),
    )(pltpu.with_memory_space_constraint(buf, pltpu.HBM))


def _join_wait(name, started, after):
    send, recv, buf_thru, _ = started

    def body(buf_ref, send, recv, after_ref, buf_out):
        x, y, c = _me()
        hr = buf_ref.shape[0] // 2
        theirs = buf_ref.at[pl.ds((1 - c) * hr, hr), :]
        cp = pltpu.make_async_remote_copy(src_ref=theirs, dst_ref=theirs, send_sem=send, recv_sem=recv,
                                          device_id=(x, y, 1 - c), device_id_type=MESH)
        cp.wait_send()
        cp.wait_recv()

    return pl.pallas_call(
        body, name=name, out_shape=pltpu.HBM(buf_thru.shape, buf_thru.dtype),
        in_specs=(HBM, SEM, SEM, ANY), out_specs=HBM, input_output_aliases={0: 0},
        compiler_params=pltpu.CompilerParams(has_side_effects=EFFECT),
    )(buf_thru, send, recv, after)


def _adamw(name, w, g, m, v, tr=256, deps=()):
    R, C = w.shape
    tr = _tile(R, tr, 8)
    c1 = 1.0 - ADAM_B1 ** ADAM_STEP
    c2 = 1.0 - ADAM_B2 ** ADAM_STEP

    def body(w_ref, g_ref, m_ref, v_ref, *rest):
        d_ref, nm_ref, nv_ref = rest[len(deps):]
        gv = g_ref[...]
        nm = ADAM_B1 * m_ref[...] + (1.0 - ADAM_B1) * gv
        nv = ADAM_B2 * v_ref[...] + (1.0 - ADAM_B2) * jnp.square(gv)
        d_ref[...] = -ADAM_LR * ((nm / c1) / (jnp.sqrt(nv / c2) + ADAM_EPS) + ADAM_WD * w_ref[...])
        nm_ref[...] = nm
        nv_ref[...] = nv

    spec = pl.BlockSpec((tr, C), lambda i: (i, 0))
    return pl.pallas_call(
        body, name=name, grid=(R // tr,), in_specs=[spec] * 4 + [pl.BlockSpec(TOKEN, lambda i: (0, 0))] * len(deps),
        out_specs=[spec] * 3, out_shape=[jax.ShapeDtypeStruct((R, C), F32)] * 3,
        compiler_params=_params(("parallel",)),
    )(w, g, m, v, *deps)


def kernel(x, p, g_mix, w_in, w_dw, b_dw, g_conv_ln, b_conv_ln, w_out, g_ffn, w_gate, w_up, w_down, g_ple, w_pgate, b_pgate, w_ple, g_final, loss_target, m_g_mix, m_w_in, m_w_dw, m_b_dw, m_g_conv_ln, m_b_conv_ln, m_w_out, m_g_ffn, m_w_gate, m_w_up, m_w_down, m_g_ple, m_w_pgate, m_b_pgate, m_w_ple, m_g_final, v_g_mix, v_w_in, v_w_dw, v_b_dw, v_g_conv_ln, v_b_conv_ln, v_w_out, v_g_ffn, v_w_gate, v_w_up, v_w_down, v_g_ple, v_w_pgate, v_b_pgate, v_w_ple, v_g_final):
    T, D = x.shape[1], x.shape[2]
    DC = b_dw.shape[1]
    DA = D - DC
    H = DA // HEAD_DIM
    DP = p.shape[3]
    assert T % SUPER == 0 and DA == DC
    x2 = x.reshape(T, D)
    p2 = p.reshape(T, DP)
    tgt = loss_target.reshape(T, D)
    g_final2 = g_final.reshape(1, D)

    big = dict(w_in=(w_in[0], 1), w_out=(w_out[0], 0), w_gate=(w_gate[0], 1), w_up=(w_up[0], 1),
               w_down=(w_down[0], 0), w_pgate=(w_pgate[0], 0), w_ple=(w_ple[0], 1))
    axis_of = {k: big[k][1] for k in big}
    dw_shard = jnp.pad(w_dw.reshape(CONV_W, -1), ((0, HALO - CONV_W), (0, 0)))
    w_in_full, w_dw_full = _gather_now(big["w_in"][0].astype(BF16), dw_shard)
    later = ["w_out", "w_gate", "w_up", "w_down", "w_pgate", "w_ple"]
    travelling = {k: _gather_start("gather_start_" + k, big[k][0].astype(BF16), axis_of[k]) for k in later}
    issued = [travelling[k][-1] for k in later]

    def weight(k, after):
        return _gather_wait("gather_wait_" + k, travelling[k], axis_of[k], after)

    a = _rms_fwd("rms_mix", x2, g_mix, deps=issued)
    z = _matmul("mm_in", [[(a, w_in_full)]], "nn", [F32], _plain)[0]
    cat, o_attn, lse = _attn_fwd(z, D, H)
    cat, conv_c = _conv_fwd(z, cat, w_dw_full, b_dw, g_conv_ln, b_conv_ln, 3 * DA // DC)
    W = dict(w_in=w_in_full)
    W["w_out"] = weight("w_out", cat)
    h1 = _matmul("mm_out", [[(cat, W["w_out"])]], "nn", [F32], _add_residual, extras=[(x2, "mn")])[0]
    f = _rms_fwd("rms_ffn", h1, g_ffn)
    W["w_gate"] = weight("w_gate", f)
    W["w_up"] = weight("w_up", f)

    def swiglu(accs, extras):
        gt, up = accs
        return gt, up, (gt * jax.nn.sigmoid(gt)) * up

    gate, up, act = _matmul("mm_gate_up", [[(f, W["w_gate"])], [(f, W["w_up"])]], "nn", [BF16, BF16, BF16],
                            swiglu, tm=512, tn=512)
    W["w_down"] = weight("w_down", act)
    h2 = _matmul("mm_down", [[(act, W["w_down"])]], "nn", [F32], _add_residual, extras=[(h1, "mn")], tk=1408)[0]
    n2 = _rms_fwd("rms_ple", h2, g_ple)
    W["w_ple"] = weight("w_ple", n2)
    W["w_pgate"] = weight("w_pgate", n2)
    e = _matmul("mm_ple", [[(p2, W["w_ple"])]], "nn", [F32], _plain)[0]

    def ple_gate(accs, extras):
        bias, hv, ev = extras
        gt = jax.nn.sigmoid(accs[0] + bias)
        return hv + ev * gt, gt, ev

    h3, gte, e16 = _matmul("mm_pgate", [[(n2, W["w_pgate"])]], "nn", [F32, BF16, BF16], ple_gate,
                           extras=[(b_pgate, "n"), (h2, "mn"), (e, "mn")], tm=512)

    sent, joined = {}, {}

    def send_grad(k, g):
        sent[k] = _scatter_start("scatter_start_" + k, g, axis_of[k])
        return sent[k][-1]

    def reduce_grad(k, after):
        g_thru, land = _scatter_wait("scatter_wait_" + k, sent[k], axis_of[k], after)
        half = _sum_pieces("sum_" + k, _own_piece(g_thru, axis_of[k]), land)
        joined[k] = _join_start("join_start_" + k, half)
        return joined[k][-1]

    dh3, de, du4, dg_final, db_pgate, loss_part = _loss_head(h3, tgt, g_final2, gte, e16)
    t_ple = send_grad("w_ple", _matmul("mm_dw_ple", [[(p2, de)]], "tn", [BF16], _plain, tk=1024)[0])
    t_pgate = send_grad("w_pgate", _matmul("mm_dw_pgate", [[(n2, du4)]], "tn", [BF16], _plain, tk=1024)[0])
    dn2 = _matmul("mm_dn2", [[(du4, W["w_pgate"])]], "nt", [F32], _plain, deps=[t_ple, t_pgate])[0]
    dh2, dh2_16, dg_ple = _rms_bwd("rms_ple_bwd", dn2, h2, g_ple, dh3)
    t_down = send_grad("w_down", _matmul("mm_dw_down", [[(act, dh2_16)]], "tn", [BF16], _plain, tm=1408, tk=1024)[0])

    def swiglu_bwd(accs, extras):
        gt, up = extras[0].astype(F32), extras[1].astype(F32)
        sg = jax.nn.sigmoid(gt)
        dact = accs[0]
        return dact * up * (sg * (1.0 + gt * (1.0 - sg))), dact * (gt * sg)

    dgate, dup = _matmul("mm_dact", [[(dh2_16, W["w_down"])]], "nt", [BF16, BF16], swiglu_bwd,
                         extras=[(gate, "mn"), (up, "mn")], tm=512, tn=512, deps=[t_down])
    t_gate = send_grad("w_gate", _matmul("mm_dw_gate", [[(f, dgate)]], "tn", [BF16], _plain, tn=1408, tk=1024)[0])
    t_up = send_grad("w_up", _matmul("mm_dw_up", [[(f, dup)]], "tn", [BF16], _plain, tn=1408, tk=1024,
                                     deps=[t_gate])[0])
    df = _matmul("mm_df", [[(dgate, W["w_gate"]), (dup, W["w_up"])]], "nt", [F32], _plain, tk=1408, deps=[t_up])[0]
    dh1, dh1_16, dg_ffn = _rms_bwd("rms_ffn_bwd", df, h1, g_ffn, dh2)
    t_out = send_grad("w_out", _matmul("mm_dw_out", [[(cat, dh1_16)]], "tn", [BF16], _plain, tk=1024)[0])
    dcat = _matmul("mm_dcat", [[(dh1_16, W["w_out"])]], "nt", [F32], _plain, deps=[t_out])[0]
    j1 = [reduce_grad(k, dcat) for k in ("w_ple", "w_pgate", "w_down")]
    dc, dg_ln, db_ln, db_dw = _conv_ln_bwd(dcat, conv_c, g_conv_ln, b_conv_ln, deps=j1)
    dcv, dcg, dw_dw = _conv_bwd(z, dc, w_dw_full, 3 * DA // DC)
    dq, dk, dv = _attn_bwd(z, dcat, o_attn, lse, H)
    dz = jnp.concatenate([dq, dk, dv, dcv, dcg], axis=1)
    t_in = send_grad("w_in", _matmul("mm_dw_in", [[(a, dz)]], "tn", [BF16], _plain, tk=1024)[0])
    j2 = [reduce_grad(k, dz) for k in ("w_gate", "w_up", "w_out")]
    da = _matmul("mm_da", [[(dz, W["w_in"])]], "nt", [F32], _plain, tk=1024, deps=[t_in] + j2)[0]
    grad_x, _, dg_mix = _rms_bwd("rms_mix_bwd", da, x2, g_mix, dh1)

    wide = [dg_mix, dg_ffn, dg_ple, db_pgate, dg_final,
            jnp.concatenate([db_dw, dg_ln], axis=1), jnp.concatenate([db_ln, jnp.zeros_like(db_ln)], axis=1),
            jnp.pad(loss_part, ((0, 0), (0, D - 128))),
            dw_dw.reshape(HALO * DC // D, D)]
    small = jnp.concatenate(wide, axis=0)
    small = jnp.pad(small, ((0, -small.shape[0] % 8), (0, 0)))
    small_sum = _sum_slots("sum_small", _exchange_small(small))
    j3 = reduce_grad("w_in", small_sum)

    grads, deltas, new_m, new_v = {}, {}, {}, {}
    moments = dict(w_in=(m_w_in, v_w_in), w_out=(m_w_out, v_w_out), w_gate=(m_w_gate, v_w_gate),
                   w_up=(m_w_up, v_w_up), w_down=(m_w_down, v_w_down), w_pgate=(m_w_pgate, v_w_pgate),
                   w_ple=(m_w_ple, v_w_ple))
    last, deps = small_sum, [j3]
    for k in ("w_ple", "w_pgate", "w_down", "w_gate", "w_up", "w_out", "w_in"):
        g_k = _join_wait("join_wait_" + k, joined[k], last)
        d_, m_, v_ = _adamw("adamw_" + k, big[k][0], g_k, moments[k][0][0], moments[k][1][0], deps=deps)
        grads[k], deltas[k], new_m[k], new_v[k] = g_k[None], d_[None], m_[None], v_[None]
        last, deps = d_, ()

    half = lambda r, lo: small_sum[r:r + 1, lo * DC:(lo + 1) * DC]
    vec = dict(g_mix=small_sum[0:1], g_ffn=small_sum[1:2], g_ple=small_sum[2:3], b_pgate=small_sum[3:4],
               g_final=small_sum[4:5], b_dw=half(5, 0), g_conv_ln=half(5, 1), b_conv_ln=half(6, 0))
    loss = small_sum[7, 0]
    dw_dw_sum = small_sum[8:8 + HALO * DC // D].reshape(HALO, DC)
    s_me = 2 * lax.axis_index("x") + lax.axis_index("y")
    cs = w_dw.shape[3]
    vec["w_dw"] = lax.dynamic_slice(dw_dw_sum, (0, s_me * cs), (CONV_W, cs))
    small_w = dict(g_mix=(g_mix, m_g_mix, v_g_mix), g_ffn=(g_ffn, m_g_ffn, v_g_ffn), g_ple=(g_ple, m_g_ple, v_g_ple),
                   b_pgate=(b_pgate, m_b_pgate, v_b_pgate), g_final=(g_final, m_g_final, v_g_final),
                   b_dw=(b_dw, m_b_dw, v_b_dw), g_conv_ln=(g_conv_ln, m_g_conv_ln, v_g_conv_ln),
                   b_conv_ln=(b_conv_ln, m_b_conv_ln, v_b_conv_ln), w_dw=(w_dw, m_w_dw, v_w_dw))
    for k, (w_, m_, v_) in small_w.items():
        shape = w_.shape
        g2 = vec[k]
        to2 = lambda t: t.reshape(g2.shape)
        d_, nm_, nv_ = _adamw("adamw_" + k, to2(w_), g2, to2(m_), to2(v_))
        grads[k], deltas[k], new_m[k], new_v[k] = (t.reshape(shape) for t in (g2, d_, nm_, nv_))

    order = ["g_mix", "w_in", "w_dw", "b_dw", "g_conv_ln", "b_conv_ln", "w_out", "g_ffn", "w_gate", "w_up", "w_down",
             "g_ple", "w_pgate", "b_pgate", "w_ple", "g_final"]
    return (loss, grad_x.reshape(x.shape), *[grads[k] for k in order], *[deltas[k] for k in order],
            *[new_m[k] for k in order], *[new_v[k] for k in order])
```

```python
import functools

import jax
import jax.numpy as jnp
from jax import lax
from jax.experimental import pallas as pl
from jax.experimental.pallas import tpu as pltpu

F32 = jnp.float32
BF16 = jnp.bfloat16

EPS = 1e-6
HEAD_DIM = 128
BLK = 128
DILATIONS = (1, 4, 16)
SUPER = BLK * DILATIONS[-1]
CONV_W = 31
HALO = 32
ADAM_LR, ADAM_B1, ADAM_B2, ADAM_EPS, ADAM_WD, ADAM_STEP = 0.001, 0.9, 0.999, 1e-08, 0.01, 10

V7X_VMEM_BYTES = 64 * 1024 * 1024
VMEM_LIMIT = V7X_VMEM_BYTES * 3 // 4
MESH = pl.DeviceIdType.MESH
ANY = pl.BlockSpec(memory_space=pl.ANY)
HBM = pl.BlockSpec(memory_space=pltpu.HBM)
SEM = pl.BlockSpec(memory_space=pltpu.SEMAPHORE)
EFFECT = pltpu.SideEffectType.DATAFLOW_SIDE_EFFECTING
TOKEN = (8, 128)


def _params(semantics=None, **kw):
    return pltpu.CompilerParams(dimension_semantics=semantics, vmem_limit_bytes=VMEM_LIMIT, **kw)


def _tile(n, want, mult=128):
    if n <= want:
        return n
    for t in range(want - want % mult, 0, -mult):
        if n % t == 0:
            return t
    raise ValueError((n, want, mult))


_DIMS = {"nn": ((1,), (0,)), "nt": ((1,), (1,)), "tn": ((0,), (0,))}


def _matmul(name, groups, mode, out_dtypes, epilogue, extras=(), tm=1024, tn=1024, tk=2048, sub=None, deps=()):
    a0, b0 = groups[0][0]
    if mode == "nn":
        (M, K), N = a0.shape, b0.shape[1]
    elif mode == "nt":
        (M, K), N = a0.shape, b0.shape[0]
    else:
        (K, M), N = a0.shape, b0.shape[1]
    tm, tn, tk = _tile(M, tm), _tile(N, tn), _tile(K, tk)
    nk = K // tk
    if mode == "tn":
        a_spec = pl.BlockSpec((tk, tm), lambda i, j, k: (k, i))
    else:
        a_spec = pl.BlockSpec((tm, tk), lambda i, j, k: (i, k))
    if mode == "nt":
        b_spec = pl.BlockSpec((tn, tk), lambda i, j, k: (j, k))
    else:
        b_spec = pl.BlockSpec((tk, tn), lambda i, j, k: (k, j))
    operands, in_specs = [], []
    for grp in groups:
        for a, b in grp:
            operands += [a, b]
            in_specs += [a_spec, b_spec]
    for arr, kind in extras:
        operands.append(arr)
        if kind == "mn":
            in_specs.append(pl.BlockSpec((tm, tn), lambda i, j, k: (i, j)))
        else:
            in_specs.append(pl.BlockSpec((1, tn), lambda i, j, k: (0, j)))
    for tok in deps:
        operands.append(tok)
        in_specs.append(pl.BlockSpec(TOKEN, lambda i, j, k: (0, 0)))
    n_pairs = [len(g) for g in groups]
    n_ex, n_out, n_grp, n_dep = len(extras), len(out_dtypes), len(groups), len(deps)
    out_shape = [jax.ShapeDtypeStruct((M, N), dt) for dt in out_dtypes]
    out_specs = [pl.BlockSpec((tm, tn), lambda i, j, k: (i, j)) for _ in out_dtypes]

    kinds = [kind for _, kind in extras]
    sub = tm if (sub is None or nk > 1 or mode == "tn") else sub
    assert tm % sub == 0

    def body(*refs):
        ex_refs = refs[2 * sum(n_pairs):2 * sum(n_pairs) + n_ex]
        pos = 2 * sum(n_pairs) + n_ex + n_dep
        out_refs = refs[pos:pos + n_out]
        acc_refs = refs[pos + n_out:]

        def products(rows):
            pos, parts = 0, []
            for g in range(n_grp):
                part = None
                for _ in range(n_pairs[g]):
                    a_ref, b_ref = refs[pos], refs[pos + 1]
                    pos += 2
                    d = lax.dot_general(a_ref[rows].astype(BF16), b_ref[...].astype(BF16),
                                        (_DIMS[mode], ((), ())), preferred_element_type=F32)
                    part = d if part is None else part + d
                parts.append(part)
            return parts

        def finish(accs, rows):
            outs = epilogue(accs, [e[rows] if kind == "mn" else e[...] for e, kind in zip(ex_refs, kinds)])
            for o_ref, o in zip(out_refs, outs):
                o_ref[rows] = o.astype(o_ref.dtype)

        if nk == 1:
            for r in range(tm // sub):
                rows = (pl.ds(r * sub, sub), slice(None)) if sub < tm else (slice(None), slice(None))
                finish(products(rows), rows)
        else:
            k = pl.program_id(2)
            parts = products((slice(None), slice(None)))

            @pl.when(k == 0)
            def _():
                for acc, part in zip(acc_refs, parts):
                    acc[...] = part

            @pl.when(k > 0)
            def _():
                for acc, part in zip(acc_refs, parts):
                    acc[...] += part

            @pl.when(k == nk - 1)
            def _():
                finish([acc[...] for acc in acc_refs], (slice(None), slice(None)))

    scratch = [pltpu.VMEM((tm, tn), F32) for _ in range(n_grp)] if nk > 1 else []
    return pl.pallas_call(
        body, name=name, grid=(M // tm, N // tn, nk),
        in_specs=in_specs, out_specs=out_specs, out_shape=out_shape, scratch_shapes=scratch,
        compiler_params=_params(("parallel", "parallel", "arbitrary")),
    )(*operands)


def _plain(accs, extras):
    return (accs[0],)


def _add_residual(accs, extras):
    return (accs[0] + extras[0],)


def _rowwise(name, fn, rows, consts, row_outs, acc_outs, tb, deps=()):
    T = rows[0][0].shape[0]
    tb = _tile(T, tb, 16)
    operands = [r[0] for r in rows] + list(consts) + list(deps)
    in_specs = [pl.BlockSpec((tb, c), functools.partial(lambda i, cb: (i, cb), cb=cb)) for _, c, cb in rows]
    in_specs += [pl.BlockSpec(c.shape, functools.partial(lambda i, nd: (0,) * nd, nd=c.ndim)) for c in consts]
    in_specs += [pl.BlockSpec(TOKEN, lambda i: (0, 0)) for _ in deps]
    out_shape = [jax.ShapeDtypeStruct(s, dt) for s, dt, _, _ in row_outs]
    out_specs = [pl.BlockSpec((tb, c), functools.partial(lambda i, cb: (i, cb), cb=cb)) for _, _, c, cb in row_outs]
    out_shape += [jax.ShapeDtypeStruct(s, F32) for s in acc_outs]
    out_specs += [pl.BlockSpec(s, functools.partial(lambda i, nd: (0,) * nd, nd=len(s))) for s in acc_outs]
    n_rows, n_consts, n_ro, n_dep = len(rows), len(consts), len(row_outs), len(deps)

    def body(*refs):
        row_refs = refs[:n_rows]
        const_refs = refs[n_rows:n_rows + n_consts]
        out_refs = refs[n_rows + n_consts + n_dep:]
        ro, ao = fn([r[...] for r in row_refs], [c[...] for c in const_refs])
        for o_ref, o in zip(out_refs[:n_ro], ro):
            o_ref[...] = o.astype(o_ref.dtype)
        if acc_outs:
            i = pl.program_id(0)

            @pl.when(i == 0)
            def _():
                for a_ref, a in zip(out_refs[n_ro:], ao):
                    a_ref[...] = a

            @pl.when(i > 0)
            def _():
                for a_ref, a in zip(out_refs[n_ro:], ao):
                    a_ref[...] += a

    return pl.pallas_call(
        body, name=name, grid=(T // tb,), in_specs=in_specs, out_specs=out_specs, out_shape=out_shape,
        compiler_params=_params(("arbitrary",) if acc_outs else ("parallel",)),
    )(*operands)


def _colsum(v):
    return jnp.sum(v, axis=0, keepdims=True)


def _rms_fwd(name, x, g, tb=512, deps=()):
    T, D = x.shape

    def fn(rows, consts):
        xv, gv = rows[0], consts[0]
        r = lax.rsqrt(jnp.mean(xv * xv, axis=-1, keepdims=True) + EPS)
        return ((xv * r) * gv,), ()

    return _rowwise(name, fn, [(x, D, 0)], [g], [((T, D), BF16, D, 0)], [], tb, deps=deps)[0]


def _rms_bwd(name, dy, x, g, resid, tb=256):
    T, D = x.shape

    def fn(rows, consts):
        dyv, xv, rv = rows
        gv = consts[0]
        r = lax.rsqrt(jnp.mean(xv * xv, axis=-1, keepdims=True) + EPS)
        n = xv * r
        dn = dyv * gv
        dx = r * (dn - n * jnp.mean(dn * n, axis=-1, keepdims=True))
        tot = rv + dx
        return (tot, tot), (_colsum(dyv * n),)

    return _rowwise(name, fn, [(dy, D, 0), (x, D, 0), (resid, D, 0)], [g],
                    [((T, D), F32, D, 0), ((T, D), BF16, D, 0)], [(1, D)], tb)


NBLK = SUPER // BLK


def _classes(ref, dil, rows, start=0, dtype=None):
    parts = [ref[pl.ds(start + r, rows, stride=dil), :] if dil > 1 else ref[pl.ds(start, rows), :]
             for r in range(dil)]
    if dtype is not None:
        parts = [p.astype(dtype) for p in parts]
    return parts


def _keys_with_prev(ref, prev_ref, dil):
    L = SUPER // dil
    own = _classes(ref, dil, L, dtype=BF16)
    last = _classes(prev_ref, dil, BLK, start=SUPER - BLK * dil, dtype=BF16)
    blocks = []
    for r in range(dil):
        ext = jnp.concatenate([last[r], own[r]], axis=0)
        blocks += [ext[j * BLK:(j + 2) * BLK] for j in range(L // BLK)]
    return jnp.stack(blocks, axis=0)


def _band_mask(dil, has_prev):
    qi = lax.broadcasted_iota(jnp.int32, (BLK, 2 * BLK), 0)
    kj = lax.broadcasted_iota(jnp.int32, (BLK, 2 * BLK), 1)
    own = jnp.logical_and(kj >= BLK, kj - BLK <= qi)
    prev = jnp.logical_and(kj < BLK, kj >= qi)
    b = lax.broadcasted_iota(jnp.int32, (NBLK, 1, 1), 0)
    first = (b & (SUPER // (BLK * dil) - 1)) == 0
    prev_ok = jnp.logical_or(jnp.logical_not(first), has_prev)
    return jnp.logical_or(own[None], jnp.logical_and(prev[None], prev_ok))


def _bdot(a, b, ca, cb):
    return lax.dot_general(a, b, (((ca,), (cb,)), ((0,), (0,))), preferred_element_type=F32)


def _put_classes(dst, value, dil, rows, start=0, add=False, src_start=0, src_stride=None):
    src_stride = rows if src_stride is None else src_stride
    for r in range(dil):
        idx = (pl.ds(start + r, rows, stride=dil) if dil > 1 else pl.ds(start, rows), slice(None))
        part = value[src_start + r * src_stride:src_start + r * src_stride + rows]
        dst[idx] = dst[idx] + part if add else part


def _attn_fwd(z, cat_width, n_heads):
    T = z.shape[0]
    H = n_heads
    DA = H * HEAD_DIM
    nb = T // SUPER
    scale = HEAD_DIM ** -0.5

    def body(q_ref, k_ref, v_ref, kp_ref, vp_ref, cat_ref, o_ref, lse_ref, ob, lb):
        has_prev = pl.program_id(1) > 0
        for b, dil in enumerate(DILATIONS):
            L = SUPER // dil
            q3 = jnp.concatenate(_classes(q_ref, dil, L, dtype=BF16), axis=0).reshape(NBLK, BLK, HEAD_DIM)
            k3 = _keys_with_prev(k_ref, kp_ref, dil)
            v3 = _keys_with_prev(v_ref, vp_ref, dil)
            s = jnp.where(_band_mask(dil, has_prev), _bdot(q3, k3, 2, 2) * scale, -jnp.inf)
            m = jnp.max(s, axis=-1, keepdims=True)
            e = jnp.exp(s - m)
            den = jnp.sum(e, axis=-1, keepdims=True)
            o3 = _bdot((e / den).astype(BF16), v3, 2, 1)
            lse3 = jnp.broadcast_to(m + jnp.log(den), (NBLK, BLK, HEAD_DIM))
            _put_classes(ob.at[b], o3.reshape(SUPER, HEAD_DIM), dil, L)
            _put_classes(lb.at[b], lse3.reshape(SUPER, HEAD_DIM), dil, L)
        l0, l1, l2 = lb[0], lb[1], lb[2]
        mx = jnp.maximum(jnp.maximum(l0, l1), l2)
        tot = mx + jnp.log(jnp.exp(l0 - mx) + jnp.exp(l1 - mx) + jnp.exp(l2 - mx))
        o = jnp.exp(l0 - tot) * ob[0] + jnp.exp(l1 - tot) * ob[1] + jnp.exp(l2 - tot) * ob[2]
        o_ref[...] = o
        cat_ref[...] = o.astype(BF16)
        lse_ref[...] = tot

    blk = (SUPER, HEAD_DIM)
    in_specs = [
        pl.BlockSpec(blk, lambda h, n: (n, h)),
        pl.BlockSpec(blk, lambda h, n: (n, H + h)),
        pl.BlockSpec(blk, lambda h, n: (n, 2 * H + h)),
        pl.BlockSpec(blk, lambda h, n: (jnp.maximum(n - 1, 0), H + h)),
        pl.BlockSpec(blk, lambda h, n: (jnp.maximum(n - 1, 0), 2 * H + h)),
    ]
    out_spec = pl.BlockSpec(blk, lambda h, n: (n, h))
    return pl.pallas_call(
        body, name="attn_fwd", grid=(H, nb), in_specs=in_specs, out_specs=[out_spec] * 3,
        out_shape=[jax.ShapeDtypeStruct((T, cat_width), BF16), jax.ShapeDtypeStruct((T, DA), F32),
                   jax.ShapeDtypeStruct((T, DA), F32)],
        scratch_shapes=[pltpu.VMEM((3, SUPER, HEAD_DIM), F32), pltpu.VMEM((3, SUPER, HEAD_DIM), F32)],
        compiler_params=_params(("parallel", "parallel")),
    )(z, z, z, z, z)


def _attn_bwd(z, dcat, o, lse, n_heads):
    T = z.shape[0]
    H = n_heads
    nb = T // SUPER
    scale = HEAD_DIM ** -0.5

    def body(q_ref, k_ref, v_ref, kp_ref, vp_ref, do_ref, o_ref, lse_ref, dz_q, dz_k, dz_v,
             dq_acc, dk_acc, dv_acc, dkp_acc, dvp_acc, dsum):
        i = pl.program_id(1)
        has_prev = i < nb - 1

        @pl.when(i == 0)
        def _():
            dk_acc[...] = jnp.zeros_like(dk_acc)
            dv_acc[...] = jnp.zeros_like(dv_acc)

        @pl.when(i > 0)
        def _():
            dk_acc[...] = dkp_acc[...]
            dv_acc[...] = dvp_acc[...]

        dq_acc[...] = jnp.zeros_like(dq_acc)
        dkp_acc[...] = jnp.zeros_like(dkp_acc)
        dvp_acc[...] = jnp.zeros_like(dvp_acc)
        dsum[...] = jnp.broadcast_to(jnp.sum(do_ref[...] * o_ref[...], axis=-1, keepdims=True), (SUPER, HEAD_DIM))
        for dil in DILATIONS:
            L = SUPER // dil
            wide = lambda ref: jnp.tile(jnp.concatenate(_classes(ref, dil, L), axis=0).reshape(NBLK, BLK, HEAD_DIM),
                                        (1, 1, 2))
            q3 = jnp.concatenate(_classes(q_ref, dil, L, dtype=BF16), axis=0).reshape(NBLK, BLK, HEAD_DIM)
            do3 = jnp.concatenate(_classes(do_ref, dil, L, dtype=BF16), axis=0).reshape(NBLK, BLK, HEAD_DIM)
            k3 = _keys_with_prev(k_ref, kp_ref, dil)
            v3 = _keys_with_prev(v_ref, vp_ref, dil)
            p = jnp.where(_band_mask(dil, has_prev), jnp.exp(_bdot(q3, k3, 2, 2) * scale - wide(lse_ref)), 0.0)
            ds = (p * (_bdot(do3, v3, 2, 2) - wide(dsum)) * scale).astype(BF16)
            dq = _bdot(ds, k3, 2, 1).reshape(SUPER, HEAD_DIM)
            dk = _bdot(ds, q3, 1, 1)
            dv = _bdot(p.astype(BF16), do3, 1, 1)
            _put_classes(dq_acc, dq, dil, L, add=True)
            for acc, prev_acc, g in ((dk_acc, dkp_acc, dk), (dv_acc, dvp_acc, dv)):
                _put_classes(acc, g[:, BLK:, :].reshape(SUPER, HEAD_DIM), dil, L, add=True)
                to_prev = g[:, :BLK, :].reshape(SUPER, HEAD_DIM)
                if L > BLK:
                    _put_classes(acc, to_prev, dil, L - BLK, add=True, src_start=BLK, src_stride=L)
                _put_classes(prev_acc, to_prev, dil, BLK, start=SUPER - BLK * dil, add=True, src_stride=L)
        dz_q[...] = dq_acc[...].astype(BF16)
        dz_k[...] = dk_acc[...].astype(BF16)
        dz_v[...] = dv_acc[...].astype(BF16)

    blk = (SUPER, HEAD_DIM)
    row = lambda i: nb - 1 - i
    in_specs = [
        pl.BlockSpec(blk, lambda h, i: (row(i), h)),
        pl.BlockSpec(blk, lambda h, i: (row(i), H + h)),
        pl.BlockSpec(blk, lambda h, i: (row(i), 2 * H + h)),
        pl.BlockSpec(blk, lambda h, i: (jnp.maximum(row(i) - 1, 0), H + h)),
        pl.BlockSpec(blk, lambda h, i: (jnp.maximum(row(i) - 1, 0), 2 * H + h)),
        pl.BlockSpec(blk, lambda h, i: (row(i), h)),
        pl.BlockSpec(blk, lambda h, i: (row(i), h)),
        pl.BlockSpec(blk, lambda h, i: (row(i), h)),
    ]
    out_spec = pl.BlockSpec(blk, lambda h, i: (row(i), h))
    return pl.pallas_call(
        body, name="attn_bwd", grid=(H, nb), in_specs=in_specs, out_specs=[out_spec] * 3,
        out_shape=[jax.ShapeDtypeStruct((T, H * HEAD_DIM), BF16)] * 3,
        scratch_shapes=[pltpu.VMEM(blk, F32) for _ in range(6)],
        compiler_params=_params(("parallel", "arbitrary")),
    )(z, z, z, z, z, dcat, o, lse)


def _glu(cv, cg):
    return cv * jax.nn.sigmoid(cg)


SUBLANES = 8


def _fill_shifted(sh, ext, rows):
    for b in range(1, SUBLANES):
        sh[b - 1, pl.ds(0, rows - SUBLANES), :] = ext[pl.ds(b, rows - SUBLANES), :]


def _window(sh, ext, offset, rows):
    b = offset % SUBLANES
    if b == 0:
        return ext[pl.ds(offset, rows), :]
    return sh[b - 1, pl.ds(offset - b, rows), :]


def _conv_fwd(z, cat, w_dw, b_dw, g_ln, b_ln, col0, tb=512):
    T = z.shape[0]
    DC = w_dw.shape[1]
    tb = _tile(T, tb, HALO)
    hb = tb // HALO
    cat_cb = cat.shape[1] // DC - 1

    def body(cv_ref, cg_ref, cvh_ref, cgh_ref, w_ref, bdw_ref, g_ref, b_ref, cat_in, cat_ref, c_ref, u_ext, sh):
        i = pl.program_id(0)
        halo = _glu(cvh_ref[...], cgh_ref[...])
        u_ext[pl.ds(0, HALO), :] = jnp.where(i > 0, halo, 0.0)
        u_ext[pl.ds(HALO, tb), :] = _glu(cv_ref[...], cg_ref[...])
        _fill_shifted(sh, u_ext, tb + HALO)
        acc = jnp.broadcast_to(bdw_ref[...], (tb, DC))
        for j in range(CONV_W):
            acc = acc + w_ref[pl.ds(j, 1), :] * _window(sh, u_ext, HALO - (CONV_W - 1) + j, tb)
        c_ref[...] = acc
        mu = jnp.mean(acc, axis=-1, keepdims=True)
        var = jnp.mean(jnp.square(acc - mu), axis=-1, keepdims=True)
        y = (acc - mu) * lax.rsqrt(var + EPS) * g_ref[...] + b_ref[...]
        cat_ref[...] = (y * jax.nn.sigmoid(y)).astype(BF16)

    cur = lambda cb: pl.BlockSpec((tb, DC), lambda i: (i, cb))
    halo = lambda cb: pl.BlockSpec((HALO, DC), lambda i: (jnp.maximum(i * hb - 1, 0), cb))
    whole = lambda a: pl.BlockSpec(a.shape, lambda i: (0, 0))
    return pl.pallas_call(
        body, name="conv_fwd", grid=(T // tb,),
        in_specs=[cur(col0), cur(col0 + 1), halo(col0), halo(col0 + 1), whole(w_dw), whole(b_dw), whole(g_ln),
                  whole(b_ln), ANY],
        out_specs=[pl.BlockSpec((tb, DC), lambda i: (i, cat_cb)), pl.BlockSpec((tb, DC), lambda i: (i, 0))],
        out_shape=[jax.ShapeDtypeStruct(cat.shape, cat.dtype), jax.ShapeDtypeStruct((T, DC), F32)],
        scratch_shapes=[pltpu.VMEM((tb + HALO, DC), F32), pltpu.VMEM((SUBLANES - 1, tb + HALO, DC), F32)],
        input_output_aliases={8: 0},
        compiler_params=_params(("parallel",)),
    )(z, z, z, z, w_dw, b_dw, g_ln, b_ln, cat)


def _conv_ln_bwd(dcat, c, g_ln, b_ln, tb=256, deps=()):
    T, DC = c.shape
    d_cb = dcat.shape[1] // DC - 1

    def fn(rows, consts):
        dov, cv_ = rows
        gv, bv = consts
        mu = jnp.mean(cv_, axis=-1, keepdims=True)
        xc = cv_ - mu
        rstd = lax.rsqrt(jnp.mean(jnp.square(xc), axis=-1, keepdims=True) + EPS)
        ln = xc * rstd
        y = ln * gv + bv
        sg = jax.nn.sigmoid(y)
        dy = dov * (sg * (1.0 + y * (1.0 - sg)))
        dln = dy * gv
        dc = rstd * (dln - jnp.mean(dln, axis=-1, keepdims=True) - ln * jnp.mean(dln * ln, axis=-1, keepdims=True))
        return (dc,), (_colsum(dy * ln), _colsum(dy), _colsum(dc))

    return _rowwise("conv_ln_bwd", fn, [(dcat, DC, d_cb), (c, DC, 0)], [g_ln, b_ln],
                    [((T, DC), F32, DC, 0)], [(1, DC)] * 3, tb, deps=deps)


def _conv_bwd(z, dc, w_dw, col0, tb=512):
    T, DC = dc.shape
    tb = _tile(T, tb, HALO)
    hb = tb // HALO
    nblk = T // tb

    def body(cv_ref, cg_ref, dc_ref, dcn_ref, w_ref, dcv_ref, dcg_ref, dw_ref, dc_ext, sh):
        i = pl.program_id(0)
        cv, cg = cv_ref[...], cg_ref[...]
        sg = jax.nn.sigmoid(cg)
        u = cv * sg
        dc_ext[pl.ds(0, tb), :] = dc_ref[...]
        dc_ext[pl.ds(tb, HALO), :] = jnp.where(i < nblk - 1, dcn_ref[...], 0.0)
        _fill_shifted(sh, dc_ext, tb + HALO)

        @pl.when(i == 0)
        def _():
            dw_ref[...] = jnp.zeros_like(dw_ref)

        du = jnp.zeros((tb, DC), F32)
        for j in range(CONV_W):
            d_j = _window(sh, dc_ext, CONV_W - 1 - j, tb)
            du = du + w_ref[pl.ds(j, 1), :] * d_j
            dw_ref[pl.ds(j, 1), :] += _colsum(u * d_j)
        dcv_ref[...] = (du * sg).astype(BF16)
        dcg_ref[...] = (du * cv * sg * (1.0 - sg)).astype(BF16)

    cur = lambda cb: pl.BlockSpec((tb, DC), lambda i: (i, cb))
    nxt = pl.BlockSpec((HALO, DC), lambda i: (jnp.minimum((i + 1) * hb, T // HALO - 1), 0))
    return pl.pallas_call(
        body, name="conv_bwd", grid=(nblk,),
        in_specs=[cur(col0), cur(col0 + 1), cur(0), nxt, pl.BlockSpec(w_dw.shape, lambda i: (0, 0))],
        out_specs=[cur(0), cur(0), pl.BlockSpec((HALO, DC), lambda i: (0, 0))],
        out_shape=[jax.ShapeDtypeStruct((T, DC), BF16), jax.ShapeDtypeStruct((T, DC), BF16),
                   jax.ShapeDtypeStruct((HALO, DC), F32)],
        scratch_shapes=[pltpu.VMEM((tb + HALO, DC), F32), pltpu.VMEM((SUBLANES - 1, tb + HALO, DC), F32)],
        compiler_params=_params(("arbitrary",)),
    )(z, z, dc, dc, w_dw)


def _loss_head(h3, target, g_final, gte, e, tb=256):
    T, D = h3.shape

    def fn(rows, consts):
        hv, tv, gt, ev = rows
        gv = consts[0]
        gt = gt.astype(F32)
        ev = ev.astype(F32)
        r = lax.rsqrt(jnp.mean(hv * hv, axis=-1, keepdims=True) + EPS)
        n = hv * r
        diff = n * gv - tv
        loss = 0.5 * jnp.sum(jnp.mean(jnp.square(diff), axis=-1, keepdims=True), axis=0, keepdims=True)
        dy = diff * (1.0 / D)
        dn = dy * gv
        dh = r * (dn - n * jnp.mean(dn * n, axis=-1, keepdims=True))
        du4 = dh * ev * gt * (1.0 - gt)
        return (dh, dh * gt, du4), (_colsum(dy * n), _colsum(du4), jnp.broadcast_to(loss, (1, 128)))

    return _rowwise("loss_head", fn, [(h3, D, 0), (target, D, 0), (gte, D, 0), (e, D, 0)], [g_final],
                    [((T, D), F32, D, 0), ((T, D), BF16, D, 0), ((T, D), BF16, D, 0)],
                    [(1, D), (1, D), (1, 128)], tb)


def _me():
    return lax.axis_index("x"), lax.axis_index("y"), lax.axis_index("c")


def _chips3(x, y):
    return [(1 - x, y), (x, 1 - y), (1 - x, 1 - y)]


def _peers7(x, y, c):
    for m in range(1, 8):
        yield m - 1, (x ^ (m >> 2), y ^ ((m >> 1) & 1), c ^ (m & 1))


def _shard_of(ref, axis, s):
    R, C = ref.shape
    if axis == 1:
        return ref.at[:, pl.ds(s * (C // 4), C // 4)]
    return ref.at[pl.ds(s * (R // 4), R // 4), :]


def _region(ref, axis, shard, half):
    R, C = ref.shape
    if axis == 1:
        cs, hr = C // 4, R // 2
        return ref.at[pl.ds(half * hr, hr), pl.ds(shard * cs, cs)]
    hr = R // 8
    return ref.at[pl.ds(shard * 2 * hr + half * hr, hr), :]


def _gather_now(shard, small):
    R, C = shard.shape
    cs = small.shape[1]

    def body(shard_ref, small_ref, out_ref, small_out, send, recv, fsend, frecv, lsem):
        x, y, c = _me()
        me_s = 2 * x + y
        sibling = (x, y, 1 - c)
        chips = _chips3(x, y)
        half = shard_ref.at[pl.ds(c * (R // 2), R // 2), :]
        locals_ = [pltpu.make_async_copy(shard_ref, _shard_of(out_ref, 1, me_s), lsem.at[0]),
                   pltpu.make_async_copy(small_ref, _shard_of(small_out, 1, me_s), lsem.at[1])]
        for cp in locals_:
            cp.start()
        firsts = []
        for j, (px, py) in enumerate(chips):
            firsts.append(pltpu.make_async_remote_copy(
                src_ref=half, dst_ref=_region(out_ref, 1, me_s, c), send_sem=send.at[0, j], recv_sem=recv.at[0, j],
                device_id=(px, py, c), device_id_type=MESH))
            firsts.append(pltpu.make_async_remote_copy(
                src_ref=small_ref, dst_ref=_shard_of(small_out, 1, me_s), send_sem=send.at[1, j],
                recv_sem=recv.at[1, j], device_id=(px, py, c), device_id_type=MESH))
        for cp in firsts:
            cp.start()
        relays = []
        for j, (px, py) in enumerate(chips):
            landed = _region(out_ref, 1, 2 * px + py, c)
            pltpu.make_async_remote_copy(
                src_ref=half, dst_ref=landed, send_sem=send.at[0, j], recv_sem=recv.at[0, j],
                device_id=(px, py, c), device_id_type=MESH).wait_recv()
            relay = pltpu.make_async_remote_copy(
                src_ref=landed, dst_ref=landed, send_sem=fsend.at[j], recv_sem=frecv.at[j],
                device_id=sibling, device_id_type=MESH)
            relay.start()
            relays.append(relay)
        for j, (px, py) in enumerate(chips):
            pltpu.make_async_remote_copy(
                src_ref=small_ref, dst_ref=_shard_of(small_out, 1, 2 * px + py), send_sem=send.at[1, j],
                recv_sem=recv.at[1, j], device_id=(px, py, c), device_id_type=MESH).wait_recv()
            theirs = _region(out_ref, 1, 2 * px + py, 1 - c)
            pltpu.make_async_remote_copy(
                src_ref=theirs, dst_ref=theirs, send_sem=fsend.at[j], recv_sem=frecv.at[j],
                device_id=sibling, device_id_type=MESH).wait_recv()
        for cp in firsts + relays:
            cp.wait_send()
        for cp in locals_:
            cp.wait()

    return pl.pallas_call(
        body, name="gather_now", in_specs=[ANY, ANY], out_specs=[ANY, ANY],
        out_shape=[jax.ShapeDtypeStruct((R, 4 * C), shard.dtype),
                   jax.ShapeDtypeStruct((small.shape[0], 4 * cs), small.dtype)],
        scratch_shapes=[pltpu.SemaphoreType.DMA((2, 3)), pltpu.SemaphoreType.DMA((2, 3)),
                        pltpu.SemaphoreType.DMA((3,)), pltpu.SemaphoreType.DMA((3,)), pltpu.SemaphoreType.DMA((2,))],
        compiler_params=_params(),
    )(shard, small)


def _exchange_small(small):
    def body(small_ref, out_ref, send, recv, lsem):
        x, y, c = _me()
        me = 4 * x + 2 * y + c
        own = pltpu.make_async_copy(small_ref, out_ref.at[me], lsem)
        own.start()
        sends = [pltpu.make_async_remote_copy(
            src_ref=small_ref, dst_ref=out_ref.at[me], send_sem=send.at[m], recv_sem=recv.at[m],
            device_id=peer, device_id_type=MESH) for m, peer in _peers7(x, y, c)]
        for cp in sends:
            cp.start()
        for m, (px, py, pc) in _peers7(x, y, c):
            pltpu.make_async_remote_copy(
                src_ref=small_ref, dst_ref=out_ref.at[4 * px + 2 * py + pc], send_sem=send.at[m], recv_sem=recv.at[m],
                device_id=(px, py, pc), device_id_type=MESH).wait_recv()
        for cp in sends:
            cp.wait_send()
        own.wait()

    return pl.pallas_call(
        body, name="exchange_small", in_specs=[ANY], out_specs=ANY,
        out_shape=jax.ShapeDtypeStruct((8,) + small.shape, small.dtype),
        scratch_shapes=[pltpu.SemaphoreType.DMA((7,)), pltpu.SemaphoreType.DMA((7,)), pltpu.SemaphoreType.DMA(())],
        compiler_params=_params(),
    )(small)


def _sum_slots(name, slots, tr=256):
    S, R, C = slots.shape
    tr = _tile(R, tr, 16)

    def body(s_ref, o_ref):
        acc = s_ref[0].astype(F32)
        for s in range(1, S):
            acc = acc + s_ref[s].astype(F32)
        o_ref[...] = acc

    return pl.pallas_call(
        body, name=name, grid=(R // tr,), in_specs=[pl.BlockSpec((S, tr, C), lambda i: (0, i, 0))],
        out_specs=pl.BlockSpec((tr, C), lambda i: (i, 0)), out_shape=jax.ShapeDtypeStruct((R, C), F32),
        compiler_params=_params(("parallel",)),
    )(slots)


def _place_shard(name, shard, axis):
    R, C = shard.shape
    full_shape = (R, 4 * C) if axis == 1 else (4 * R, C)

    def body(shard_ref, full_ref, sem):
        x, y, c = _me()
        cp = pltpu.make_async_copy(shard_ref, _shard_of(full_ref, axis, 2 * x + y), sem)
        cp.start()
        cp.wait()

    return pl.pallas_call(
        body, name=name, in_specs=[ANY], out_specs=ANY, out_shape=jax.ShapeDtypeStruct(full_shape, shard.dtype),
        scratch_shapes=[pltpu.SemaphoreType.DMA(())], compiler_params=_params(),
    )(shard)


def _gather_start(name, shard, axis, after):
    full = _place_shard(name + "_place", shard, axis)
    full_shape = full.shape

    def body(shard_ref, full_ref, after_ref, send, recv, shard_thru, full_thru, token):
        x, y, c = _me()
        for j, (px, py) in enumerate(_chips3(x, y)):
            pltpu.make_async_remote_copy(
                src_ref=shard_ref, dst_ref=_shard_of(full_ref, axis, 2 * x + y), send_sem=send.at[j],
                recv_sem=recv.at[j], device_id=(px, py, c), device_id_type=MESH).start()
        token[...] = jnp.zeros_like(token)

    return pl.pallas_call(
        body, name=name,
        out_shape=(pltpu.SemaphoreType.DMA((3,)), pltpu.SemaphoreType.DMA((3,)), pltpu.HBM(shard.shape, shard.dtype),
                   pltpu.HBM(full_shape, shard.dtype), jax.ShapeDtypeStruct(TOKEN, F32)),
        in_specs=(HBM, HBM, ANY), out_specs=(SEM, SEM, HBM, HBM, pl.BlockSpec(memory_space=pltpu.VMEM)),
        input_output_aliases={0: 2, 1: 3}, compiler_params=pltpu.CompilerParams(has_side_effects=EFFECT),
    )(pltpu.with_memory_space_constraint(shard, pltpu.HBM), pltpu.with_memory_space_constraint(full, pltpu.HBM), after)


def _gather_wait(name, started, axis, after):
    send, recv, shard_thru, full_thru, _ = started

    def body(shard_ref, full_ref, send, recv, after_ref, shard_dead, full_out):
        x, y, c = _me()
        for j, (px, py) in enumerate(_chips3(x, y)):
            cp = pltpu.make_async_remote_copy(
                src_ref=shard_ref, dst_ref=_shard_of(full_ref, axis, 2 * px + py), send_sem=send.at[j],
                recv_sem=recv.at[j], device_id=(px, py, c), device_id_type=MESH)
            cp.wait_send()
            cp.wait_recv()

    return pl.pallas_call(
        body, name=name,
        out_shape=(pltpu.HBM(shard_thru.shape, shard_thru.dtype), pltpu.HBM(full_thru.shape, full_thru.dtype)),
        in_specs=(HBM, HBM, SEM, SEM, ANY), out_specs=(HBM, HBM), input_output_aliases={0: 0, 1: 1},
        compiler_params=pltpu.CompilerParams(has_side_effects=EFFECT),
    )(shard_thru, full_thru, send, recv, after)[1]


def _piece_shape(shape, axis):
    R, C = shape
    return (R // 2, C // 4) if axis == 1 else (R // 8, C)


def _scatter_start(name, g, axis):
    land_shape = (7,) + _piece_shape(g.shape, axis)

    def body(g_ref, land_ref, send, recv, g_thru, land_thru, token):
        x, y, c = _me()
        for m, (px, py, pc) in _peers7(x, y, c):
            pltpu.make_async_remote_copy(
                src_ref=_region(g_ref, axis, 2 * px + py, pc), dst_ref=land_ref.at[m], send_sem=send.at[m],
                recv_sem=recv.at[m], device_id=(px, py, pc), device_id_type=MESH).start()
        token[...] = jnp.zeros_like(token)

    return pl.pallas_call(
        body, name=name,
        out_shape=(pltpu.SemaphoreType.DMA((7,)), pltpu.SemaphoreType.DMA((7,)), pltpu.HBM(g.shape, g.dtype),
                   pltpu.HBM(land_shape, g.dtype), jax.ShapeDtypeStruct(TOKEN, F32)),
        in_specs=(HBM, HBM), out_specs=(SEM, SEM, HBM, HBM, pl.BlockSpec(memory_space=pltpu.VMEM)),
        input_output_aliases={0: 2, 1: 3}, compiler_params=pltpu.CompilerParams(has_side_effects=EFFECT),
    )(pltpu.with_memory_space_constraint(g, pltpu.HBM),
      pltpu.with_memory_space_constraint(lax.empty(land_shape, g.dtype), pltpu.HBM))


def _scatter_wait(name, started, axis, after):
    send, recv, g_thru, land_thru, _ = started

    def body(g_ref, land_ref, send, recv, after_ref, g_out, land_out):
        x, y, c = _me()
        for m, (px, py, pc) in _peers7(x, y, c):
            cp = pltpu.make_async_remote_copy(
                src_ref=_region(g_ref, axis, 2 * px + py, pc), dst_ref=land_ref.at[m], send_sem=send.at[m],
                recv_sem=recv.at[m], device_id=(px, py, pc), device_id_type=MESH)
            cp.wait_send()
            cp.wait_recv()

    return pl.pallas_call(
        body, name=name,
        out_shape=(pltpu.HBM(g_thru.shape, g_thru.dtype), pltpu.HBM(land_thru.shape, land_thru.dtype)),
        in_specs=(HBM, HBM, SEM, SEM, ANY), out_specs=(HBM, HBM), input_output_aliases={0: 0, 1: 1},
        compiler_params=pltpu.CompilerParams(has_side_effects=EFFECT),
    )(g_thru, land_thru, send, recv, after)


def _own_piece(g, axis):
    x, y, c = _me()
    pr, pc_ = _piece_shape(g.shape, axis)
    if axis == 1:
        return lax.dynamic_slice(g, (c * pr, (2 * x + y) * pc_), (pr, pc_))
    return lax.dynamic_slice(g, ((2 * x + y) * 2 * pr + c * pr, 0), (pr, pc_))


def _sum_pieces(name, own, slots, tr=256):
    S, R, C = slots.shape
    tr = _tile(R, tr, 16)
    nblk = R // tr
    c_arr = lax.axis_index("c").astype(jnp.int32).reshape(1)

    def body(c_ref, own_ref, s_ref, o_ref):
        acc = own_ref[...].astype(F32)
        for s in range(S):
            acc = acc + s_ref[s].astype(F32)
        o_ref[...] = acc

    grid_spec = pltpu.PrefetchScalarGridSpec(
        num_scalar_prefetch=1, grid=(nblk,),
        in_specs=[pl.BlockSpec((tr, C), lambda i, c_ref: (i, 0)), pl.BlockSpec((S, tr, C), lambda i, c_ref: (0, i, 0))],
        out_specs=pl.BlockSpec((tr, C), lambda i, c_ref: (c_ref[0] * nblk + i, 0)))
    return pl.pallas_call(
        body, name=name, grid_spec=grid_spec, out_shape=jax.ShapeDtypeStruct((2 * R, C), F32),
        compiler_params=_params(("parallel",)),
    )(c_arr, own, slots)


def _join_start(name, buf):
    def body(buf_ref, send, recv, buf_thru, token):
        x, y, c = _me()
        hr = buf_ref.shape[0] // 2
        mine = buf_ref.at[pl.ds(c * hr, hr), :]
        pltpu.make_async_remote_copy(src_ref=mine, dst_ref=mine, send_sem=send, recv_sem=recv,
                                     device_id=(x, y, 1 - c), device_id_type=MESH).start()
        token[...] = jnp.zeros_like(token)

    return pl.pallas_call(
        body, name=name,
        out_shape=(pltpu.SemaphoreType.DMA(()), pltpu.SemaphoreType.DMA(()), pltpu.HBM(buf.shape, buf.dtype),
                   jax.ShapeDtypeStruct(TOKEN, F32)),
        in_specs=(HBM,), out_specs=(SEM, SEM, HBM, pl.BlockSpec(memory_space=pltpu.VMEM)),
        input_output_aliases={0: 2}, compiler_params=pltpu.CompilerParams(has_side_effects=EFFECT),
    )(pltpu.with_memory_space_constraint(buf, pltpu.HBM))


def _join_wait(name, started, after):
    send, recv, buf_thru, _ = started

    def body(buf_ref, send, recv, after_ref, buf_out):
        x, y, c = _me()
        hr = buf_ref.shape[0] // 2
        theirs = buf_ref.at[pl.ds((1 - c) * hr, hr), :]
        cp = pltpu.make_async_remote_copy(src_ref=theirs, dst_ref=theirs, send_sem=send, recv_sem=recv,
                                          device_id=(x, y, 1 - c), device_id_type=MESH)
        cp.wait_send()
        cp.wait_recv()

    return pl.pallas_call(
        body, name=name, out_shape=pltpu.HBM(buf_thru.shape, buf_thru.dtype),
        in_specs=(HBM, SEM, SEM, ANY), out_specs=HBM, input_output_aliases={0: 0},
        compiler_params=pltpu.CompilerParams(has_side_effects=EFFECT),
    )(buf_thru, send, recv, after)


def _adamw(name, w, g, m, v, tr=256, deps=()):
    R, C = w.shape
    tr = _tile(R, tr, 8)
    c1 = 1.0 - ADAM_B1 ** ADAM_STEP
    c2 = 1.0 - ADAM_B2 ** ADAM_STEP

    def body(w_ref, g_ref, m_ref, v_ref, *rest):
        d_ref, nm_ref, nv_ref = rest[len(deps):]
        gv = g_ref[...]
        nm = ADAM_B1 * m_ref[...] + (1.0 - ADAM_B1) * gv
        nv = ADAM_B2 * v_ref[...] + (1.0 - ADAM_B2) * jnp.square(gv)
        d_ref[...] = -ADAM_LR * ((nm / c1) / (jnp.sqrt(nv / c2) + ADAM_EPS) + ADAM_WD * w_ref[...])
        nm_ref[...] = nm
        nv_ref[...] = nv

    spec = pl.BlockSpec((tr, C), lambda i: (i, 0))
    return pl.pallas_call(
        body, name=name, grid=(R // tr,), in_specs=[spec] * 4 + [pl.BlockSpec(TOKEN, lambda i: (0, 0))] * len(deps),
        out_specs=[spec] * 3, out_shape=[jax.ShapeDtypeStruct((R, C), F32)] * 3,
        compiler_params=_params(("parallel",)),
    )(w, g, m, v, *deps)


def kernel(x, p, g_mix, w_in, w_dw, b_dw, g_conv_ln, b_conv_ln, w_out, g_ffn, w_gate, w_up, w_down, g_ple, w_pgate, b_pgate, w_ple, g_final, loss_target, m_g_mix, m_w_in, m_w_dw, m_b_dw, m_g_conv_ln, m_b_conv_ln, m_w_out, m_g_ffn, m_w_gate, m_w_up, m_w_down, m_g_ple, m_w_pgate, m_b_pgate, m_w_ple, m_g_final, v_g_mix, v_w_in, v_w_dw, v_b_dw, v_g_conv_ln, v_b_conv_ln, v_w_out, v_g_ffn, v_w_gate, v_w_up, v_w_down, v_g_ple, v_w_pgate, v_b_pgate, v_w_ple, v_g_final):
    T, D = x.shape[1], x.shape[2]
    DC = b_dw.shape[1]
    DA = D - DC
    H = DA // HEAD_DIM
    DP = p.shape[3]
    assert T % SUPER == 0 and DA == DC
    x2 = x.reshape(T, D)
    p2 = p.reshape(T, DP)
    tgt = loss_target.reshape(T, D)
    g_final2 = g_final.reshape(1, D)

    big = dict(w_in=(w_in[0], 1), w_out=(w_out[0], 0), w_gate=(w_gate[0], 1), w_up=(w_up[0], 1),
               w_down=(w_down[0], 0), w_pgate=(w_pgate[0], 0), w_ple=(w_ple[0], 1))
    axis_of = {k: big[k][1] for k in big}
    dw_shard = jnp.pad(w_dw.reshape(CONV_W, -1), ((0, HALO - CONV_W), (0, 0)))
    w_in_full, w_dw_full = _gather_now(big["w_in"][0].astype(BF16), dw_shard)
    later = ["w_out", "w_gate", "w_up", "w_down", "w_pgate", "w_ple"]
    travelling = {k: _gather_start("gather_start_" + k, big[k][0].astype(BF16), axis_of[k], w_in_full) for k in later}
    issued = [travelling[k][-1] for k in later]

    def weight(k, after):
        return _gather_wait("gather_wait_" + k, travelling[k], axis_of[k], after)

    a = _rms_fwd("rms_mix", x2, g_mix, deps=issued)
    z = _matmul("mm_in", [[(a, w_in_full)]], "nn", [F32], _plain)[0]
    cat, o_attn, lse = _attn_fwd(z, D, H)
    cat, conv_c = _conv_fwd(z, cat, w_dw_full, b_dw, g_conv_ln, b_conv_ln, 3 * DA // DC)
    W = dict(w_in=w_in_full)
    W["w_out"] = weight("w_out", cat)
    h1 = _matmul("mm_out", [[(cat, W["w_out"])]], "nn", [F32], _add_residual, extras=[(x2, "mn")])[0]
    f = _rms_fwd("rms_ffn", h1, g_ffn)
    W["w_gate"] = weight("w_gate", f)
    W["w_up"] = weight("w_up", f)

    def swiglu(accs, extras):
        gt, up = accs
        return gt, up, (gt * jax.nn.sigmoid(gt)) * up

    gate, up, act = _matmul("mm_gate_up", [[(f, W["w_gate"])], [(f, W["w_up"])]], "nn", [BF16, BF16, BF16],
                            swiglu, tm=1024, tn=512, sub=256)
    W["w_down"] = weight("w_down", act)
    h2 = _matmul("mm_down", [[(act, W["w_down"])]], "nn", [F32], _add_residual, extras=[(h1, "mn")], tk=1408)[0]
    n2 = _rms_fwd("rms_ple", h2, g_ple)
    W["w_ple"] = weight("w_ple", n2)
    W["w_pgate"] = weight("w_pgate", n2)
    e = _matmul("mm_ple", [[(p2, W["w_ple"])]], "nn", [F32], _plain)[0]

    def ple_gate(accs, extras):
        bias, hv, ev = extras
        gt = jax.nn.sigmoid(accs[0] + bias)
        return hv + ev * gt, gt, ev

    h3, gte, e16 = _matmul("mm_pgate", [[(n2, W["w_pgate"])]], "nn", [F32, BF16, BF16], ple_gate,
                           extras=[(b_pgate, "n"), (h2, "mn"), (e, "mn")], tm=512, sub=256)

    sent, joined = {}, {}

    def send_grad(k, g):
        sent[k] = _scatter_start("scatter_start_" + k, g, axis_of[k])
        return sent[k][-1]

    def reduce_grad(k, after):
        g_thru, land = _scatter_wait("scatter_wait_" + k, sent[k], axis_of[k], after)
        half = _sum_pieces("sum_" + k, _own_piece(g_thru, axis_of[k]), land)
        joined[k] = _join_start("join_start_" + k, half)
        return joined[k][-1]

    dh3, de, du4, dg_final, db_pgate, loss_part = _loss_head(h3, tgt, g_final2, gte, e16)
    t_ple = send_grad("w_ple", _matmul("mm_dw_ple", [[(p2, de)]], "tn", [BF16], _plain, tk=1024)[0])
    t_pgate = send_grad("w_pgate", _matmul("mm_dw_pgate", [[(n2, du4)]], "tn", [BF16], _plain, tk=2048)[0])
    dn2 = _matmul("mm_dn2", [[(du4, W["w_pgate"])]], "nt", [F32], _plain, deps=[t_ple, t_pgate])[0]
    dh2, dh2_16, dg_ple = _rms_bwd("rms_ple_bwd", dn2, h2, g_ple, dh3)
    t_down = send_grad("w_down", _matmul("mm_dw_down", [[(act, dh2_16)]], "tn", [BF16], _plain, tm=1408, tk=1024)[0])

    def swiglu_bwd(accs, extras):
        gt, up = extras[0].astype(F32), extras[1].astype(F32)
        sg = jax.nn.sigmoid(gt)
        dact = accs[0]
        return dact * up * (sg * (1.0 + gt * (1.0 - sg))), dact * (gt * sg)

    dgate, dup = _matmul("mm_dact", [[(dh2_16, W["w_down"])]], "nt", [BF16, BF16], swiglu_bwd,
                         extras=[(gate, "mn"), (up, "mn")], tm=1024, tn=512, sub=256, deps=[t_down])
    t_gate = send_grad("w_gate", _matmul("mm_dw_gate", [[(f, dgate)]], "tn", [BF16], _plain, tn=1408, tk=2048)[0])
    t_up = send_grad("w_up", _matmul("mm_dw_up", [[(f, dup)]], "tn", [BF16], _plain, tn=1408, tk=2048,
                                     deps=[t_gate])[0])
    df = _matmul("mm_df", [[(dgate, W["w_gate"]), (dup, W["w_up"])]], "nt", [F32], _plain, tk=1408, deps=[t_up])[0]
    dh1, dh1_16, dg_ffn = _rms_bwd("rms_ffn_bwd", df, h1, g_ffn, dh2)
    t_out = send_grad("w_out", _matmul("mm_dw_out", [[(cat, dh1_16)]], "tn", [BF16], _plain, tk=2048)[0])
    dcat = _matmul("mm_dcat", [[(dh1_16, W["w_out"])]], "nt", [F32], _plain, deps=[t_out])[0]
    j1 = [reduce_grad(k, dcat) for k in ("w_ple", "w_pgate", "w_down")]
    dc, dg_ln, db_ln, db_dw = _conv_ln_bwd(dcat, conv_c, g_conv_ln, b_conv_ln, deps=j1)
    dcv, dcg, dw_dw = _conv_bwd(z, dc, w_dw_full, 3 * DA // DC)
    dq, dk, dv = _attn_bwd(z, dcat, o_attn, lse, H)
    dz = jnp.concatenate([dq, dk, dv, dcv, dcg], axis=1)
    t_in = send_grad("w_in", _matmul("mm_dw_in", [[(a, dz)]], "tn", [BF16], _plain, tk=2048)[0])
    j2 = [reduce_grad(k, dz) for k in ("w_gate", "w_up", "w_out")]
    da = _matmul("mm_da", [[(dz, W["w_in"])]], "nt", [F32], _plain, tk=1024, deps=[t_in] + j2)[0]
    grad_x, _, dg_mix = _rms_bwd("rms_mix_bwd", da, x2, g_mix, dh1)

    wide = [dg_mix, dg_ffn, dg_ple, db_pgate, dg_final,
            jnp.concatenate([db_dw, dg_ln], axis=1), jnp.concatenate([db_ln, jnp.zeros_like(db_ln)], axis=1),
            jnp.pad(loss_part, ((0, 0), (0, D - 128))),
            dw_dw.reshape(HALO * DC // D, D)]
    small = jnp.concatenate(wide, axis=0)
    small = jnp.pad(small, ((0, -small.shape[0] % 8), (0, 0)))
    small_sum = _sum_slots("sum_small", _exchange_small(small))
    j3 = reduce_grad("w_in", small_sum)

    grads, deltas, new_m, new_v = {}, {}, {}, {}
    moments = dict(w_in=(m_w_in, v_w_in), w_out=(m_w_out, v_w_out), w_gate=(m_w_gate, v_w_gate),
                   w_up=(m_w_up, v_w_up), w_down=(m_w_down, v_w_down), w_pgate=(m_w_pgate, v_w_pgate),
                   w_ple=(m_w_ple, v_w_ple))
    last, deps = small_sum, [j3]
    for k in ("w_ple", "w_pgate", "w_down", "w_gate", "w_up", "w_out", "w_in"):
        g_k = _join_wait("join_wait_" + k, joined[k], last)
        d_, m_, v_ = _adamw("adamw_" + k, big[k][0], g_k, moments[k][0][0], moments[k][1][0], deps=deps)
        grads[k], deltas[k], new_m[k], new_v[k] = g_k[None], d_[None], m_[None], v_[None]
        last, deps = d_, ()

    half = lambda r, lo: small_sum[r:r + 1, lo * DC:(lo + 1) * DC]
    vec = dict(g_mix=small_sum[0:1], g_ffn=small_sum[1:2], g_ple=small_sum[2:3], b_pgate=small_sum[3:4],
               g_final=small_sum[4:5], b_dw=half(5, 0), g_conv_ln=half(5, 1), b_conv_ln=half(6, 0))
    loss = small_sum[7, 0]
    dw_dw_sum = small_sum[8:8 + HALO * DC // D].reshape(HALO, DC)
    s_me = 2 * lax.axis_index("x") + lax.axis_index("y")
    cs = w_dw.shape[3]
    vec["w_dw"] = lax.dynamic_slice(dw_dw_sum, (0, s_me * cs), (CONV_W, cs))
    small_w = dict(g_mix=(g_mix, m_g_mix, v_g_mix), g_ffn=(g_ffn, m_g_ffn, v_g_ffn), g_ple=(g_ple, m_g_ple, v_g_ple),
                   b_pgate=(b_pgate, m_b_pgate, v_b_pgate), g_final=(g_final, m_g_final, v_g_final),
                   b_dw=(b_dw, m_b_dw, v_b_dw), g_conv_ln=(g_conv_ln, m_g_conv_ln, v_g_conv_ln),
                   b_conv_ln=(b_conv_ln, m_b_conv_ln, v_b_conv_ln), w_dw=(w_dw, m_w_dw, v_w_dw))
    for k, (w_, m_, v_) in small_w.items():
        shape = w_.shape
        g2 = vec[k]
        to2 = lambda t: t.reshape(g2.shape)
        d_, nm_, nv_ = _adamw("adamw_" + k, to2(w_), g2, to2(m_), to2(v_))
        grads[k], deltas[k], new_m[k], new_v[k] = (t.reshape(shape) for t in (g2, d_, nm_, nv_))

    order = ["g_mix", "w_in", "w_dw", "b_dw", "g_conv_ln", "b_conv_ln", "w_out", "g_ffn", "w_gate", "w_up", "w_down",
             "g_ple", "w_pgate", "b_pgate", "w_ple", "g_final"]
    return (loss, grad_x.reshape(x.shape), *[grads[k] for k in order], *[deltas[k] for k in order],
            *[new_m[k] for k in order], *[new_v[k] for k in order])
```

```python
import functools

import jax
import jax.numpy as jnp
from jax import lax
from jax.experimental import pallas as pl
from jax.experimental.pallas import tpu as pltpu

F32 = jnp.float32
BF16 = jnp.bfloat16

EPS = 1e-6
HEAD_DIM = 128
BLK = 128
DILATIONS = (1, 4, 16)
SUPER = BLK * DILATIONS[-1]
CONV_W = 31
HALO = 32
ADAM_LR, ADAM_B1, ADAM_B2, ADAM_EPS, ADAM_WD, ADAM_STEP = 0.001, 0.9, 0.999, 1e-08, 0.01, 10

V7X_VMEM_BYTES = 64 * 1024 * 1024
VMEM_LIMIT = V7X_VMEM_BYTES * 3 // 4
VMEM_LIMIT_LARGE = V7X_VMEM_BYTES * 15 // 16
MESH = pl.DeviceIdType.MESH
ANY = pl.BlockSpec(memory_space=pl.ANY)
HBM = pl.BlockSpec(memory_space=pltpu.HBM)
SEM = pl.BlockSpec(memory_space=pltpu.SEMAPHORE)
EFFECT = pltpu.SideEffectType.DATAFLOW_SIDE_EFFECTING
TOKEN = (8, 128)


def _params(semantics=None, vmem=VMEM_LIMIT, **kw):
    return pltpu.CompilerParams(dimension_semantics=semantics, vmem_limit_bytes=vmem, **kw)


def _tile(n, want, mult=128):
    if n <= want:
        return n
    for t in range(want - want % mult, 0, -mult):
        if n % t == 0:
            return t
    raise ValueError((n, want, mult))


_DIMS = {"nn": ((1,), (0,)), "nt": ((1,), (1,)), "tn": ((0,), (0,))}


def _matmul(name, groups, mode, out_dtypes, epilogue, extras=(), tm=1024, tn=1024, tk=2048, sub=None, deps=(),
            vmem=VMEM_LIMIT):
    a0, b0 = groups[0][0]
    if mode == "nn":
        (M, K), N = a0.shape, b0.shape[1]
    elif mode == "nt":
        (M, K), N = a0.shape, b0.shape[0]
    else:
        (K, M), N = a0.shape, b0.shape[1]
    tm, tn, tk = _tile(M, tm), _tile(N, tn), _tile(K, tk)
    nk = K // tk
    if mode == "tn":
        a_spec = pl.BlockSpec((tk, tm), lambda i, j, k: (k, i))
    else:
        a_spec = pl.BlockSpec((tm, tk), lambda i, j, k: (i, k))
    if mode == "nt":
        b_spec = pl.BlockSpec((tn, tk), lambda i, j, k: (j, k))
    else:
        b_spec = pl.BlockSpec((tk, tn), lambda i, j, k: (k, j))
    operands, in_specs = [], []
    for grp in groups:
        for a, b in grp:
            operands += [a, b]
            in_specs += [a_spec, b_spec]
    for arr, kind in extras:
        operands.append(arr)
        if kind == "mn":
            in_specs.append(pl.BlockSpec((tm, tn), lambda i, j, k: (i, j)))
        else:
            in_specs.append(pl.BlockSpec((1, tn), lambda i, j, k: (0, j)))
    for tok in deps:
        operands.append(tok)
        in_specs.append(pl.BlockSpec(TOKEN, lambda i, j, k: (0, 0)))
    n_pairs = [len(g) for g in groups]
    n_ex, n_out, n_grp, n_dep = len(extras), len(out_dtypes), len(groups), len(deps)
    out_shape = [jax.ShapeDtypeStruct((M, N), dt) for dt in out_dtypes]
    out_specs = [pl.BlockSpec((tm, tn), lambda i, j, k: (i, j)) for _ in out_dtypes]

    kinds = [kind for _, kind in extras]
    sub = tm if (sub is None or nk > 1 or mode == "tn") else sub
    assert tm % sub == 0

    def body(*refs):
        ex_refs = refs[2 * sum(n_pairs):2 * sum(n_pairs) + n_ex]
        pos = 2 * sum(n_pairs) + n_ex + n_dep
        out_refs = refs[pos:pos + n_out]
        acc_refs = refs[pos + n_out:]

        def products(rows):
            pos, parts = 0, []
            for g in range(n_grp):
                part = None
                for _ in range(n_pairs[g]):
                    a_ref, b_ref = refs[pos], refs[pos + 1]
                    pos += 2
                    d = lax.dot_general(a_ref[rows].astype(BF16), b_ref[...].astype(BF16),
                                        (_DIMS[mode], ((), ())), preferred_element_type=F32)
                    part = d if part is None else part + d
                parts.append(part)
            return parts

        def finish(accs, rows):
            outs = epilogue(accs, [e[rows] if kind == "mn" else e[...] for e, kind in zip(ex_refs, kinds)])
            for o_ref, o in zip(out_refs, outs):
                o_ref[rows] = o.astype(o_ref.dtype)

        if nk == 1:
            for r in range(tm // sub):
                rows = (pl.ds(r * sub, sub), slice(None)) if sub < tm else (slice(None), slice(None))
                finish(products(rows), rows)
        else:
            k = pl.program_id(2)
            parts = products((slice(None), slice(None)))

            @pl.when(k == 0)
            def _():
                for acc, part in zip(acc_refs, parts):
                    acc[...] = part

            @pl.when(k > 0)
            def _():
                for acc, part in zip(acc_refs, parts):
                    acc[...] += part

            @pl.when(k == nk - 1)
            def _():
                finish([acc[...] for acc in acc_refs], (slice(None), slice(None)))

    scratch = [pltpu.VMEM((tm, tn), F32) for _ in range(n_grp)] if nk > 1 else []
    return pl.pallas_call(
        body, name=name, grid=(M // tm, N // tn, nk),
        in_specs=in_specs, out_specs=out_specs, out_shape=out_shape, scratch_shapes=scratch,
        compiler_params=_params(("parallel", "parallel", "arbitrary"), vmem),
    )(*operands)


def _plain(accs, extras):
    return (accs[0],)


def _add_residual(accs, extras):
    return (accs[0] + extras[0],)


def _rowwise(name, fn, rows, consts, row_outs, acc_outs, tb, deps=()):
    T = rows[0][0].shape[0]
    tb = _tile(T, tb, 16)
    operands = [r[0] for r in rows] + list(consts) + list(deps)
    in_specs = [pl.BlockSpec((tb, c), functools.partial(lambda i, cb: (i, cb), cb=cb)) for _, c, cb in rows]
    in_specs += [pl.BlockSpec(c.shape, functools.partial(lambda i, nd: (0,) * nd, nd=c.ndim)) for c in consts]
    in_specs += [pl.BlockSpec(TOKEN, lambda i: (0, 0)) for _ in deps]
    out_shape = [jax.ShapeDtypeStruct(s, dt) for s, dt, _, _ in row_outs]
    out_specs = [pl.BlockSpec((tb, c), functools.partial(lambda i, cb: (i, cb), cb=cb)) for _, _, c, cb in row_outs]
    out_shape += [jax.ShapeDtypeStruct(s, F32) for s in acc_outs]
    out_specs += [pl.BlockSpec(s, functools.partial(lambda i, nd: (0,) * nd, nd=len(s))) for s in acc_outs]
    n_rows, n_consts, n_ro, n_dep = len(rows), len(consts), len(row_outs), len(deps)

    def body(*refs):
        row_refs = refs[:n_rows]
        const_refs = refs[n_rows:n_rows + n_consts]
        out_refs = refs[n_rows + n_consts + n_dep:]
        ro, ao = fn([r[...] for r in row_refs], [c[...] for c in const_refs])
        for o_ref, o in zip(out_refs[:n_ro], ro):
            o_ref[...] = o.astype(o_ref.dtype)
        if acc_outs:
            i = pl.program_id(0)

            @pl.when(i == 0)
            def _():
                for a_ref, a in zip(out_refs[n_ro:], ao):
                    a_ref[...] = a

            @pl.when(i > 0)
            def _():
                for a_ref, a in zip(out_refs[n_ro:], ao):
                    a_ref[...] += a

    return pl.pallas_call(
        body, name=name, grid=(T // tb,), in_specs=in_specs, out_specs=out_specs, out_shape=out_shape,
        compiler_params=_params(("arbitrary",) if acc_outs else ("parallel",)),
    )(*operands)


def _colsum(v):
    return jnp.sum(v, axis=0, keepdims=True)


def _rms_fwd(name, x, g, tb=512, deps=()):
    T, D = x.shape

    def fn(rows, consts):
        xv, gv = rows[0], consts[0]
        r = lax.rsqrt(jnp.mean(xv * xv, axis=-1, keepdims=True) + EPS)
        return ((xv * r) * gv,), ()

    return _rowwise(name, fn, [(x, D, 0)], [g], [((T, D), BF16, D, 0)], [], tb, deps=deps)[0]


def _rms_bwd(name, dy, x, g, resid, tb=256):
    T, D = x.shape

    def fn(rows, consts):
        dyv, xv, rv = rows
        gv = consts[0]
        r = lax.rsqrt(jnp.mean(xv * xv, axis=-1, keepdims=True) + EPS)
        n = xv * r
        dn = dyv * gv
        dx = r * (dn - n * jnp.mean(dn * n, axis=-1, keepdims=True))
        tot = rv + dx
        return (tot, tot), (_colsum(dyv * n),)

    return _rowwise(name, fn, [(dy, D, 0), (x, D, 0), (resid, D, 0)], [g],
                    [((T, D), F32, D, 0), ((T, D), BF16, D, 0)], [(1, D)], tb)


NBLK = SUPER // BLK


def _classes(ref, dil, rows, start=0, dtype=None):
    parts = [ref[pl.ds(start + r, rows, stride=dil), :] if dil > 1 else ref[pl.ds(start, rows), :]
             for r in range(dil)]
    if dtype is not None:
        parts = [p.astype(dtype) for p in parts]
    return parts


def _keys_with_prev(ref, prev_ref, dil):
    L = SUPER // dil
    own = _classes(ref, dil, L, dtype=BF16)
    last = _classes(prev_ref, dil, BLK, start=SUPER - BLK * dil, dtype=BF16)
    blocks = []
    for r in range(dil):
        ext = jnp.concatenate([last[r], own[r]], axis=0)
        blocks += [ext[j * BLK:(j + 2) * BLK] for j in range(L // BLK)]
    return jnp.stack(blocks, axis=0)


def _band_mask(dil, has_prev):
    qi = lax.broadcasted_iota(jnp.int32, (BLK, 2 * BLK), 0)
    kj = lax.broadcasted_iota(jnp.int32, (BLK, 2 * BLK), 1)
    own = jnp.logical_and(kj >= BLK, kj - BLK <= qi)
    prev = jnp.logical_and(kj < BLK, kj >= qi)
    b = lax.broadcasted_iota(jnp.int32, (NBLK, 1, 1), 0)
    first = (b & (SUPER // (BLK * dil) - 1)) == 0
    prev_ok = jnp.logical_or(jnp.logical_not(first), has_prev)
    return jnp.logical_or(own[None], jnp.logical_and(prev[None], prev_ok))


def _bdot(a, b, ca, cb):
    return lax.dot_general(a, b, (((ca,), (cb,)), ((0,), (0,))), preferred_element_type=F32)


def _put_classes(dst, value, dil, rows, start=0, add=False, src_start=0, src_stride=None):
    src_stride = rows if src_stride is None else src_stride
    for r in range(dil):
        idx = (pl.ds(start + r, rows, stride=dil) if dil > 1 else pl.ds(start, rows), slice(None))
        part = value[src_start + r * src_stride:src_start + r * src_stride + rows]
        dst[idx] = dst[idx] + part if add else part


def _attn_fwd(z, cat_width, n_heads):
    T = z.shape[0]
    H = n_heads
    DA = H * HEAD_DIM
    nb = T // SUPER
    scale = HEAD_DIM ** -0.5

    def body(q_ref, k_ref, v_ref, kp_ref, vp_ref, cat_ref, o_ref, lse_ref, ob, lb):
        has_prev = pl.program_id(1) > 0
        for b, dil in enumerate(DILATIONS):
            L = SUPER // dil
            q3 = jnp.concatenate(_classes(q_ref, dil, L, dtype=BF16), axis=0).reshape(NBLK, BLK, HEAD_DIM)
            k3 = _keys_with_prev(k_ref, kp_ref, dil)
            v3 = _keys_with_prev(v_ref, vp_ref, dil)
            s = jnp.where(_band_mask(dil, has_prev), _bdot(q3, k3, 2, 2) * scale, -jnp.inf)
            m = jnp.max(s, axis=-1, keepdims=True)
            e = jnp.exp(s - m)
            den = jnp.sum(e, axis=-1, keepdims=True)
            o3 = _bdot((e / den).astype(BF16), v3, 2, 1)
            lse3 = jnp.broadcast_to(m + jnp.log(den), (NBLK, BLK, HEAD_DIM))
            _put_classes(ob.at[b], o3.reshape(SUPER, HEAD_DIM), dil, L)
            _put_classes(lb.at[b], lse3.reshape(SUPER, HEAD_DIM), dil, L)
        l0, l1, l2 = lb[0], lb[1], lb[2]
        mx = jnp.maximum(jnp.maximum(l0, l1), l2)
        tot = mx + jnp.log(jnp.exp(l0 - mx) + jnp.exp(l1 - mx) + jnp.exp(l2 - mx))
        o = jnp.exp(l0 - tot) * ob[0] + jnp.exp(l1 - tot) * ob[1] + jnp.exp(l2 - tot) * ob[2]
        o_ref[...] = o
        cat_ref[...] = o.astype(BF16)
        lse_ref[...] = tot

    blk = (SUPER, HEAD_DIM)
    in_specs = [
        pl.BlockSpec(blk, lambda h, n: (n, h)),
        pl.BlockSpec(blk, lambda h, n: (n, H + h)),
        pl.BlockSpec(blk, lambda h, n: (n, 2 * H + h)),
        pl.BlockSpec(blk, lambda h, n: (jnp.maximum(n - 1, 0), H + h)),
        pl.BlockSpec(blk, lambda h, n: (jnp.maximum(n - 1, 0), 2 * H + h)),
    ]
    out_spec = pl.BlockSpec(blk, lambda h, n: (n, h))
    return pl.pallas_call(
        body, name="attn_fwd", grid=(H, nb), in_specs=in_specs, out_specs=[out_spec] * 3,
        out_shape=[jax.ShapeDtypeStruct((T, cat_width), BF16), jax.ShapeDtypeStruct((T, DA), F32),
                   jax.ShapeDtypeStruct((T, DA), F32)],
        scratch_shapes=[pltpu.VMEM((3, SUPER, HEAD_DIM), F32), pltpu.VMEM((3, SUPER, HEAD_DIM), F32)],
        compiler_params=_params(("parallel", "parallel")),
    )(z, z, z, z, z)


def _attn_bwd(z, dcat, o, lse, n_heads):
    T = z.shape[0]
    H = n_heads
    nb = T // SUPER
    scale = HEAD_DIM ** -0.5

    def body(q_ref, k_ref, v_ref, kp_ref, vp_ref, do_ref, o_ref, lse_ref, dz_q, dz_k, dz_v,
             dq_acc, dk_acc, dv_acc, dkp_acc, dvp_acc, dsum):
        i = pl.program_id(1)
        has_prev = i < nb - 1

        @pl.when(i == 0)
        def _():
            dk_acc[...] = jnp.zeros_like(dk_acc)
            dv_acc[...] = jnp.zeros_like(dv_acc)

        @pl.when(i > 0)
        def _():
            dk_acc[...] = dkp_acc[...]
            dv_acc[...] = dvp_acc[...]

        dq_acc[...] = jnp.zeros_like(dq_acc)
        dkp_acc[...] = jnp.zeros_like(dkp_acc)
        dvp_acc[...] = jnp.zeros_like(dvp_acc)
        dsum[...] = jnp.broadcast_to(jnp.sum(do_ref[...] * o_ref[...], axis=-1, keepdims=True), (SUPER, HEAD_DIM))
        for dil in DILATIONS:
            L = SUPER // dil
            wide = lambda ref: jnp.tile(jnp.concatenate(_classes(ref, dil, L), axis=0).reshape(NBLK, BLK, HEAD_DIM),
                                        (1, 1, 2))
            q3 = jnp.concatenate(_classes(q_ref, dil, L, dtype=BF16), axis=0).reshape(NBLK, BLK, HEAD_DIM)
            do3 = jnp.concatenate(_classes(do_ref, dil, L, dtype=BF16), axis=0).reshape(NBLK, BLK, HEAD_DIM)
            k3 = _keys_with_prev(k_ref, kp_ref, dil)
            v3 = _keys_with_prev(v_ref, vp_ref, dil)
            p = jnp.where(_band_mask(dil, has_prev), jnp.exp(_bdot(q3, k3, 2, 2) * scale - wide(lse_ref)), 0.0)
            ds = (p * (_bdot(do3, v3, 2, 2) - wide(dsum)) * scale).astype(BF16)
            dq = _bdot(ds, k3, 2, 1).reshape(SUPER, HEAD_DIM)
            dk = _bdot(ds, q3, 1, 1)
            dv = _bdot(p.astype(BF16), do3, 1, 1)
            _put_classes(dq_acc, dq, dil, L, add=True)
            for acc, prev_acc, g in ((dk_acc, dkp_acc, dk), (dv_acc, dvp_acc, dv)):
                _put_classes(acc, g[:, BLK:, :].reshape(SUPER, HEAD_DIM), dil, L, add=True)
                to_prev = g[:, :BLK, :].reshape(SUPER, HEAD_DIM)
                if L > BLK:
                    _put_classes(acc, to_prev, dil, L - BLK, add=True, src_start=BLK, src_stride=L)
                _put_classes(prev_acc, to_prev, dil, BLK, start=SUPER - BLK * dil, add=True, src_stride=L)
        dz_q[...] = dq_acc[...].astype(BF16)
        dz_k[...] = dk_acc[...].astype(BF16)
        dz_v[...] = dv_acc[...].astype(BF16)

    blk = (SUPER, HEAD_DIM)
    row = lambda i: nb - 1 - i
    in_specs = [
        pl.BlockSpec(blk, lambda h, i: (row(i), h)),
        pl.BlockSpec(blk, lambda h, i: (row(i), H + h)),
        pl.BlockSpec(blk, lambda h, i: (row(i), 2 * H + h)),
        pl.BlockSpec(blk, lambda h, i: (jnp.maximum(row(i) - 1, 0), H + h)),
        pl.BlockSpec(blk, lambda h, i: (jnp.maximum(row(i) - 1, 0), 2 * H + h)),
        pl.BlockSpec(blk, lambda h, i: (row(i), h)),
        pl.BlockSpec(blk, lambda h, i: (row(i), h)),
        pl.BlockSpec(blk, lambda h, i: (row(i), h)),
    ]
    out_spec = pl.BlockSpec(blk, lambda h, i: (row(i), h))
    return pl.pallas_call(
        body, name="attn_bwd", grid=(H, nb), in_specs=in_specs, out_specs=[out_spec] * 3,
        out_shape=[jax.ShapeDtypeStruct((T, H * HEAD_DIM), BF16)] * 3,
        scratch_shapes=[pltpu.VMEM(blk, F32) for _ in range(6)],
        compiler_params=_params(("parallel", "arbitrary")),
    )(z, z, z, z, z, dcat, o, lse)


def _glu(cv, cg):
    return cv * jax.nn.sigmoid(cg)


SUBLANES = 8


def _fill_shifted(sh, ext, rows):
    for b in range(1, SUBLANES):
        sh[b - 1, pl.ds(0, rows - SUBLANES), :] = ext[pl.ds(b, rows - SUBLANES), :]


def _window(sh, ext, offset, rows):
    b = offset % SUBLANES
    if b == 0:
        return ext[pl.ds(offset, rows), :]
    return sh[b - 1, pl.ds(offset - b, rows), :]


def _conv_fwd(z, cat, w_dw, b_dw, g_ln, b_ln, col0, tb=512):
    T = z.shape[0]
    DC = w_dw.shape[1]
    tb = _tile(T, tb, HALO)
    hb = tb // HALO
    cat_cb = cat.shape[1] // DC - 1

    def body(cv_ref, cg_ref, cvh_ref, cgh_ref, w_ref, bdw_ref, g_ref, b_ref, cat_in, cat_ref, c_ref, u_ext, sh):
        i = pl.program_id(0)
        halo = _glu(cvh_ref[...], cgh_ref[...])
        u_ext[pl.ds(0, HALO), :] = jnp.where(i > 0, halo, 0.0)
        u_ext[pl.ds(HALO, tb), :] = _glu(cv_ref[...], cg_ref[...])
        _fill_shifted(sh, u_ext, tb + HALO)
        acc = jnp.broadcast_to(bdw_ref[...], (tb, DC))
        for j in range(CONV_W):
            acc = acc + w_ref[pl.ds(j, 1), :] * _window(sh, u_ext, HALO - (CONV_W - 1) + j, tb)
        c_ref[...] = acc
        mu = jnp.mean(acc, axis=-1, keepdims=True)
        var = jnp.mean(jnp.square(acc - mu), axis=-1, keepdims=True)
        y = (acc - mu) * lax.rsqrt(var + EPS) * g_ref[...] + b_ref[...]
        cat_ref[...] = (y * jax.nn.sigmoid(y)).astype(BF16)

    cur = lambda cb: pl.BlockSpec((tb, DC), lambda i: (i, cb))
    halo = lambda cb: pl.BlockSpec((HALO, DC), lambda i: (jnp.maximum(i * hb - 1, 0), cb))
    whole = lambda a: pl.BlockSpec(a.shape, lambda i: (0, 0))
    return pl.pallas_call(
        body, name="conv_fwd", grid=(T // tb,),
        in_specs=[cur(col0), cur(col0 + 1), halo(col0), halo(col0 + 1), whole(w_dw), whole(b_dw), whole(g_ln),
                  whole(b_ln), ANY],
        out_specs=[pl.BlockSpec((tb, DC), lambda i: (i, cat_cb)), pl.BlockSpec((tb, DC), lambda i: (i, 0))],
        out_shape=[jax.ShapeDtypeStruct(cat.shape, cat.dtype), jax.ShapeDtypeStruct((T, DC), F32)],
        scratch_shapes=[pltpu.VMEM((tb + HALO, DC), F32), pltpu.VMEM((SUBLANES - 1, tb + HALO, DC), F32)],
        input_output_aliases={8: 0},
        compiler_params=_params(("parallel",)),
    )(z, z, z, z, w_dw, b_dw, g_ln, b_ln, cat)


def _conv_ln_bwd(dcat, c, g_ln, b_ln, tb=256, deps=()):
    T, DC = c.shape
    d_cb = dcat.shape[1] // DC - 1

    def fn(rows, consts):
        dov, cv_ = rows
        gv, bv = consts
        mu = jnp.mean(cv_, axis=-1, keepdims=True)
        xc = cv_ - mu
        rstd = lax.rsqrt(jnp.mean(jnp.square(xc), axis=-1, keepdims=True) + EPS)
        ln = xc * rstd
        y = ln * gv + bv
        sg = jax.nn.sigmoid(y)
        dy = dov * (sg * (1.0 + y * (1.0 - sg)))
        dln = dy * gv
        dc = rstd * (dln - jnp.mean(dln, axis=-1, keepdims=True) - ln * jnp.mean(dln * ln, axis=-1, keepdims=True))
        return (dc,), (_colsum(dy * ln), _colsum(dy), _colsum(dc))

    return _rowwise("conv_ln_bwd", fn, [(dcat, DC, d_cb), (c, DC, 0)], [g_ln, b_ln],
                    [((T, DC), F32, DC, 0)], [(1, DC)] * 3, tb, deps=deps)


def _conv_bwd(z, dc, w_dw, col0, tb=512):
    T, DC = dc.shape
    tb = _tile(T, tb, HALO)
    hb = tb // HALO
    nblk = T // tb

    def body(cv_ref, cg_ref, dc_ref, dcn_ref, w_ref, dcv_ref, dcg_ref, dw_ref, dc_ext, sh):
        i = pl.program_id(0)
        cv, cg = cv_ref[...], cg_ref[...]
        sg = jax.nn.sigmoid(cg)
        u = cv * sg
        dc_ext[pl.ds(0, tb), :] = dc_ref[...]
        dc_ext[pl.ds(tb, HALO), :] = jnp.where(i < nblk - 1, dcn_ref[...], 0.0)
        _fill_shifted(sh, dc_ext, tb + HALO)

        @pl.when(i == 0)
        def _():
            dw_ref[...] = jnp.zeros_like(dw_ref)

        du = jnp.zeros((tb, DC), F32)
        for j in range(CONV_W):
            d_j = _window(sh, dc_ext, CONV_W - 1 - j, tb)
            du = du + w_ref[pl.ds(j, 1), :] * d_j
            dw_ref[pl.ds(j, 1), :] += _colsum(u * d_j)
        dcv_ref[...] = (du * sg).astype(BF16)
        dcg_ref[...] = (du * cv * sg * (1.0 - sg)).astype(BF16)

    cur = lambda cb: pl.BlockSpec((tb, DC), lambda i: (i, cb))
    nxt = pl.BlockSpec((HALO, DC), lambda i: (jnp.minimum((i + 1) * hb, T // HALO - 1), 0))
    return pl.pallas_call(
        body, name="conv_bwd", grid=(nblk,),
        in_specs=[cur(col0), cur(col0 + 1), cur(0), nxt, pl.BlockSpec(w_dw.shape, lambda i: (0, 0))],
        out_specs=[cur(0), cur(0), pl.BlockSpec((HALO, DC), lambda i: (0, 0))],
        out_shape=[jax.ShapeDtypeStruct((T, DC), BF16), jax.ShapeDtypeStruct((T, DC), BF16),
                   jax.ShapeDtypeStruct((HALO, DC), F32)],
        scratch_shapes=[pltpu.VMEM((tb + HALO, DC), F32), pltpu.VMEM((SUBLANES - 1, tb + HALO, DC), F32)],
        compiler_params=_params(("arbitrary",)),
    )(z, z, dc, dc, w_dw)


def _loss_head(h3, target, g_final, gte, e, tb=256):
    T, D = h3.shape

    def fn(rows, consts):
        hv, tv, gt, ev = rows
        gv = consts[0]
        gt = gt.astype(F32)
        ev = ev.astype(F32)
        r = lax.rsqrt(jnp.mean(hv * hv, axis=-1, keepdims=True) + EPS)
        n = hv * r
        diff = n * gv - tv
        loss = 0.5 * jnp.sum(jnp.mean(jnp.square(diff), axis=-1, keepdims=True), axis=0, keepdims=True)
        dy = diff * (1.0 / D)
        dn = dy * gv
        dh = r * (dn - n * jnp.mean(dn * n, axis=-1, keepdims=True))
        du4 = dh * ev * gt * (1.0 - gt)
        return (dh, dh * gt, du4), (_colsum(dy * n), _colsum(du4), jnp.broadcast_to(loss, (1, 128)))

    return _rowwise("loss_head", fn, [(h3, D, 0), (target, D, 0), (gte, D, 0), (e, D, 0)], [g_final],
                    [((T, D), F32, D, 0), ((T, D), BF16, D, 0), ((T, D), BF16, D, 0)],
                    [(1, D), (1, D), (1, 128)], tb)


def _me():
    return lax.axis_index("x"), lax.axis_index("y"), lax.axis_index("c")


def _chips3(x, y):
    return [(1 - x, y), (x, 1 - y), (1 - x, 1 - y)]


def _peers7(x, y, c):
    for m in range(1, 8):
        yield m - 1, (x ^ (m >> 2), y ^ ((m >> 1) & 1), c ^ (m & 1))


def _shard_of(ref, axis, s):
    R, C = ref.shape
    if axis == 1:
        return ref.at[:, pl.ds(s * (C // 4), C // 4)]
    return ref.at[pl.ds(s * (R // 4), R // 4), :]


def _region(ref, axis, shard, half):
    R, C = ref.shape
    if axis == 1:
        cs, hr = C // 4, R // 2
        return ref.at[pl.ds(half * hr, hr), pl.ds(shard * cs, cs)]
    hr = R // 8
    return ref.at[pl.ds(shard * 2 * hr + half * hr, hr), :]


def _gather_now(shard, small):
    R, C = shard.shape
    cs = small.shape[1]

    def body(shard_ref, small_ref, out_ref, small_out, send, recv, fsend, frecv, lsem):
        x, y, c = _me()
        me_s = 2 * x + y
        sibling = (x, y, 1 - c)
        chips = _chips3(x, y)
        half = shard_ref.at[pl.ds(c * (R // 2), R // 2), :]
        locals_ = [pltpu.make_async_copy(shard_ref, _shard_of(out_ref, 1, me_s), lsem.at[0]),
                   pltpu.make_async_copy(small_ref, _shard_of(small_out, 1, me_s), lsem.at[1])]
        for cp in locals_:
            cp.start()
        firsts = []
        for j, (px, py) in enumerate(chips):
            firsts.append(pltpu.make_async_remote_copy(
                src_ref=half, dst_ref=_region(out_ref, 1, me_s, c), send_sem=send.at[0, j], recv_sem=recv.at[0, j],
                device_id=(px, py, c), device_id_type=MESH))
            firsts.append(pltpu.make_async_remote_copy(
                src_ref=small_ref, dst_ref=_shard_of(small_out, 1, me_s), send_sem=send.at[1, j],
                recv_sem=recv.at[1, j], device_id=(px, py, c), device_id_type=MESH))
        for cp in firsts:
            cp.start()
        relays = []
        for j, (px, py) in enumerate(chips):
            landed = _region(out_ref, 1, 2 * px + py, c)
            pltpu.make_async_remote_copy(
                src_ref=half, dst_ref=landed, send_sem=send.at[0, j], recv_sem=recv.at[0, j],
                device_id=(px, py, c), device_id_type=MESH).wait_recv()
            relay = pltpu.make_async_remote_copy(
                src_ref=landed, dst_ref=landed, send_sem=fsend.at[j], recv_sem=frecv.at[j],
                device_id=sibling, device_id_type=MESH)
            relay.start()
            relays.append(relay)
        for j, (px, py) in enumerate(chips):
            pltpu.make_async_remote_copy(
                src_ref=small_ref, dst_ref=_shard_of(small_out, 1, 2 * px + py), send_sem=send.at[1, j],
                recv_sem=recv.at[1, j], device_id=(px, py, c), device_id_type=MESH).wait_recv()
            theirs = _region(out_ref, 1, 2 * px + py, 1 - c)
            pltpu.make_async_remote_copy(
                src_ref=theirs, dst_ref=theirs, send_sem=fsend.at[j], recv_sem=frecv.at[j],
                device_id=sibling, device_id_type=MESH).wait_recv()
        for cp in firsts + relays:
            cp.wait_send()
        for cp in locals_:
            cp.wait()

    return pl.pallas_call(
        body, name="gather_now", in_specs=[ANY, ANY], out_specs=[ANY, ANY],
        out_shape=[jax.ShapeDtypeStruct((R, 4 * C), shard.dtype),
                   jax.ShapeDtypeStruct((small.shape[0], 4 * cs), small.dtype)],
        scratch_shapes=[pltpu.SemaphoreType.DMA((2, 3)), pltpu.SemaphoreType.DMA((2, 3)),
                        pltpu.SemaphoreType.DMA((3,)), pltpu.SemaphoreType.DMA((3,)), pltpu.SemaphoreType.DMA((2,))],
        compiler_params=_params(),
    )(shard, small)


def _exchange_small(small):
    def body(small_ref, out_ref, send, recv, lsem):
        x, y, c = _me()
        me = 4 * x + 2 * y + c
        own = pltpu.make_async_copy(small_ref, out_ref.at[me], lsem)
        own.start()
        sends = [pltpu.make_async_remote_copy(
            src_ref=small_ref, dst_ref=out_ref.at[me], send_sem=send.at[m], recv_sem=recv.at[m],
            device_id=peer, device_id_type=MESH) for m, peer in _peers7(x, y, c)]
        for cp in sends:
            cp.start()
        for m, (px, py, pc) in _peers7(x, y, c):
            pltpu.make_async_remote_copy(
                src_ref=small_ref, dst_ref=out_ref.at[4 * px + 2 * py + pc], send_sem=send.at[m], recv_sem=recv.at[m],
                device_id=(px, py, pc), device_id_type=MESH).wait_recv()
        for cp in sends:
            cp.wait_send()
        own.wait()

    return pl.pallas_call(
        body, name="exchange_small", in_specs=[ANY], out_specs=ANY,
        out_shape=jax.ShapeDtypeStruct((8,) + small.shape, small.dtype),
        scratch_shapes=[pltpu.SemaphoreType.DMA((7,)), pltpu.SemaphoreType.DMA((7,)), pltpu.SemaphoreType.DMA(())],
        compiler_params=_params(),
    )(small)


def _sum_slots(name, slots, tr=256):
    S, R, C = slots.shape
    tr = _tile(R, tr, 16)

    def body(s_ref, o_ref):
        acc = s_ref[0].astype(F32)
        for s in range(1, S):
            acc = acc + s_ref[s].astype(F32)
        o_ref[...] = acc

    return pl.pallas_call(
        body, name=name, grid=(R // tr,), in_specs=[pl.BlockSpec((S, tr, C), lambda i: (0, i, 0))],
        out_specs=pl.BlockSpec((tr, C), lambda i: (i, 0)), out_shape=jax.ShapeDtypeStruct((R, C), F32),
        compiler_params=_params(("parallel",)),
    )(slots)


def _place_shard(name, shard, axis):
    R, C = shard.shape
    full_shape = (R, 4 * C) if axis == 1 else (4 * R, C)

    def body(shard_ref, full_ref, sem):
        x, y, c = _me()
        cp = pltpu.make_async_copy(shard_ref, _shard_of(full_ref, axis, 2 * x + y), sem)
        cp.start()
        cp.wait()

    return pl.pallas_call(
        body, name=name, in_specs=[ANY], out_specs=ANY, out_shape=jax.ShapeDtypeStruct(full_shape, shard.dtype),
        scratch_shapes=[pltpu.SemaphoreType.DMA(())], compiler_params=_params(),
    )(shard)


def _gather_start(name, shard, axis, after):
    full = _place_shard(name + "_place", shard, axis)
    full_shape = full.shape

    def body(shard_ref, full_ref, after_ref, send, recv, shard_thru, full_thru, token):
        x, y, c = _me()
        for j, (px, py) in enumerate(_chips3(x, y)):
            pltpu.make_async_remote_copy(
                src_ref=shard_ref, dst_ref=_shard_of(full_ref, axis, 2 * x + y), send_sem=send.at[j],
                recv_sem=recv.at[j], device_id=(px, py, c), device_id_type=MESH).start()
        token[...] = jnp.zeros_like(token)

    return pl.pallas_call(
        body, name=name,
        out_shape=(pltpu.SemaphoreType.DMA((3,)), pltpu.SemaphoreType.DMA((3,)), pltpu.HBM(shard.shape, shard.dtype),
                   pltpu.HBM(full_shape, shard.dtype), jax.ShapeDtypeStruct(TOKEN, F32)),
        in_specs=(HBM, HBM, ANY), out_specs=(SEM, SEM, HBM, HBM, pl.BlockSpec(memory_space=pltpu.VMEM)),
        input_output_aliases={0: 2, 1: 3}, compiler_params=pltpu.CompilerParams(has_side_effects=EFFECT),
    )(pltpu.with_memory_space_constraint(shard, pltpu.HBM), pltpu.with_memory_space_constraint(full, pltpu.HBM), after)


def _gather_wait(name, started, axis, after):
    send, recv, shard_thru, full_thru, _ = started

    def body(shard_ref, full_ref, send, recv, after_ref, shard_dead, full_out):
        x, y, c = _me()
        for j, (px, py) in enumerate(_chips3(x, y)):
            cp = pltpu.make_async_remote_copy(
                src_ref=shard_ref, dst_ref=_shard_of(full_ref, axis, 2 * px + py), send_sem=send.at[j],
                recv_sem=recv.at[j], device_id=(px, py, c), device_id_type=MESH)
            cp.wait_send()
            cp.wait_recv()

    return pl.pallas_call(
        body, name=name,
        out_shape=(pltpu.HBM(shard_thru.shape, shard_thru.dtype), pltpu.HBM(full_thru.shape, full_thru.dtype)),
        in_specs=(HBM, HBM, SEM, SEM, ANY), out_specs=(HBM, HBM), input_output_aliases={0: 0, 1: 1},
        compiler_params=pltpu.CompilerParams(has_side_effects=EFFECT),
    )(shard_thru, full_thru, send, recv, after)[1]


def _piece_shape(shape, axis):
    R, C = shape
    return (R // 2, C // 4) if axis == 1 else (R // 8, C)


def _scatter_start(name, g, axis):
    land_shape = (7,) + _piece_shape(g.shape, axis)

    def body(g_ref, land_ref, send, recv, g_thru, land_thru, token):
        x, y, c = _me()
        for m, (px, py, pc) in _peers7(x, y, c):
            pltpu.make_async_remote_copy(
                src_ref=_region(g_ref, axis, 2 * px + py, pc), dst_ref=land_ref.at[m], send_sem=send.at[m],
                recv_sem=recv.at[m], device_id=(px, py, pc), device_id_type=MESH).start()
        token[...] = jnp.zeros_like(token)

    return pl.pallas_call(
        body, name=name,
        out_shape=(pltpu.SemaphoreType.DMA((7,)), pltpu.SemaphoreType.DMA((7,)), pltpu.HBM(g.shape, g.dtype),
                   pltpu.HBM(land_shape, g.dtype), jax.ShapeDtypeStruct(TOKEN, F32)),
        in_specs=(HBM, HBM), out_specs=(SEM, SEM, HBM, HBM, pl.BlockSpec(memory_space=pltpu.VMEM)),
        input_output_aliases={0: 2, 1: 3}, compiler_params=pltpu.CompilerParams(has_side_effects=EFFECT),
    )(pltpu.with_memory_space_constraint(g, pltpu.HBM),
      pltpu.with_memory_space_constraint(lax.empty(land_shape, g.dtype), pltpu.HBM))


def _scatter_wait(name, started, axis, after):
    send, recv, g_thru, land_thru, _ = started

    def body(g_ref, land_ref, send, recv, after_ref, g_out, land_out):
        x, y, c = _me()
        for m, (px, py, pc) in _peers7(x, y, c):
            cp = pltpu.make_async_remote_copy(
                src_ref=_region(g_ref, axis, 2 * px + py, pc), dst_ref=land_ref.at[m], send_sem=send.at[m],
                recv_sem=recv.at[m], device_id=(px, py, pc), device_id_type=MESH)
            cp.wait_send()
            cp.wait_recv()

    return pl.pallas_call(
        body, name=name,
        out_shape=(pltpu.HBM(g_thru.shape, g_thru.dtype), pltpu.HBM(land_thru.shape, land_thru.dtype)),
        in_specs=(HBM, HBM, SEM, SEM, ANY), out_specs=(HBM, HBM), input_output_aliases={0: 0, 1: 1},
        compiler_params=pltpu.CompilerParams(has_side_effects=EFFECT),
    )(g_thru, land_thru, send, recv, after)


def _own_piece(g, axis):
    x, y, c = _me()
    pr, pc_ = _piece_shape(g.shape, axis)
    if axis == 1:
        return lax.dynamic_slice(g, (c * pr, (2 * x + y) * pc_), (pr, pc_))
    return lax.dynamic_slice(g, ((2 * x + y) * 2 * pr + c * pr, 0), (pr, pc_))


def _sum_pieces(name, own, slots, tr=256):
    S, R, C = slots.shape
    tr = _tile(R, tr, 16)
    nblk = R // tr
    c_arr = lax.axis_index("c").astype(jnp.int32).reshape(1)

    def body(c_ref, own_ref, s_ref, o_ref):
        acc = own_ref[...].astype(F32)
        for s in range(S):
            acc = acc + s_ref[s].astype(F32)
        o_ref[...] = acc

    grid_spec = pltpu.PrefetchScalarGridSpec(
        num_scalar_prefetch=1, grid=(nblk,),
        in_specs=[pl.BlockSpec((tr, C), lambda i, c_ref: (i, 0)), pl.BlockSpec((S, tr, C), lambda i, c_ref: (0, i, 0))],
        out_specs=pl.BlockSpec((tr, C), lambda i, c_ref: (c_ref[0] * nblk + i, 0)))
    return pl.pallas_call(
        body, name=name, grid_spec=grid_spec, out_shape=jax.ShapeDtypeStruct((2 * R, C), F32),
        compiler_params=_params(("parallel",)),
    )(c_arr, own, slots)


def _join_start(name, buf):
    def body(buf_ref, send, recv, buf_thru, token):
        x, y, c = _me()
        hr = buf_ref.shape[0] // 2
        mine = buf_ref.at[pl.ds(c * hr, hr), :]
        pltpu.make_async_remote_copy(src_ref=mine, dst_ref=mine, send_sem=send, recv_sem=recv,
                                     device_id=(x, y, 1 - c), device_id_type=MESH).start()
        token[...] = jnp.zeros_like(token)

    return pl.pallas_call(
        body, name=name,
        out_shape=(pltpu.SemaphoreType.DMA(()), pltpu.SemaphoreType.DMA(()), pltpu.HBM(buf.shape, buf.dtype),
                   jax.ShapeDtypeStruct(TOKEN, F32)),
        in_specs=(HBM,), out_specs=(SEM, SEM, HBM, pl.BlockSpec(memory_space=pltpu.VMEM)),
        input_output_aliases={0: 2}, compiler_params=pltpu.CompilerParams(has_side_effects=EFFECT),
    )(pltpu.with_memory_space_constraint(buf, pltpu.HBM))


def _join_wait(name, started, after):
    send, recv, buf_thru, _ = started

    def body(buf_ref, send, recv, after_ref, buf_out):
        x, y, c = _me()
        hr = buf_ref.shape[0] // 2
        theirs = buf_ref.at[pl.ds((1 - c) * hr, hr), :]
        cp = pltpu.make_async_remote_copy(src_ref=theirs, dst_ref=theirs, send_sem=send, recv_sem=recv,
                                          device_id=(x, y, 1 - c), device_id_type=MESH)
        cp.wait_send()
        cp.wait_recv()

    return pl.pallas_call(
        body, name=name, out_shape=pltpu.HBM(buf_thru.shape, buf_thru.dtype),
        in_specs=(HBM, SEM, SEM, ANY), out_specs=HBM, input_output_aliases={0: 0},
        compiler_params=pltpu.CompilerParams(has_side_effects=EFFECT),
    )(buf_thru, send, recv, after)


def _adamw(name, w, g, m, v, tr=256, deps=()):
    R, C = w.shape
    tr = _tile(R, tr, 8)
    c1 = 1.0 - ADAM_B1 ** ADAM_STEP
    c2 = 1.0 - ADAM_B2 ** ADAM_STEP

    def body(w_ref, g_ref, m_ref, v_ref, *rest):
        d_ref, nm_ref, nv_ref = rest[len(deps):]
        gv = g_ref[...]
        nm = ADAM_B1 * m_ref[...] + (1.0 - ADAM_B1) * gv
        nv = ADAM_B2 * v_ref[...] + (1.0 - ADAM_B2) * jnp.square(gv)
        d_ref[...] = -ADAM_LR * ((nm / c1) / (jnp.sqrt(nv / c2) + ADAM_EPS) + ADAM_WD * w_ref[...])
        nm_ref[...] = nm
        nv_ref[...] = nv

    spec = pl.BlockSpec((tr, C), lambda i: (i, 0))
    return pl.pallas_call(
        body, name=name, grid=(R // tr,), in_specs=[spec] * 4 + [pl.BlockSpec(TOKEN, lambda i: (0, 0))] * len(deps),
        out_specs=[spec] * 3, out_shape=[jax.ShapeDtypeStruct((R, C), F32)] * 3,
        compiler_params=_params(("parallel",)),
    )(w, g, m, v, *deps)


def kernel(x, p, g_mix, w_in, w_dw, b_dw, g_conv_ln, b_conv_ln, w_out, g_ffn, w_gate, w_up, w_down, g_ple, w_pgate, b_pgate, w_ple, g_final, loss_target, m_g_mix, m_w_in, m_w_dw, m_b_dw, m_g_conv_ln, m_b_conv_ln, m_w_out, m_g_ffn, m_w_gate, m_w_up, m_w_down, m_g_ple, m_w_pgate, m_b_pgate, m_w_ple, m_g_final, v_g_mix, v_w_in, v_w_dw, v_b_dw, v_g_conv_ln, v_b_conv_ln, v_w_out, v_g_ffn, v_w_gate, v_w_up, v_w_down, v_g_ple, v_w_pgate, v_b_pgate, v_w_ple, v_g_final):
    T, D = x.shape[1], x.shape[2]
    DC = b_dw.shape[1]
    DA = D - DC
    H = DA // HEAD_DIM
    DP = p.shape[3]
    assert T % SUPER == 0 and DA == DC
    x2 = x.reshape(T, D)
    p2 = p.reshape(T, DP)
    tgt = loss_target.reshape(T, D)
    g_final2 = g_final.reshape(1, D)

    big = dict(w_in=(w_in[0], 1), w_out=(w_out[0], 0), w_gate=(w_gate[0], 1), w_up=(w_up[0], 1),
               w_down=(w_down[0], 0), w_pgate=(w_pgate[0], 0), w_ple=(w_ple[0], 1))
    axis_of = {k: big[k][1] for k in big}
    dw_shard = jnp.pad(w_dw.reshape(CONV_W, -1), ((0, HALO - CONV_W), (0, 0)))
    w_in_full, w_dw_full = _gather_now(big["w_in"][0].astype(BF16), dw_shard)
    travelling = {}

    def fetch(keys, after):
        for k in keys:
            travelling[k] = _gather_start("gather_start_" + k, big[k][0].astype(BF16), axis_of[k], after)
        return [travelling[k][-1] for k in keys]

    issued = fetch(["w_out", "w_gate", "w_up"], w_in_full)

    def weight(k, after):
        return _gather_wait("gather_wait_" + k, travelling[k], axis_of[k], after)

    a = _rms_fwd("rms_mix", x2, g_mix, deps=issued)
    z = _matmul("mm_in", [[(a, w_in_full)]], "nn", [F32], _plain)[0]
    cat, o_attn, lse = _attn_fwd(z, D, H)
    cat, conv_c = _conv_fwd(z, cat, w_dw_full, b_dw, g_conv_ln, b_conv_ln, 3 * DA // DC)
    W = dict(w_in=w_in_full)
    W["w_out"] = weight("w_out", cat)
    h1 = _matmul("mm_out", [[(cat, W["w_out"])]], "nn", [F32], _add_residual, extras=[(x2, "mn")])[0]
    f = _rms_fwd("rms_ffn", h1, g_ffn)
    W["w_gate"] = weight("w_gate", f)
    W["w_up"] = weight("w_up", f)

    def swiglu(accs, extras):
        gt, up = accs
        return gt, up, (gt * jax.nn.sigmoid(gt)) * up

    gate, up, act = _matmul("mm_gate_up", [[(f, W["w_gate"])], [(f, W["w_up"])]], "nn", [BF16, BF16, BF16],
                            swiglu, tm=1024, tn=512, sub=256,
                            deps=fetch(["w_down", "w_pgate", "w_ple"], W["w_up"]))
    W["w_down"] = weight("w_down", act)
    h2 = _matmul("mm_down", [[(act, W["w_down"])]], "nn", [F32], _add_residual, extras=[(h1, "mn")], tm=512, tk=5632)[0]
    n2 = _rms_fwd("rms_ple", h2, g_ple)
    W["w_ple"] = weight("w_ple", n2)
    W["w_pgate"] = weight("w_pgate", n2)
    e = _matmul("mm_ple", [[(p2, W["w_ple"])]], "nn", [F32], _plain)[0]

    def ple_gate(accs, extras):
        bias, hv, ev = extras
        gt = jax.nn.sigmoid(accs[0] + bias)
        return hv + ev * gt, gt, ev

    h3, gte, e16 = _matmul("mm_pgate", [[(n2, W["w_pgate"])]], "nn", [F32, BF16, BF16], ple_gate,
                           extras=[(b_pgate, "n"), (h2, "mn"), (e, "mn")], tm=512, sub=256)

    sent, joined = {}, {}

    def send_grad(k, g):
        sent[k] = _scatter_start("scatter_start_" + k, g, axis_of[k])
        return sent[k][-1]

    def reduce_grad(k, after):
        g_thru, land = _scatter_wait("scatter_wait_" + k, sent[k], axis_of[k], after)
        half = _sum_pieces("sum_" + k, _own_piece(g_thru, axis_of[k]), land)
        joined[k] = _join_start("join_start_" + k, half)
        return joined[k][-1]

    dh3, de, du4, dg_final, db_pgate, loss_part = _loss_head(h3, tgt, g_final2, gte, e16)
    t_ple = send_grad("w_ple", _matmul("mm_dw_ple", [[(p2, de)]], "tn", [BF16], _plain, tk=1024)[0])
    t_pgate = send_grad("w_pgate", _matmul("mm_dw_pgate", [[(n2, du4)]], "tn", [BF16], _plain, tk=2048)[0])
    dn2 = _matmul("mm_dn2", [[(du4, W["w_pgate"])]], "nt", [F32], _plain, deps=[t_ple, t_pgate])[0]
    dh2, dh2_16, dg_ple = _rms_bwd("rms_ple_bwd", dn2, h2, g_ple, dh3)
    t_down = send_grad("w_down", _matmul("mm_dw_down", [[(act, dh2_16)]], "tn", [BF16], _plain, tm=1408, tk=1024)[0])

    def swiglu_bwd(accs, extras):
        gt, up = extras[0].astype(F32), extras[1].astype(F32)
        sg = jax.nn.sigmoid(gt)
        dact = accs[0]
        return dact * up * (sg * (1.0 + gt * (1.0 - sg))), dact * (gt * sg)

    dgate, dup = _matmul("mm_dact", [[(dh2_16, W["w_down"])]], "nt", [BF16, BF16], swiglu_bwd,
                         extras=[(gate, "mn"), (up, "mn")], tm=1024, tn=512, sub=256, deps=[t_down])
    t_gate = send_grad("w_gate", _matmul("mm_dw_gate", [[(f, dgate)]], "tn", [BF16], _plain, tn=1408, tk=2048)[0])
    t_up = send_grad("w_up", _matmul("mm_dw_up", [[(f, dup)]], "tn", [BF16], _plain, tn=1408, tk=2048,
                                     deps=[t_gate])[0])
    df = _matmul("mm_df", [[(dgate, W["w_gate"]), (dup, W["w_up"])]], "nt", [F32], _plain, tm=1024, tn=256,
                 tk=5632, deps=[t_up], vmem=VMEM_LIMIT_LARGE)[0]
    dh1, dh1_16, dg_ffn = _rms_bwd("rms_ffn_bwd", df, h1, g_ffn, dh2)
    t_out = send_grad("w_out", _matmul("mm_dw_out", [[(cat, dh1_16)]], "tn", [BF16], _plain, tk=2048)[0])
    dcat = _matmul("mm_dcat", [[(dh1_16, W["w_out"])]], "nt", [F32], _plain, deps=[t_out])[0]
    j1 = [reduce_grad(k, dcat) for k in ("w_ple", "w_pgate", "w_down")]
    dc, dg_ln, db_ln, db_dw = _conv_ln_bwd(dcat, conv_c, g_conv_ln, b_conv_ln, deps=j1)
    dcv, dcg, dw_dw = _conv_bwd(z, dc, w_dw_full, 3 * DA // DC)
    dq, dk, dv = _attn_bwd(z, dcat, o_attn, lse, H)
    dz = jnp.concatenate([dq, dk, dv, dcv, dcg], axis=1)
    t_in = send_grad("w_in", _matmul("mm_dw_in", [[(a, dz)]], "tn", [BF16], _plain, tk=2048)[0])
    j2 = [reduce_grad(k, dz) for k in ("w_gate", "w_up", "w_out")]
    da = _matmul("mm_da", [[(dz, W["w_in"])]], "nt", [F32], _plain, tm=512, tk=5120, deps=[t_in] + j2)[0]
    grad_x, _, dg_mix = _rms_bwd("rms_mix_bwd", da, x2, g_mix, dh1)

    wide = [dg_mix, dg_ffn, dg_ple, db_pgate, dg_final,
            jnp.concatenate([db_dw, dg_ln], axis=1), jnp.concatenate([db_ln, jnp.zeros_like(db_ln)], axis=1),
            jnp.pad(loss_part, ((0, 0), (0, D - 128))),
            dw_dw.reshape(HALO * DC // D, D)]
    small = jnp.concatenate(wide, axis=0)
    small = jnp.pad(small, ((0, -small.shape[0] % 8), (0, 0)))
    small_sum = _sum_slots("sum_small", _exchange_small(small))
    j3 = reduce_grad("w_in", small_sum)

    grads, deltas, new_m, new_v = {}, {}, {}, {}
    moments = dict(w_in=(m_w_in, v_w_in), w_out=(m_w_out, v_w_out), w_gate=(m_w_gate, v_w_gate),
                   w_up=(m_w_up, v_w_up), w_down=(m_w_down, v_w_down), w_pgate=(m_w_pgate, v_w_pgate),
                   w_ple=(m_w_ple, v_w_ple))
    last, deps = small_sum, [j3]
    for k in ("w_ple", "w_pgate", "w_down", "w_gate", "w_up", "w_out", "w_in"):
        g_k = _join_wait("join_wait_" + k, joined[k], last)
        d_, m_, v_ = _adamw("adamw_" + k, big[k][0], g_k, moments[k][0][0], moments[k][1][0], deps=deps)
        grads[k], deltas[k], new_m[k], new_v[k] = g_k[None], d_[None], m_[None], v_[None]
        last, deps = d_, ()

    half = lambda r, lo: small_sum[r:r + 1, lo * DC:(lo + 1) * DC]
    vec = dict(g_mix=small_sum[0:1], g_ffn=small_sum[1:2], g_ple=small_sum[2:3], b_pgate=small_sum[3:4],
               g_final=small_sum[4:5], b_dw=half(5, 0), g_conv_ln=half(5, 1), b_conv_ln=half(6, 0))
    loss = small_sum[7, 0]
    dw_dw_sum = small_sum[8:8 + HALO * DC // D].reshape(HALO, DC)
    s_me = 2 * lax.axis_index("x") + lax.axis_index("y")
    cs = w_dw.shape[3]
    vec["w_dw"] = lax.dynamic_slice(dw_dw_sum, (0, s_me * cs), (CONV_W, cs))
    small_w = dict(g_mix=(g_mix, m_g_mix, v_g_mix), g_ffn=(g_ffn, m_g_ffn, v_g_ffn), g_ple=(g_ple, m_g_ple, v_g_ple),
                   b_pgate=(b_pgate, m_b_pgate, v_b_pgate), g_final=(g_final, m_g_final, v_g_final),
                   b_dw=(b_dw, m_b_dw, v_b_dw), g_conv_ln=(g_conv_ln, m_g_conv_ln, v_g_conv_ln),
                   b_conv_ln=(b_conv_ln, m_b_conv_ln, v_b_conv_ln), w_dw=(w_dw, m_w_dw, v_w_dw))
    for k, (w_, m_, v_) in small_w.items():
        shape = w_.shape
        g2 = vec[k]
        to2 = lambda t: t.reshape(g2.shape)
        d_, nm_, nv_ = _adamw("adamw_" + k, to2(w_), g2, to2(m_), to2(v_))
        grads[k], deltas[k], new_m[k], new_v[k] = (t.reshape(shape) for t in (g2, d_, nm_, nv_))

    order = ["g_mix", "w_in", "w_dw", "b_dw", "g_conv_ln", "b_conv_ln", "w_out", "g_ffn", "w_gate", "w_up", "w_down",
             "g_ple", "w_pgate", "b_pgate", "w_ple", "g_final"]
    return (loss, grad_x.reshape(x.shape), *[grads[k] for k in order], *[deltas[k] for k in order],
            *[new_m[k] for k in order], *[new_v[k] for k in order])
```

```python
import functools

import jax
import jax.numpy as jnp
from jax import lax
from jax.experimental import pallas as pl
from jax.experimental.pallas import tpu as pltpu

F32 = jnp.float32
BF16 = jnp.bfloat16

EPS = 1e-6
HEAD_DIM = 128
BLK = 128
DILATIONS = (1, 4, 16)
SUPER = BLK * DILATIONS[-1]
CONV_W = 31
HALO = 32
ADAM_LR, ADAM_B1, ADAM_B2, ADAM_EPS, ADAM_WD, ADAM_STEP = 0.001, 0.9, 0.999, 1e-08, 0.01, 10

V7X_VMEM_BYTES = 64 * 1024 * 1024
VMEM_LIMIT = V7X_VMEM_BYTES * 3 // 4
VMEM_LIMIT_LARGE = V7X_VMEM_BYTES * 15 // 16
MESH = pl.DeviceIdType.MESH
ANY = pl.BlockSpec(memory_space=pl.ANY)
HBM = pl.BlockSpec(memory_space=pltpu.HBM)
SEM = pl.BlockSpec(memory_space=pltpu.SEMAPHORE)
EFFECT = pltpu.SideEffectType.DATAFLOW_SIDE_EFFECTING
TOKEN = (8, 128)


def _params(semantics=None, vmem=VMEM_LIMIT, **kw):
    return pltpu.CompilerParams(dimension_semantics=semantics, vmem_limit_bytes=vmem, **kw)


def _tile(n, want, mult=128):
    if n <= want:
        return n
    for t in range(want - want % mult, 0, -mult):
        if n % t == 0:
            return t
    raise ValueError((n, want, mult))


_DIMS = {"nn": ((1,), (0,)), "nt": ((1,), (1,)), "tn": ((0,), (0,))}


def _matmul(name, groups, mode, out_dtypes, epilogue, extras=(), tm=1024, tn=1024, tk=2048, sub=None, deps=(),
            vmem=VMEM_LIMIT):
    a0, b0 = groups[0][0][:2]
    if mode == "nn":
        (M, K), N = a0.shape, b0.shape[1]
    elif mode == "nt":
        (M, K), N = a0.shape, b0.shape[0]
    else:
        (K, M), N = a0.shape, b0.shape[1]
    tm, tn, tk = _tile(M, tm), _tile(N, tn), _tile(K, tk)
    nk = K // tk
    if mode == "tn":
        a_spec = pl.BlockSpec((tk, tm), lambda i, j, k: (k, i))
    else:
        a_spec = pl.BlockSpec((tm, tk), lambda i, j, k: (i, k))
    operands, in_specs = [], []
    for grp in groups:
        for a, b, *k0 in grp:
            k0 = k0[0] if k0 else 0
            if mode == "nt":
                b_spec = pl.BlockSpec((tn, tk), functools.partial(lambda i, j, k, k0: (j, k + k0), k0=k0))
            else:
                b_spec = pl.BlockSpec((tk, tn), functools.partial(lambda i, j, k, k0: (k + k0, j), k0=k0))
            operands += [a, b]
            in_specs += [a_spec, b_spec]
    for arr, kind in extras:
        operands.append(arr)
        if kind == "mn":
            in_specs.append(pl.BlockSpec((tm, tn), lambda i, j, k: (i, j)))
        else:
            in_specs.append(pl.BlockSpec((1, tn), lambda i, j, k: (0, j)))
    for tok in deps:
        operands.append(tok)
        in_specs.append(pl.BlockSpec(TOKEN, lambda i, j, k: (0, 0)))
    n_pairs = [len(g) for g in groups]
    n_ex, n_out, n_grp, n_dep = len(extras), len(out_dtypes), len(groups), len(deps)
    out_shape = [jax.ShapeDtypeStruct((M, N), dt) for dt in out_dtypes]
    out_specs = [pl.BlockSpec((tm, tn), lambda i, j, k: (i, j)) for _ in out_dtypes]

    kinds = [kind for _, kind in extras]
    sub = tm if (sub is None or nk > 1 or mode == "tn") else sub
    assert tm % sub == 0

    def body(*refs):
        ex_refs = refs[2 * sum(n_pairs):2 * sum(n_pairs) + n_ex]
        pos = 2 * sum(n_pairs) + n_ex + n_dep
        out_refs = refs[pos:pos + n_out]
        acc_refs = refs[pos + n_out:]

        def products(rows):
            pos, parts = 0, []
            for g in range(n_grp):
                part = None
                for _ in range(n_pairs[g]):
                    a_ref, b_ref = refs[pos], refs[pos + 1]
                    pos += 2
                    d = lax.dot_general(a_ref[rows].astype(BF16), b_ref[...].astype(BF16),
                                        (_DIMS[mode], ((), ())), preferred_element_type=F32)
                    part = d if part is None else part + d
                parts.append(part)
            return parts

        def finish(accs, rows):
            outs = epilogue(accs, [e[rows] if kind == "mn" else e[...] for e, kind in zip(ex_refs, kinds)])
            for o_ref, o in zip(out_refs, outs):
                o_ref[rows] = o.astype(o_ref.dtype)

        if nk == 1:
            for r in range(tm // sub):
                rows = (pl.ds(r * sub, sub), slice(None)) if sub < tm else (slice(None), slice(None))
                finish(products(rows), rows)
        else:
            k = pl.program_id(2)
            parts = products((slice(None), slice(None)))

            @pl.when(k == 0)
            def _():
                for acc, part in zip(acc_refs, parts):
                    acc[...] = part

            @pl.when(k > 0)
            def _():
                for acc, part in zip(acc_refs, parts):
                    acc[...] += part

            @pl.when(k == nk - 1)
            def _():
                finish([acc[...] for acc in acc_refs], (slice(None), slice(None)))

    scratch = [pltpu.VMEM((tm, tn), F32) for _ in range(n_grp)] if nk > 1 else []
    return pl.pallas_call(
        body, name=name, grid=(M // tm, N // tn, nk),
        in_specs=in_specs, out_specs=out_specs, out_shape=out_shape, scratch_shapes=scratch,
        compiler_params=_params(("parallel", "parallel", "arbitrary"), vmem),
    )(*operands)


def _plain(accs, extras):
    return (accs[0],)


def _add_residual(accs, extras):
    return (accs[0] + extras[0],)


def _rowwise(name, fn, rows, consts, row_outs, acc_outs, tb, deps=()):
    T = rows[0][0].shape[0]
    tb = _tile(T, tb, 16)
    operands = [r[0] for r in rows] + list(consts) + list(deps)
    in_specs = [pl.BlockSpec((tb, c), functools.partial(lambda i, cb: (i, cb), cb=cb)) for _, c, cb in rows]
    in_specs += [pl.BlockSpec(c.shape, functools.partial(lambda i, nd: (0,) * nd, nd=c.ndim)) for c in consts]
    in_specs += [pl.BlockSpec(TOKEN, lambda i: (0, 0)) for _ in deps]
    out_shape = [jax.ShapeDtypeStruct(s, dt) for s, dt, _, _ in row_outs]
    out_specs = [pl.BlockSpec((tb, c), functools.partial(lambda i, cb: (i, cb), cb=cb)) for _, _, c, cb in row_outs]
    out_shape += [jax.ShapeDtypeStruct(s, F32) for s in acc_outs]
    out_specs += [pl.BlockSpec(s, functools.partial(lambda i, nd: (0,) * nd, nd=len(s))) for s in acc_outs]
    n_rows, n_consts, n_ro, n_dep = len(rows), len(consts), len(row_outs), len(deps)

    def body(*refs):
        row_refs = refs[:n_rows]
        const_refs = refs[n_rows:n_rows + n_consts]
        out_refs = refs[n_rows + n_consts + n_dep:]
        ro, ao = fn([r[...] for r in row_refs], [c[...] for c in const_refs])
        for o_ref, o in zip(out_refs[:n_ro], ro):
            o_ref[...] = o.astype(o_ref.dtype)
        if acc_outs:
            i = pl.program_id(0)

            @pl.when(i == 0)
            def _():
                for a_ref, a in zip(out_refs[n_ro:], ao):
                    a_ref[...] = a

            @pl.when(i > 0)
            def _():
                for a_ref, a in zip(out_refs[n_ro:], ao):
                    a_ref[...] += a

    return pl.pallas_call(
        body, name=name, grid=(T // tb,), in_specs=in_specs, out_specs=out_specs, out_shape=out_shape,
        compiler_params=_params(("arbitrary",) if acc_outs else ("parallel",)),
    )(*operands)


def _colsum(v):
    return jnp.sum(v, axis=0, keepdims=True)


def _rms_fwd(name, x, g, tb=512, deps=()):
    T, D = x.shape

    def fn(rows, consts):
        xv, gv = rows[0], consts[0]
        r = lax.rsqrt(jnp.mean(xv * xv, axis=-1, keepdims=True) + EPS)
        return ((xv * r) * gv,), ()

    return _rowwise(name, fn, [(x, D, 0)], [g], [((T, D), BF16, D, 0)], [], tb, deps=deps)[0]


def _rms_bwd(name, dy, x, g, resid, tb=256, weight=None):
    T, D = x.shape
    consts = [g] if weight is None else [g, weight]

    def fn(rows, consts):
        dyv, xv, rv = rows
        gv = consts[0]
        if weight is not None:
            dyv = lax.dot_general(dyv, consts[1], (((1,), (1,)), ((), ())), preferred_element_type=F32)
        r = lax.rsqrt(jnp.mean(xv * xv, axis=-1, keepdims=True) + EPS)
        n = xv * r
        dn = dyv * gv
        dx = r * (dn - n * jnp.mean(dn * n, axis=-1, keepdims=True))
        tot = rv + dx
        return (tot, tot), (_colsum(dyv * n),)

    return _rowwise(name, fn, [(dy, dy.shape[1], 0), (x, D, 0), (resid, D, 0)], consts,
                    [((T, D), F32, D, 0), ((T, D), BF16, D, 0)], [(1, D)], tb)


NBLK = SUPER // BLK


def _classes(ref, dil, rows, start=0, dtype=None):
    parts = [ref[pl.ds(start + r, rows, stride=dil), :] if dil > 1 else ref[pl.ds(start, rows), :]
             for r in range(dil)]
    if dtype is not None:
        parts = [p.astype(dtype) for p in parts]
    return parts


def _keys_with_prev(ref, prev_ref, dil):
    L = SUPER // dil
    own = _classes(ref, dil, L, dtype=BF16)
    last = _classes(prev_ref, dil, BLK, start=SUPER - BLK * dil, dtype=BF16)
    blocks = []
    for r in range(dil):
        ext = jnp.concatenate([last[r], own[r]], axis=0)
        blocks += [ext[j * BLK:(j + 2) * BLK] for j in range(L // BLK)]
    return jnp.stack(blocks, axis=0)


def _band_mask(dil, has_prev):
    qi = lax.broadcasted_iota(jnp.int32, (BLK, 2 * BLK), 0)
    kj = lax.broadcasted_iota(jnp.int32, (BLK, 2 * BLK), 1)
    own = jnp.logical_and(kj >= BLK, kj - BLK <= qi)
    prev = jnp.logical_and(kj < BLK, kj >= qi)
    b = lax.broadcasted_iota(jnp.int32, (NBLK, 1, 1), 0)
    first = (b & (SUPER // (BLK * dil) - 1)) == 0
    prev_ok = jnp.logical_or(jnp.logical_not(first), has_prev)
    return jnp.logical_or(own[None], jnp.logical_and(prev[None], prev_ok))


def _bdot(a, b, ca, cb):
    return lax.dot_general(a, b, (((ca,), (cb,)), ((0,), (0,))), preferred_element_type=F32)


def _put_classes(dst, value, dil, rows, start=0, add=False, src_start=0, src_stride=None):
    src_stride = rows if src_stride is None else src_stride
    for r in range(dil):
        idx = (pl.ds(start + r, rows, stride=dil) if dil > 1 else pl.ds(start, rows), slice(None))
        part = value[src_start + r * src_stride:src_start + r * src_stride + rows]
        dst[idx] = dst[idx] + part if add else part


def _attn_fwd(z, cat_width, n_heads):
    T = z.shape[0]
    H = n_heads
    DA = H * HEAD_DIM
    nb = T // SUPER
    scale = HEAD_DIM ** -0.5

    def body(q_ref, k_ref, v_ref, kp_ref, vp_ref, cat_ref, o_ref, lse_ref, ob, lb):
        has_prev = pl.program_id(1) > 0
        for b, dil in enumerate(DILATIONS):
            L = SUPER // dil
            q3 = jnp.concatenate(_classes(q_ref, dil, L, dtype=BF16), axis=0).reshape(NBLK, BLK, HEAD_DIM)
            k3 = _keys_with_prev(k_ref, kp_ref, dil)
            v3 = _keys_with_prev(v_ref, vp_ref, dil)
            s = jnp.where(_band_mask(dil, has_prev), _bdot(q3, k3, 2, 2) * scale, -jnp.inf)
            m = jnp.max(s, axis=-1, keepdims=True)
            e = jnp.exp(s - m)
            den = jnp.sum(e, axis=-1, keepdims=True)
            o3 = _bdot((e / den).astype(BF16), v3, 2, 1)
            lse3 = jnp.broadcast_to(m + jnp.log(den), (NBLK, BLK, HEAD_DIM))
            _put_classes(ob.at[b], o3.reshape(SUPER, HEAD_DIM), dil, L)
            _put_classes(lb.at[b], lse3.reshape(SUPER, HEAD_DIM), dil, L)
        l0, l1, l2 = lb[0], lb[1], lb[2]
        mx = jnp.maximum(jnp.maximum(l0, l1), l2)
        tot = mx + jnp.log(jnp.exp(l0 - mx) + jnp.exp(l1 - mx) + jnp.exp(l2 - mx))
        o = jnp.exp(l0 - tot) * ob[0] + jnp.exp(l1 - tot) * ob[1] + jnp.exp(l2 - tot) * ob[2]
        o_ref[...] = o
        cat_ref[...] = o.astype(BF16)
        lse_ref[...] = tot

    blk = (SUPER, HEAD_DIM)
    in_specs = [
        pl.BlockSpec(blk, lambda h, n: (n, h)),
        pl.BlockSpec(blk, lambda h, n: (n, H + h)),
        pl.BlockSpec(blk, lambda h, n: (n, 2 * H + h)),
        pl.BlockSpec(blk, lambda h, n: (jnp.maximum(n - 1, 0), H + h)),
        pl.BlockSpec(blk, lambda h, n: (jnp.maximum(n - 1, 0), 2 * H + h)),
    ]
    out_spec = pl.BlockSpec(blk, lambda h, n: (n, h))
    return pl.pallas_call(
        body, name="attn_fwd", grid=(H, nb), in_specs=in_specs, out_specs=[out_spec] * 3,
        out_shape=[jax.ShapeDtypeStruct((T, cat_width), BF16), jax.ShapeDtypeStruct((T, DA), F32),
                   jax.ShapeDtypeStruct((T, DA), F32)],
        scratch_shapes=[pltpu.VMEM((3, SUPER, HEAD_DIM), F32), pltpu.VMEM((3, SUPER, HEAD_DIM), F32)],
        compiler_params=_params(("parallel", "parallel")),
    )(z, z, z, z, z)


def _attn_bwd(z, dcat, o, lse, n_heads):
    T = z.shape[0]
    H = n_heads
    nb = T // SUPER
    scale = HEAD_DIM ** -0.5

    def body(q_ref, k_ref, v_ref, kp_ref, vp_ref, do_ref, o_ref, lse_ref, dz_q, dz_k, dz_v,
             dq_acc, dk_acc, dv_acc, dkp_acc, dvp_acc, dsum):
        i = pl.program_id(1)
        has_prev = i < nb - 1

        @pl.when(i == 0)
        def _():
            dk_acc[...] = jnp.zeros_like(dk_acc)
            dv_acc[...] = jnp.zeros_like(dv_acc)

        @pl.when(i > 0)
        def _():
            dk_acc[...] = dkp_acc[...]
            dv_acc[...] = dvp_acc[...]

        dq_acc[...] = jnp.zeros_like(dq_acc)
        dkp_acc[...] = jnp.zeros_like(dkp_acc)
        dvp_acc[...] = jnp.zeros_like(dvp_acc)
        dsum[...] = jnp.broadcast_to(jnp.sum(do_ref[...] * o_ref[...], axis=-1, keepdims=True), (SUPER, HEAD_DIM))
        for dil in DILATIONS:
            L = SUPER // dil
            wide = lambda ref: jnp.tile(jnp.concatenate(_classes(ref, dil, L), axis=0).reshape(NBLK, BLK, HEAD_DIM),
                                        (1, 1, 2))
            q3 = jnp.concatenate(_classes(q_ref, dil, L, dtype=BF16), axis=0).reshape(NBLK, BLK, HEAD_DIM)
            do3 = jnp.concatenate(_classes(do_ref, dil, L, dtype=BF16), axis=0).reshape(NBLK, BLK, HEAD_DIM)
            k3 = _keys_with_prev(k_ref, kp_ref, dil)
            v3 = _keys_with_prev(v_ref, vp_ref, dil)
            p = jnp.where(_band_mask(dil, has_prev), jnp.exp(_bdot(q3, k3, 2, 2) * scale - wide(lse_ref)), 0.0)
            ds = (p * (_bdot(do3, v3, 2, 2) - wide(dsum)) * scale).astype(BF16)
            dq = _bdot(ds, k3, 2, 1).reshape(SUPER, HEAD_DIM)
            dk = _bdot(ds, q3, 1, 1)
            dv = _bdot(p.astype(BF16), do3, 1, 1)
            _put_classes(dq_acc, dq, dil, L, add=True)
            for acc, prev_acc, g in ((dk_acc, dkp_acc, dk), (dv_acc, dvp_acc, dv)):
                _put_classes(acc, g[:, BLK:, :].reshape(SUPER, HEAD_DIM), dil, L, add=True)
                to_prev = g[:, :BLK, :].reshape(SUPER, HEAD_DIM)
                if L > BLK:
                    _put_classes(acc, to_prev, dil, L - BLK, add=True, src_start=BLK, src_stride=L)
                _put_classes(prev_acc, to_prev, dil, BLK, start=SUPER - BLK * dil, add=True, src_stride=L)
        dz_q[...] = dq_acc[...].astype(BF16)
        dz_k[...] = dk_acc[...].astype(BF16)
        dz_v[...] = dv_acc[...].astype(BF16)

    blk = (SUPER, HEAD_DIM)
    row = lambda i: nb - 1 - i
    in_specs = [
        pl.BlockSpec(blk, lambda h, i: (row(i), h)),
        pl.BlockSpec(blk, lambda h, i: (row(i), H + h)),
        pl.BlockSpec(blk, lambda h, i: (row(i), 2 * H + h)),
        pl.BlockSpec(blk, lambda h, i: (jnp.maximum(row(i) - 1, 0), H + h)),
        pl.BlockSpec(blk, lambda h, i: (jnp.maximum(row(i) - 1, 0), 2 * H + h)),
        pl.BlockSpec(blk, lambda h, i: (row(i), h)),
        pl.BlockSpec(blk, lambda h, i: (row(i), h)),
        pl.BlockSpec(blk, lambda h, i: (row(i), h)),
    ]
    out_spec = pl.BlockSpec(blk, lambda h, i: (row(i), h))
    return pl.pallas_call(
        body, name="attn_bwd", grid=(H, nb), in_specs=in_specs, out_specs=[out_spec] * 3,
        out_shape=[jax.ShapeDtypeStruct((T, H * HEAD_DIM), BF16)] * 3,
        scratch_shapes=[pltpu.VMEM(blk, F32) for _ in range(6)],
        compiler_params=_params(("parallel", "arbitrary")),
    )(z, z, z, z, z, dcat, o, lse)


def _glu(cv, cg):
    return cv * jax.nn.sigmoid(cg)


SUBLANES = 8


def _fill_shifted(sh, ext, rows):
    for b in range(1, SUBLANES):
        sh[b - 1, pl.ds(0, rows - SUBLANES), :] = ext[pl.ds(b, rows - SUBLANES), :]


def _window(sh, ext, offset, rows):
    b = offset % SUBLANES
    if b == 0:
        return ext[pl.ds(offset, rows), :]
    return sh[b - 1, pl.ds(offset - b, rows), :]


def _conv_fwd(z, cat, w_dw, b_dw, g_ln, b_ln, col0, tb=512):
    T = z.shape[0]
    DC = w_dw.shape[1]
    tb = _tile(T, tb, HALO)
    hb = tb // HALO
    cat_cb = cat.shape[1] // DC - 1

    def body(cv_ref, cg_ref, cvh_ref, cgh_ref, w_ref, bdw_ref, g_ref, b_ref, cat_in, cat_ref, c_ref, u_ext, sh):
        i = pl.program_id(0)
        halo = _glu(cvh_ref[...], cgh_ref[...])
        u_ext[pl.ds(0, HALO), :] = jnp.where(i > 0, halo, 0.0)
        u_ext[pl.ds(HALO, tb), :] = _glu(cv_ref[...], cg_ref[...])
        _fill_shifted(sh, u_ext, tb + HALO)
        acc = jnp.broadcast_to(bdw_ref[...], (tb, DC))
        for j in range(CONV_W):
            acc = acc + w_ref[pl.ds(j, 1), :] * _window(sh, u_ext, HALO - (CONV_W - 1) + j, tb)
        c_ref[...] = acc
        mu = jnp.mean(acc, axis=-1, keepdims=True)
        var = jnp.mean(jnp.square(acc - mu), axis=-1, keepdims=True)
        y = (acc - mu) * lax.rsqrt(var + EPS) * g_ref[...] + b_ref[...]
        cat_ref[...] = (y * jax.nn.sigmoid(y)).astype(BF16)

    cur = lambda cb: pl.BlockSpec((tb, DC), lambda i: (i, cb))
    halo = lambda cb: pl.BlockSpec((HALO, DC), lambda i: (jnp.maximum(i * hb - 1, 0), cb))
    whole = lambda a: pl.BlockSpec(a.shape, lambda i: (0, 0))
    return pl.pallas_call(
        body, name="conv_fwd", grid=(T // tb,),
        in_specs=[cur(col0), cur(col0 + 1), halo(col0), halo(col0 + 1), whole(w_dw), whole(b_dw), whole(g_ln),
                  whole(b_ln), ANY],
        out_specs=[pl.BlockSpec((tb, DC), lambda i: (i, cat_cb)), pl.BlockSpec((tb, DC), lambda i: (i, 0))],
        out_shape=[jax.ShapeDtypeStruct(cat.shape, cat.dtype), jax.ShapeDtypeStruct((T, DC), F32)],
        scratch_shapes=[pltpu.VMEM((tb + HALO, DC), F32), pltpu.VMEM((SUBLANES - 1, tb + HALO, DC), F32)],
        input_output_aliases={8: 0},
        compiler_params=_params(("parallel",)),
    )(z, z, z, z, w_dw, b_dw, g_ln, b_ln, cat)


def _conv_ln_bwd(dcat, c, g_ln, b_ln, tb=256, deps=()):
    T, DC = c.shape
    d_cb = dcat.shape[1] // DC - 1

    def fn(rows, consts):
        dov, cv_ = rows
        gv, bv = consts
        mu = jnp.mean(cv_, axis=-1, keepdims=True)
        xc = cv_ - mu
        rstd = lax.rsqrt(jnp.mean(jnp.square(xc), axis=-1, keepdims=True) + EPS)
        ln = xc * rstd
        y = ln * gv + bv
        sg = jax.nn.sigmoid(y)
        dy = dov * (sg * (1.0 + y * (1.0 - sg)))
        dln = dy * gv
        dc = rstd * (dln - jnp.mean(dln, axis=-1, keepdims=True) - ln * jnp.mean(dln * ln, axis=-1, keepdims=True))
        return (dc,), (_colsum(dy * ln), _colsum(dy), _colsum(dc))

    return _rowwise("conv_ln_bwd", fn, [(dcat, DC, d_cb), (c, DC, 0)], [g_ln, b_ln],
                    [((T, DC), F32, DC, 0)], [(1, DC)] * 3, tb, deps=deps)


def _conv_bwd(z, dc, w_dw, col0, tb=512):
    T, DC = dc.shape
    tb = _tile(T, tb, HALO)
    hb = tb // HALO
    nblk = T // tb

    def body(cv_ref, cg_ref, dc_ref, dcn_ref, w_ref, dcv_ref, dcg_ref, dw_ref, dc_ext, sh):
        i = pl.program_id(0)
        cv, cg = cv_ref[...], cg_ref[...]
        sg = jax.nn.sigmoid(cg)
        u = cv * sg
        dc_ext[pl.ds(0, tb), :] = dc_ref[...]
        dc_ext[pl.ds(tb, HALO), :] = jnp.where(i < nblk - 1, dcn_ref[...], 0.0)
        _fill_shifted(sh, dc_ext, tb + HALO)

        @pl.when(i == 0)
        def _():
            dw_ref[...] = jnp.zeros_like(dw_ref)

        du = jnp.zeros((tb, DC), F32)
        for j in range(CONV_W):
            d_j = _window(sh, dc_ext, CONV_W - 1 - j, tb)
            du = du + w_ref[pl.ds(j, 1), :] * d_j
            dw_ref[pl.ds(j, 1), :] += _colsum(u * d_j)
        dcv_ref[...] = (du * sg).astype(BF16)
        dcg_ref[...] = (du * cv * sg * (1.0 - sg)).astype(BF16)

    cur = lambda cb: pl.BlockSpec((tb, DC), lambda i: (i, cb))
    nxt = pl.BlockSpec((HALO, DC), lambda i: (jnp.minimum((i + 1) * hb, T // HALO - 1), 0))
    return pl.pallas_call(
        body, name="conv_bwd", grid=(nblk,),
        in_specs=[cur(col0), cur(col0 + 1), cur(0), nxt, pl.BlockSpec(w_dw.shape, lambda i: (0, 0))],
        out_specs=[cur(0), cur(0), pl.BlockSpec((HALO, DC), lambda i: (0, 0))],
        out_shape=[jax.ShapeDtypeStruct((T, DC), BF16), jax.ShapeDtypeStruct((T, DC), BF16),
                   jax.ShapeDtypeStruct((HALO, DC), F32)],
        scratch_shapes=[pltpu.VMEM((tb + HALO, DC), F32), pltpu.VMEM((SUBLANES - 1, tb + HALO, DC), F32)],
        compiler_params=_params(("arbitrary",)),
    )(z, z, dc, dc, w_dw)


def _ple_loss_head(n2, w_pgate, b_pgate, h2, p, w_ple, target, g_final, tb=256):
    T, D = h2.shape

    def fn(rows, consts):
        n2v, h2v, pv, tv = rows
        wp, bp, wple, gv = consts
        gt = jax.nn.sigmoid(lax.dot_general(n2v, wp, (((1,), (0,)), ((), ())), preferred_element_type=F32) + bp)
        ev = lax.dot_general(pv.astype(BF16), wple, (((1,), (0,)), ((), ())), preferred_element_type=F32)
        hv = h2v + ev * gt
        r = lax.rsqrt(jnp.mean(hv * hv, axis=-1, keepdims=True) + EPS)
        n = hv * r
        diff = n * gv - tv
        loss = 0.5 * jnp.sum(jnp.mean(jnp.square(diff), axis=-1, keepdims=True), axis=0, keepdims=True)
        dy = diff * (1.0 / D)
        dn = dy * gv
        dh = r * (dn - n * jnp.mean(dn * n, axis=-1, keepdims=True))
        du4 = dh * ev * gt * (1.0 - gt)
        return (dh, dh * gt, du4), (_colsum(dy * n), _colsum(du4), jnp.broadcast_to(loss, (1, 128)))

    return _rowwise("ple_loss_head", fn, [(n2, D, 0), (h2, D, 0), (p, p.shape[1], 0), (target, D, 0)],
                    [w_pgate, b_pgate, w_ple, g_final],
                    [((T, D), F32, D, 0), ((T, D), BF16, D, 0), ((T, D), BF16, D, 0)],
                    [(1, D), (1, D), (1, 128)], tb)


def _me():
    return lax.axis_index("x"), lax.axis_index("y"), lax.axis_index("c")


def _chips3(x, y):
    return [(1 - x, y), (x, 1 - y), (1 - x, 1 - y)]


def _peers7(x, y, c):
    for m in range(1, 8):
        yield m - 1, (x ^ (m >> 2), y ^ ((m >> 1) & 1), c ^ (m & 1))


def _shard_of(ref, axis, s):
    R, C = ref.shape
    if axis == 1:
        return ref.at[:, pl.ds(s * (C // 4), C // 4)]
    return ref.at[pl.ds(s * (R // 4), R // 4), :]


def _region(ref, axis, shard, half):
    R, C = ref.shape
    if axis == 1:
        cs, hr = C // 4, R // 2
        return ref.at[pl.ds(half * hr, hr), pl.ds(shard * cs, cs)]
    hr = R // 8
    return ref.at[pl.ds(shard * 2 * hr + half * hr, hr), :]


def _gather_now(shard, small):
    R, C = shard.shape
    cs = small.shape[1]

    def body(shard_ref, small_ref, out_ref, small_out, send, recv, fsend, frecv, lsem):
        x, y, c = _me()
        me_s = 2 * x + y
        sibling = (x, y, 1 - c)
        chips = _chips3(x, y)
        half = shard_ref.at[pl.ds(c * (R // 2), R // 2), :]
        locals_ = [pltpu.make_async_copy(shard_ref, _shard_of(out_ref, 1, me_s), lsem.at[0]),
                   pltpu.make_async_copy(small_ref, _shard_of(small_out, 1, me_s), lsem.at[1])]
        for cp in locals_:
            cp.start()
        firsts = []
        for j, (px, py) in enumerate(chips):
            firsts.append(pltpu.make_async_remote_copy(
                src_ref=half, dst_ref=_region(out_ref, 1, me_s, c), send_sem=send.at[0, j], recv_sem=recv.at[0, j],
                device_id=(px, py, c), device_id_type=MESH))
            firsts.append(pltpu.make_async_remote_copy(
                src_ref=small_ref, dst_ref=_shard_of(small_out, 1, me_s), send_sem=send.at[1, j],
                recv_sem=recv.at[1, j], device_id=(px, py, c), device_id_type=MESH))
        for cp in firsts:
            cp.start()
        relays = []
        for j, (px, py) in enumerate(chips):
            landed = _region(out_ref, 1, 2 * px + py, c)
            pltpu.make_async_remote_copy(
                src_ref=half, dst_ref=landed, send_sem=send.at[0, j], recv_sem=recv.at[0, j],
                device_id=(px, py, c), device_id_type=MESH).wait_recv()
            relay = pltpu.make_async_remote_copy(
                src_ref=landed, dst_ref=landed, send_sem=fsend.at[j], recv_sem=frecv.at[j],
                device_id=sibling, device_id_type=MESH)
            relay.start()
            relays.append(relay)
        for j, (px, py) in enumerate(chips):
            pltpu.make_async_remote_copy(
                src_ref=small_ref, dst_ref=_shard_of(small_out, 1, 2 * px + py), send_sem=send.at[1, j],
                recv_sem=recv.at[1, j], device_id=(px, py, c), device_id_type=MESH).wait_recv()
            theirs = _region(out_ref, 1, 2 * px + py, 1 - c)
            pltpu.make_async_remote_copy(
                src_ref=theirs, dst_ref=theirs, send_sem=fsend.at[j], recv_sem=frecv.at[j],
                device_id=sibling, device_id_type=MESH).wait_recv()
        for cp in firsts + relays:
            cp.wait_send()
        for cp in locals_:
            cp.wait()

    return pl.pallas_call(
        body, name="gather_now", in_specs=[ANY, ANY], out_specs=[ANY, ANY],
        out_shape=[jax.ShapeDtypeStruct((R, 4 * C), shard.dtype),
                   jax.ShapeDtypeStruct((small.shape[0], 4 * cs), small.dtype)],
        scratch_shapes=[pltpu.SemaphoreType.DMA((2, 3)), pltpu.SemaphoreType.DMA((2, 3)),
                        pltpu.SemaphoreType.DMA((3,)), pltpu.SemaphoreType.DMA((3,)), pltpu.SemaphoreType.DMA((2,))],
        compiler_params=_params(),
    )(shard, small)


def _exchange_small(small):
    def body(small_ref, out_ref, send, recv, lsem):
        x, y, c = _me()
        me = 4 * x + 2 * y + c
        own = pltpu.make_async_copy(small_ref, out_ref.at[me], lsem)
        own.start()
        sends = [pltpu.make_async_remote_copy(
            src_ref=small_ref, dst_ref=out_ref.at[me], send_sem=send.at[m], recv_sem=recv.at[m],
            device_id=peer, device_id_type=MESH) for m, peer in _peers7(x, y, c)]
        for cp in sends:
            cp.start()
        for m, (px, py, pc) in _peers7(x, y, c):
            pltpu.make_async_remote_copy(
                src_ref=small_ref, dst_ref=out_ref.at[4 * px + 2 * py + pc], send_sem=send.at[m], recv_sem=recv.at[m],
                device_id=(px, py, pc), device_id_type=MESH).wait_recv()
        for cp in sends:
            cp.wait_send()
        own.wait()

    return pl.pallas_call(
        body, name="exchange_small", in_specs=[ANY], out_specs=ANY,
        out_shape=jax.ShapeDtypeStruct((8,) + small.shape, small.dtype),
        scratch_shapes=[pltpu.SemaphoreType.DMA((7,)), pltpu.SemaphoreType.DMA((7,)), pltpu.SemaphoreType.DMA(())],
        compiler_params=_params(),
    )(small)


def _sum_slots(name, slots, tr=256):
    S, R, C = slots.shape
    tr = _tile(R, tr, 16)

    def body(s_ref, o_ref):
        acc = s_ref[0].astype(F32)
        for s in range(1, S):
            acc = acc + s_ref[s].astype(F32)
        o_ref[...] = acc

    return pl.pallas_call(
        body, name=name, grid=(R // tr,), in_specs=[pl.BlockSpec((S, tr, C), lambda i: (0, i, 0))],
        out_specs=pl.BlockSpec((tr, C), lambda i: (i, 0)), out_shape=jax.ShapeDtypeStruct((R, C), F32),
        compiler_params=_params(("parallel",)),
    )(slots)


def _place_shard(name, shard, axis):
    R, C = shard.shape
    full_shape = (R, 4 * C) if axis == 1 else (4 * R, C)

    def body(shard_ref, full_ref, sem):
        x, y, c = _me()
        cp = pltpu.make_async_copy(shard_ref, _shard_of(full_ref, axis, 2 * x + y), sem)
        cp.start()
        cp.wait()

    return pl.pallas_call(
        body, name=name, in_specs=[ANY], out_specs=ANY, out_shape=jax.ShapeDtypeStruct(full_shape, shard.dtype),
        scratch_shapes=[pltpu.SemaphoreType.DMA(())], compiler_params=_params(),
    )(shard)


def _gather_start(name, shard, axis, after):
    full = _place_shard(name + "_place", shard, axis)
    full_shape = full.shape

    def body(shard_ref, full_ref, after_ref, send, recv, shard_thru, full_thru, token):
        x, y, c = _me()
        for j, (px, py) in enumerate(_chips3(x, y)):
            pltpu.make_async_remote_copy(
                src_ref=shard_ref, dst_ref=_shard_of(full_ref, axis, 2 * x + y), send_sem=send.at[j],
                recv_sem=recv.at[j], device_id=(px, py, c), device_id_type=MESH).start()
        token[...] = jnp.zeros_like(token)

    return pl.pallas_call(
        body, name=name,
        out_shape=(pltpu.SemaphoreType.DMA((3,)), pltpu.SemaphoreType.DMA((3,)), pltpu.HBM(shard.shape, shard.dtype),
                   pltpu.HBM(full_shape, shard.dtype), jax.ShapeDtypeStruct(TOKEN, F32)),
        in_specs=(HBM, HBM, ANY), out_specs=(SEM, SEM, HBM, HBM, pl.BlockSpec(memory_space=pltpu.VMEM)),
        input_output_aliases={0: 2, 1: 3}, compiler_params=pltpu.CompilerParams(has_side_effects=EFFECT),
    )(pltpu.with_memory_space_constraint(shard, pltpu.HBM), pltpu.with_memory_space_constraint(full, pltpu.HBM), after)


def _gather_wait(name, started, axis, after):
    send, recv, shard_thru, full_thru, _ = started

    def body(shard_ref, full_ref, send, recv, after_ref, shard_dead, full_out):
        x, y, c = _me()
        for j, (px, py) in enumerate(_chips3(x, y)):
            cp = pltpu.make_async_remote_copy(
                src_ref=shard_ref, dst_ref=_shard_of(full_ref, axis, 2 * px + py), send_sem=send.at[j],
                recv_sem=recv.at[j], device_id=(px, py, c), device_id_type=MESH)
            cp.wait_send()
            cp.wait_recv()

    return pl.pallas_call(
        body, name=name,
        out_shape=(pltpu.HBM(shard_thru.shape, shard_thru.dtype), pltpu.HBM(full_thru.shape, full_thru.dtype)),
        in_specs=(HBM, HBM, SEM, SEM, ANY), out_specs=(HBM, HBM), input_output_aliases={0: 0, 1: 1},
        compiler_params=pltpu.CompilerParams(has_side_effects=EFFECT),
    )(shard_thru, full_thru, send, recv, after)[1]


def _piece_shape(shape, axis):
    R, C = shape
    return (R // 2, C // 4) if axis == 1 else (R // 8, C)


def _scatter_start(name, g, axis):
    land_shape = (7,) + _piece_shape(g.shape, axis)

    def body(g_ref, land_ref, send, recv, g_thru, land_thru, token):
        x, y, c = _me()
        for m, (px, py, pc) in _peers7(x, y, c):
            pltpu.make_async_remote_copy(
                src_ref=_region(g_ref, axis, 2 * px + py, pc), dst_ref=land_ref.at[m], send_sem=send.at[m],
                recv_sem=recv.at[m], device_id=(px, py, pc), device_id_type=MESH).start()
        token[...] = jnp.zeros_like(token)

    return pl.pallas_call(
        body, name=name,
        out_shape=(pltpu.SemaphoreType.DMA((7,)), pltpu.SemaphoreType.DMA((7,)), pltpu.HBM(g.shape, g.dtype),
                   pltpu.HBM(land_shape, g.dtype), jax.ShapeDtypeStruct(TOKEN, F32)),
        in_specs=(HBM, HBM), out_specs=(SEM, SEM, HBM, HBM, pl.BlockSpec(memory_space=pltpu.VMEM)),
        input_output_aliases={0: 2, 1: 3}, compiler_params=pltpu.CompilerParams(has_side_effects=EFFECT),
    )(pltpu.with_memory_space_constraint(g, pltpu.HBM),
      pltpu.with_memory_space_constraint(lax.empty(land_shape, g.dtype), pltpu.HBM))


def _scatter_wait(name, started, axis, after):
    send, recv, g_thru, land_thru, _ = started

    def body(g_ref, land_ref, send, recv, after_ref, g_out, land_out):
        x, y, c = _me()
        for m, (px, py, pc) in _peers7(x, y, c):
            cp = pltpu.make_async_remote_copy(
                src_ref=_region(g_ref, axis, 2 * px + py, pc), dst_ref=land_ref.at[m], send_sem=send.at[m],
                recv_sem=recv.at[m], device_id=(px, py, pc), device_id_type=MESH)
            cp.wait_send()
            cp.wait_recv()

    return pl.pallas_call(
        body, name=name,
        out_shape=(pltpu.HBM(g_thru.shape, g_thru.dtype), pltpu.HBM(land_thru.shape, land_thru.dtype)),
        in_specs=(HBM, HBM, SEM, SEM, ANY), out_specs=(HBM, HBM), input_output_aliases={0: 0, 1: 1},
        compiler_params=pltpu.CompilerParams(has_side_effects=EFFECT),
    )(g_thru, land_thru, send, recv, after)


def _own_piece(g, axis):
    x, y, c = _me()
    pr, pc_ = _piece_shape(g.shape, axis)
    if axis == 1:
        return lax.dynamic_slice(g, (c * pr, (2 * x + y) * pc_), (pr, pc_))
    return lax.dynamic_slice(g, ((2 * x + y) * 2 * pr + c * pr, 0), (pr, pc_))


def _sum_pieces(name, own, slots, tr=256):
    S, R, C = slots.shape
    tr = _tile(R, tr, 16)
    nblk = R // tr
    c_arr = lax.axis_index("c").astype(jnp.int32).reshape(1)

    def body(c_ref, own_ref, s_ref, o_ref):
        acc = own_ref[...].astype(F32)
        for s in range(S):
            acc = acc + s_ref[s].astype(F32)
        o_ref[...] = acc

    grid_spec = pltpu.PrefetchScalarGridSpec(
        num_scalar_prefetch=1, grid=(nblk,),
        in_specs=[pl.BlockSpec((tr, C), lambda i, c_ref: (i, 0)), pl.BlockSpec((S, tr, C), lambda i, c_ref: (0, i, 0))],
        out_specs=pl.BlockSpec((tr, C), lambda i, c_ref: (c_ref[0] * nblk + i, 0)))
    return pl.pallas_call(
        body, name=name, grid_spec=grid_spec, out_shape=jax.ShapeDtypeStruct((2 * R, C), F32),
        compiler_params=_params(("parallel",)),
    )(c_arr, own, slots)


def _join_start(name, buf):
    def body(buf_ref, send, recv, buf_thru, token):
        x, y, c = _me()
        hr = buf_ref.shape[0] // 2
        mine = buf_ref.at[pl.ds(c * hr, hr), :]
        pltpu.make_async_remote_copy(src_ref=mine, dst_ref=mine, send_sem=send, recv_sem=recv,
                                     device_id=(x, y, 1 - c), device_id_type=MESH).start()
        token[...] = jnp.zeros_like(token)

    return pl.pallas_call(
        body, name=name,
        out_shape=(pltpu.SemaphoreType.DMA(()), pltpu.SemaphoreType.DMA(()), pltpu.HBM(buf.shape, buf.dtype),
                   jax.ShapeDtypeStruct(TOKEN, F32)),
        in_specs=(HBM,), out_specs=(SEM, SEM, HBM, pl.BlockSpec(memory_space=pltpu.VMEM)),
        input_output_aliases={0: 2}, compiler_params=pltpu.CompilerParams(has_side_effects=EFFECT),
    )(pltpu.with_memory_space_constraint(buf, pltpu.HBM))


def _join_wait(name, started, after):
    send, recv, buf_thru, _ = started

    def body(buf_ref, send, recv, after_ref, buf_out):
        x, y, c = _me()
        hr = buf_ref.shape[0] // 2
        theirs = buf_ref.at[pl.ds((1 - c) * hr, hr), :]
        cp = pltpu.make_async_remote_copy(src_ref=theirs, dst_ref=theirs, send_sem=send, recv_sem=recv,
                                          device_id=(x, y, 1 - c), device_id_type=MESH)
        cp.wait_send()
        cp.wait_recv()

    return pl.pallas_call(
        body, name=name, out_shape=pltpu.HBM(buf_thru.shape, buf_thru.dtype),
        in_specs=(HBM, SEM, SEM, ANY), out_specs=HBM, input_output_aliases={0: 0},
        compiler_params=pltpu.CompilerParams(has_side_effects=EFFECT),
    )(buf_thru, send, recv, after)


def _adamw(name, w, g, m, v, tr=256, deps=()):
    R, C = w.shape
    tr = _tile(R, tr, 8)
    c1 = 1.0 - ADAM_B1 ** ADAM_STEP
    c2 = 1.0 - ADAM_B2 ** ADAM_STEP

    def body(w_ref, g_ref, m_ref, v_ref, *rest):
        d_ref, nm_ref, nv_ref = rest[len(deps):]
        gv = g_ref[...]
        nm = ADAM_B1 * m_ref[...] + (1.0 - ADAM_B1) * gv
        nv = ADAM_B2 * v_ref[...] + (1.0 - ADAM_B2) * jnp.square(gv)
        d_ref[...] = -ADAM_LR * ((nm / c1) / (jnp.sqrt(nv / c2) + ADAM_EPS) + ADAM_WD * w_ref[...])
        nm_ref[...] = nm
        nv_ref[...] = nv

    spec = pl.BlockSpec((tr, C), lambda i: (i, 0))
    return pl.pallas_call(
        body, name=name, grid=(R // tr,), in_specs=[spec] * 4 + [pl.BlockSpec(TOKEN, lambda i: (0, 0))] * len(deps),
        out_specs=[spec] * 3, out_shape=[jax.ShapeDtypeStruct((R, C), F32)] * 3,
        compiler_params=_params(("parallel",)),
    )(w, g, m, v, *deps)


def kernel(x, p, g_mix, w_in, w_dw, b_dw, g_conv_ln, b_conv_ln, w_out, g_ffn, w_gate, w_up, w_down, g_ple, w_pgate, b_pgate, w_ple, g_final, loss_target, m_g_mix, m_w_in, m_w_dw, m_b_dw, m_g_conv_ln, m_b_conv_ln, m_w_out, m_g_ffn, m_w_gate, m_w_up, m_w_down, m_g_ple, m_w_pgate, m_b_pgate, m_w_ple, m_g_final, v_g_mix, v_w_in, v_w_dw, v_b_dw, v_g_conv_ln, v_b_conv_ln, v_w_out, v_g_ffn, v_w_gate, v_w_up, v_w_down, v_g_ple, v_w_pgate, v_b_pgate, v_w_ple, v_g_final):
    T, D = x.shape[1], x.shape[2]
    DC = b_dw.shape[1]
    DA = D - DC
    H = DA // HEAD_DIM
    DP = p.shape[3]
    assert T % SUPER == 0 and DA == DC
    x2 = x.reshape(T, D)
    p2 = p.reshape(T, DP)
    tgt = loss_target.reshape(T, D)
    g_final2 = g_final.reshape(1, D)

    big = dict(w_in=(w_in[0], 1), w_out=(w_out[0], 0), w_gate=(w_gate[0], 1), w_up=(w_up[0], 1),
               w_down=(w_down[0], 0), w_pgate=(w_pgate[0], 0), w_ple=(w_ple[0], 1))
    axis_of = {k: big[k][1] for k in big}
    dw_shard = jnp.pad(w_dw.reshape(CONV_W, -1), ((0, HALO - CONV_W), (0, 0)))
    w_in_full, w_dw_full = _gather_now(big["w_in"][0].astype(BF16), dw_shard)
    travelling = {}

    def fetch(keys, after):
        for k in keys:
            travelling[k] = _gather_start("gather_start_" + k, big[k][0].astype(BF16), axis_of[k], after)
        return [travelling[k][-1] for k in keys]

    issued = fetch(["w_out", "w_gate", "w_up"], w_in_full)

    def weight(k, after):
        return _gather_wait("gather_wait_" + k, travelling[k], axis_of[k], after)

    a = _rms_fwd("rms_mix", x2, g_mix, deps=issued)
    z = _matmul("mm_in", [[(a, w_in_full)]], "nn", [F32], _plain)[0]
    cat, o_attn, lse = _attn_fwd(z, D, H)
    cat, conv_c = _conv_fwd(z, cat, w_dw_full, b_dw, g_conv_ln, b_conv_ln, 3 * DA // DC)
    W = dict(w_in=w_in_full)
    W["w_out"] = weight("w_out", cat)
    h1 = _matmul("mm_out", [[(cat, W["w_out"])]], "nn", [F32], _add_residual, extras=[(x2, "mn")])[0]
    f = _rms_fwd("rms_ffn", h1, g_ffn)
    W["w_gate"] = weight("w_gate", f)
    W["w_up"] = weight("w_up", f)

    def swiglu(accs, extras):
        gt, up = accs
        return gt, up, (gt * jax.nn.sigmoid(gt)) * up

    gate, up, act = _matmul("mm_gate_up", [[(f, W["w_gate"])], [(f, W["w_up"])]], "nn", [BF16, BF16, BF16],
                            swiglu, tm=1024, tn=512, sub=256,
                            deps=fetch(["w_down", "w_pgate", "w_ple"], W["w_up"]))
    W["w_down"] = weight("w_down", act)
    h2 = _matmul("mm_down", [[(act, W["w_down"])]], "nn", [F32], _add_residual, extras=[(h1, "mn")], tm=512, tk=5632)[0]
    n2 = _rms_fwd("rms_ple", h2, g_ple)
    W["w_ple"] = weight("w_ple", n2)
    W["w_pgate"] = weight("w_pgate", n2)
    sent, joined = {}, {}

    def send_grad(k, g):
        sent[k] = _scatter_start("scatter_start_" + k, g, axis_of[k])
        return sent[k][-1]

    def reduce_grad(k, after):
        g_thru, land = _scatter_wait("scatter_wait_" + k, sent[k], axis_of[k], after)
        half = _sum_pieces("sum_" + k, _own_piece(g_thru, axis_of[k]), land)
        joined[k] = _join_start("join_start_" + k, half)
        return joined[k][-1]

    dh3, de, du4, dg_final, db_pgate, loss_part = _ple_loss_head(n2, W["w_pgate"], b_pgate, h2, p2, W["w_ple"], tgt,
                                                                 g_final2)
    t_ple = send_grad("w_ple", _matmul("mm_dw_ple", [[(p2, de)]], "tn", [BF16], _plain, tk=1024)[0])
    t_pgate = send_grad("w_pgate", _matmul("mm_dw_pgate", [[(n2, du4)]], "tn", [BF16], _plain, tk=2048)[0])
    dh2, dh2_16, dg_ple = _rms_bwd("rms_ple_bwd", du4, h2, g_ple, dh3, weight=W["w_pgate"])
    t_down = send_grad("w_down", _matmul("mm_dw_down", [[(act, dh2_16)]], "tn", [BF16], _plain, tm=1408, tk=1024,
                                           deps=[t_ple, t_pgate])[0])

    def swiglu_bwd(accs, extras):
        gt, up = extras[0].astype(F32), extras[1].astype(F32)
        sg = jax.nn.sigmoid(gt)
        dact = accs[0]
        return dact * up * (sg * (1.0 + gt * (1.0 - sg))), dact * (gt * sg)

    dgate, dup = _matmul("mm_dact", [[(dh2_16, W["w_down"])]], "nt", [BF16, BF16], swiglu_bwd,
                         extras=[(gate, "mn"), (up, "mn")], tm=1024, tn=512, sub=256, deps=[t_down])
    t_gate = send_grad("w_gate", _matmul("mm_dw_gate", [[(f, dgate)]], "tn", [BF16], _plain, tn=1408, tk=2048)[0])
    t_up = send_grad("w_up", _matmul("mm_dw_up", [[(f, dup)]], "tn", [BF16], _plain, tn=1408, tk=2048,
                                     deps=[t_gate])[0])
    df = _matmul("mm_df", [[(dgate, W["w_gate"]), (dup, W["w_up"])]], "nt", [F32], _plain, tm=1024, tn=256,
                 tk=5632, deps=[t_up], vmem=VMEM_LIMIT_LARGE)[0]
    dh1, dh1_16, dg_ffn = _rms_bwd("rms_ffn_bwd", df, h1, g_ffn, dh2)
    t_out = send_grad("w_out", _matmul("mm_dw_out", [[(cat, dh1_16)]], "tn", [BF16], _plain, tk=2048)[0])
    dcat = _matmul("mm_dcat", [[(dh1_16, W["w_out"])]], "nt", [F32], _plain, deps=[t_out])[0]
    j1 = [reduce_grad(k, dcat) for k in ("w_ple", "w_pgate", "w_down")]
    dc, dg_ln, db_ln, db_dw = _conv_ln_bwd(dcat, conv_c, g_conv_ln, b_conv_ln, deps=j1)
    dcv, dcg, dw_dw = _conv_bwd(z, dc, w_dw_full, 3 * DA // DC)
    dq, dk, dv = _attn_bwd(z, dcat, o_attn, lse, H)
    dz = [dq, dk, dv, dcv, dcg]
    dw_in = jnp.concatenate([_matmul("mm_dw_in%d" % n, [[(a, part)]], "tn", [BF16], _plain, tk=2048)[0]
                             for n, part in enumerate(dz)], axis=1)
    t_in = send_grad("w_in", dw_in)
    j2 = [reduce_grad(k, dw_in) for k in ("w_gate", "w_up", "w_out")]
    da = _matmul("mm_da", [[(part, W["w_in"], n) for n, part in enumerate(dz)]], "nt", [F32], _plain, tm=512,
                 tk=DA, deps=[t_in] + j2)[0]
    grad_x, _, dg_mix = _rms_bwd("rms_mix_bwd", da, x2, g_mix, dh1)

    wide = [dg_mix, dg_ffn, dg_ple, db_pgate, dg_final,
            jnp.concatenate([db_dw, dg_ln], axis=1), jnp.concatenate([db_ln, jnp.zeros_like(db_ln)], axis=1),
            jnp.pad(loss_part, ((0, 0), (0, D - 128))),
            dw_dw.reshape(HALO * DC // D, D)]
    small = jnp.concatenate(wide, axis=0)
    small = jnp.pad(small, ((0, -small.shape[0] % 8), (0, 0)))
    small_sum = _sum_slots("sum_small", _exchange_small(small))
    j3 = reduce_grad("w_in", small_sum)

    grads, deltas, new_m, new_v = {}, {}, {}, {}
    moments = dict(w_in=(m_w_in, v_w_in), w_out=(m_w_out, v_w_out), w_gate=(m_w_gate, v_w_gate),
                   w_up=(m_w_up, v_w_up), w_down=(m_w_down, v_w_down), w_pgate=(m_w_pgate, v_w_pgate),
                   w_ple=(m_w_ple, v_w_ple))
    last, deps = small_sum, [j3]
    for k in ("w_ple", "w_pgate", "w_down", "w_gate", "w_up", "w_out", "w_in"):
        g_k = _join_wait("join_wait_" + k, joined[k], last)
        d_, m_, v_ = _adamw("adamw_" + k, big[k][0], g_k, moments[k][0][0], moments[k][1][0], deps=deps)
        grads[k], deltas[k], new_m[k], new_v[k] = g_k[None], d_[None], m_[None], v_[None]
        last, deps = d_, ()

    half = lambda r, lo: small_sum[r:r + 1, lo * DC:(lo + 1) * DC]
    vec = dict(g_mix=small_sum[0:1], g_ffn=small_sum[1:2], g_ple=small_sum[2:3], b_pgate=small_sum[3:4],
               g_final=small_sum[4:5], b_dw=half(5, 0), g_conv_ln=half(5, 1), b_conv_ln=half(6, 0))
    loss = small_sum[7, 0]
    dw_dw_sum = small_sum[8:8 + HALO * DC // D].reshape(HALO, DC)
    s_me = 2 * lax.axis_index("x") + lax.axis_index("y")
    cs = w_dw.shape[3]
    vec["w_dw"] = lax.dynamic_slice(dw_dw_sum, (0, s_me * cs), (CONV_W, cs))
    small_w = dict(g_mix=(g_mix, m_g_mix, v_g_mix), g_ffn=(g_ffn, m_g_ffn, v_g_ffn), g_ple=(g_ple, m_g_ple, v_g_ple),
                   b_pgate=(b_pgate, m_b_pgate, v_b_pgate), g_final=(g_final, m_g_final, v_g_final),
                   b_dw=(b_dw, m_b_dw, v_b_dw), g_conv_ln=(g_conv_ln, m_g_conv_ln, v_g_conv_ln),
                   b_conv_ln=(b_conv_ln, m_b_conv_ln, v_b_conv_ln), w_dw=(w_dw, m_w_dw, v_w_dw))
    for k, (w_, m_, v_) in small_w.items():
        shape = w_.shape
        g2 = vec[k]
        to2 = lambda t: t.reshape(g2.shape)
        d_, nm_, nv_ = _adamw("adamw_" + k, to2(w_), g2, to2(m_), to2(v_))
        grads[k], deltas[k], new_m[k], new_v[k] = (t.reshape(shape) for t in (g2, d_, nm_, nv_))

    order = ["g_mix", "w_in", "w_dw", "b_dw", "g_conv_ln", "b_conv_ln", "w_out", "g_ffn", "w_gate", "w_up", "w_down",
             "g_ple", "w_pgate", "b_pgate", "w_ple", "g_final"]
    return (loss, grad_x.reshape(x.shape), *[grads[k] for k in order], *[deltas[k] for k in order],
            *[new_m[k] for k in order], *[new_v[k] for k in order])
```

```python
import functools

import jax
import jax.numpy as jnp
from jax import lax
from jax.experimental import pallas as pl
from jax.experimental.pallas import tpu as pltpu

F32 = jnp.float32
BF16 = jnp.bfloat16

EPS = 1e-6
HEAD_DIM = 128
BLK = 128
DILATIONS = (1, 4, 16)
SUPER = BLK * DILATIONS[-1]
CONV_W = 31
HALO = 32
ADAM_LR, ADAM_B1, ADAM_B2, ADAM_EPS, ADAM_WD, ADAM_STEP = 0.001, 0.9, 0.999, 1e-08, 0.01, 10

V7X_VMEM_BYTES = 64 * 1024 * 1024
VMEM_LIMIT = V7X_VMEM_BYTES * 3 // 4
VMEM_LIMIT_LARGE = V7X_VMEM_BYTES * 15 // 16
MESH = pl.DeviceIdType.MESH
ANY = pl.BlockSpec(memory_space=pl.ANY)
HBM = pl.BlockSpec(memory_space=pltpu.HBM)
SEM = pl.BlockSpec(memory_space=pltpu.SEMAPHORE)
EFFECT = pltpu.SideEffectType.DATAFLOW_SIDE_EFFECTING
TOKEN = (8, 128)


def _params(semantics=None, vmem=VMEM_LIMIT, **kw):
    return pltpu.CompilerParams(dimension_semantics=semantics, vmem_limit_bytes=vmem, **kw)


def _tile(n, want, mult=128):
    if n <= want:
        return n
    for t in range(want - want % mult, 0, -mult):
        if n % t == 0:
            return t
    raise ValueError((n, want, mult))


_DIMS = {"nn": ((1,), (0,)), "nt": ((1,), (1,)), "tn": ((0,), (0,))}


def _matmul(name, groups, mode, out_dtypes, epilogue, extras=(), tm=1024, tn=1024, tk=2048, sub=None, deps=(),
            vmem=VMEM_LIMIT):
    a0, b0 = groups[0][0][:2]
    if mode == "nn":
        (M, K), N = a0.shape, b0.shape[1]
    elif mode == "nt":
        (M, K), N = a0.shape, b0.shape[0]
    else:
        (K, M), N = a0.shape, b0.shape[1]
    tm, tn, tk = _tile(M, tm), _tile(N, tn), _tile(K, tk)
    nk = K // tk
    if mode == "tn":
        a_spec = pl.BlockSpec((tk, tm), lambda i, j, k: (k, i))
    else:
        a_spec = pl.BlockSpec((tm, tk), lambda i, j, k: (i, k))
    operands, in_specs = [], []
    for grp in groups:
        for a, b, *k0 in grp:
            k0 = k0[0] if k0 else 0
            if mode == "nt":
                b_spec = pl.BlockSpec((tn, tk), functools.partial(lambda i, j, k, k0: (j, k + k0), k0=k0))
            else:
                b_spec = pl.BlockSpec((tk, tn), functools.partial(lambda i, j, k, k0: (k + k0, j), k0=k0))
            operands += [a, b]
            in_specs += [a_spec, b_spec]
    for arr, kind in extras:
        operands.append(arr)
        if kind == "mn":
            in_specs.append(pl.BlockSpec((tm, tn), lambda i, j, k: (i, j)))
        else:
            in_specs.append(pl.BlockSpec((1, tn), lambda i, j, k: (0, j)))
    for tok in deps:
        operands.append(tok)
        in_specs.append(pl.BlockSpec(TOKEN, lambda i, j, k: (0, 0)))
    n_pairs = [len(g) for g in groups]
    n_ex, n_out, n_grp, n_dep = len(extras), len(out_dtypes), len(groups), len(deps)
    out_shape = [jax.ShapeDtypeStruct((M, N), dt) for dt in out_dtypes]
    out_specs = [pl.BlockSpec((tm, tn), lambda i, j, k: (i, j)) for _ in out_dtypes]

    kinds = [kind for _, kind in extras]
    sub = tm if (sub is None or nk > 1 or mode == "tn") else sub
    assert tm % sub == 0

    def body(*refs):
        ex_refs = refs[2 * sum(n_pairs):2 * sum(n_pairs) + n_ex]
        pos = 2 * sum(n_pairs) + n_ex + n_dep
        out_refs = refs[pos:pos + n_out]
        acc_refs = refs[pos + n_out:]

        def products(rows):
            pos, parts = 0, []
            for g in range(n_grp):
                part = None
                for _ in range(n_pairs[g]):
                    a_ref, b_ref = refs[pos], refs[pos + 1]
                    pos += 2
                    d = lax.dot_general(a_ref[rows].astype(BF16), b_ref[...].astype(BF16),
                                        (_DIMS[mode], ((), ())), preferred_element_type=F32)
                    part = d if part is None else part + d
                parts.append(part)
            return parts

        def finish(accs, rows):
            outs = epilogue(accs, [e[rows] if kind == "mn" else e[...] for e, kind in zip(ex_refs, kinds)])
            for o_ref, o in zip(out_refs, outs):
                o_ref[rows] = o.astype(o_ref.dtype)

        if nk == 1:
            for r in range(tm // sub):
                rows = (pl.ds(r * sub, sub), slice(None)) if sub < tm else (slice(None), slice(None))
                finish(products(rows), rows)
        else:
            k = pl.program_id(2)
            parts = products((slice(None), slice(None)))

            @pl.when(k == 0)
            def _():
                for acc, part in zip(acc_refs, parts):
                    acc[...] = part

            @pl.when(k > 0)
            def _():
                for acc, part in zip(acc_refs, parts):
                    acc[...] += part

            @pl.when(k == nk - 1)
            def _():
                finish([acc[...] for acc in acc_refs], (slice(None), slice(None)))

    scratch = [pltpu.VMEM((tm, tn), F32) for _ in range(n_grp)] if nk > 1 else []
    return pl.pallas_call(
        body, name=name, grid=(M // tm, N // tn, nk),
        in_specs=in_specs, out_specs=out_specs, out_shape=out_shape, scratch_shapes=scratch,
        compiler_params=_params(("parallel", "parallel", "arbitrary"), vmem),
    )(*operands)


def _plain(accs, extras):
    return (accs[0],)


def _add_residual(accs, extras):
    return (accs[0] + extras[0],)


def _rowwise(name, fn, rows, consts, row_outs, acc_outs, tb, deps=()):
    T = rows[0][0].shape[0]
    tb = _tile(T, tb, 16)
    operands = [r[0] for r in rows] + list(consts) + list(deps)
    in_specs = [pl.BlockSpec((tb, c), functools.partial(lambda i, cb: (i, cb), cb=cb)) for _, c, cb in rows]
    in_specs += [pl.BlockSpec(c.shape, functools.partial(lambda i, nd: (0,) * nd, nd=c.ndim)) for c in consts]
    in_specs += [pl.BlockSpec(TOKEN, lambda i: (0, 0)) for _ in deps]
    out_shape = [jax.ShapeDtypeStruct(s, dt) for s, dt, _, _ in row_outs]
    out_specs = [pl.BlockSpec((tb, c), functools.partial(lambda i, cb: (i, cb), cb=cb)) for _, _, c, cb in row_outs]
    out_shape += [jax.ShapeDtypeStruct(s, F32) for s in acc_outs]
    out_specs += [pl.BlockSpec(s, functools.partial(lambda i, nd: (0,) * nd, nd=len(s))) for s in acc_outs]
    n_rows, n_consts, n_ro, n_dep = len(rows), len(consts), len(row_outs), len(deps)

    def body(*refs):
        row_refs = refs[:n_rows]
        const_refs = refs[n_rows:n_rows + n_consts]
        out_refs = refs[n_rows + n_consts + n_dep:]
        ro, ao = fn([r[...] for r in row_refs], [c[...] for c in const_refs])
        for o_ref, o in zip(out_refs[:n_ro], ro):
            o_ref[...] = o.astype(o_ref.dtype)
        if acc_outs:
            i = pl.program_id(0)

            @pl.when(i == 0)
            def _():
                for a_ref, a in zip(out_refs[n_ro:], ao):
                    a_ref[...] = a

            @pl.when(i > 0)
            def _():
                for a_ref, a in zip(out_refs[n_ro:], ao):
                    a_ref[...] += a

    return pl.pallas_call(
        body, name=name, grid=(T // tb,), in_specs=in_specs, out_specs=out_specs, out_shape=out_shape,
        compiler_params=_params(("arbitrary",) if acc_outs else ("parallel",)),
    )(*operands)


def _colsum(v):
    return jnp.sum(v, axis=0, keepdims=True)


def _rms_fwd(name, x, g, tb=512, deps=()):
    T, D = x.shape

    def fn(rows, consts):
        xv, gv = rows[0], consts[0]
        r = lax.rsqrt(jnp.mean(xv * xv, axis=-1, keepdims=True) + EPS)
        return ((xv * r) * gv,), ()

    return _rowwise(name, fn, [(x, D, 0)], [g], [((T, D), BF16, D, 0)], [], tb, deps=deps)[0]


def _rms_bwd(name, dy, x, g, resid, tb=256, weight=None):
    T, D = x.shape
    consts = [g] if weight is None else [g, weight]

    def fn(rows, consts):
        dyv, xv, rv = rows
        gv = consts[0]
        if weight is not None:
            dyv = lax.dot_general(dyv, consts[1], (((1,), (1,)), ((), ())), preferred_element_type=F32)
        r = lax.rsqrt(jnp.mean(xv * xv, axis=-1, keepdims=True) + EPS)
        n = xv * r
        dn = dyv * gv
        dx = r * (dn - n * jnp.mean(dn * n, axis=-1, keepdims=True))
        tot = rv + dx
        return (tot, tot), (_colsum(dyv * n),)

    return _rowwise(name, fn, [(dy, dy.shape[1], 0), (x, D, 0), (resid, D, 0)], consts,
                    [((T, D), F32, D, 0), ((T, D), BF16, D, 0)], [(1, D)], tb)


NBLK = SUPER // BLK


def _classes(ref, dil, rows, start=0, dtype=None):
    parts = [ref[pl.ds(start + r, rows, stride=dil), :] if dil > 1 else ref[pl.ds(start, rows), :]
             for r in range(dil)]
    if dtype is not None:
        parts = [p.astype(dtype) for p in parts]
    return parts


def _keys_with_prev(ref, prev_ref, dil):
    L = SUPER // dil
    own = _classes(ref, dil, L, dtype=BF16)
    last = _classes(prev_ref, dil, BLK, start=SUPER - BLK * dil, dtype=BF16)
    blocks = []
    for r in range(dil):
        ext = jnp.concatenate([last[r], own[r]], axis=0)
        blocks += [ext[j * BLK:(j + 2) * BLK] for j in range(L // BLK)]
    return jnp.stack(blocks, axis=0)


def _band_mask(dil, has_prev):
    qi = lax.broadcasted_iota(jnp.int32, (BLK, 2 * BLK), 0)
    kj = lax.broadcasted_iota(jnp.int32, (BLK, 2 * BLK), 1)
    own = jnp.logical_and(kj >= BLK, kj - BLK <= qi)
    prev = jnp.logical_and(kj < BLK, kj >= qi)
    b = lax.broadcasted_iota(jnp.int32, (NBLK, 1, 1), 0)
    first = (b & (SUPER // (BLK * dil) - 1)) == 0
    prev_ok = jnp.logical_or(jnp.logical_not(first), has_prev)
    return jnp.logical_or(own[None], jnp.logical_and(prev[None], prev_ok))


def _bdot(a, b, ca, cb):
    return lax.dot_general(a, b, (((ca,), (cb,)), ((0,), (0,))), preferred_element_type=F32)


def _put_classes(dst, value, dil, rows, start=0, add=False, src_start=0, src_stride=None):
    src_stride = rows if src_stride is None else src_stride
    for r in range(dil):
        idx = (pl.ds(start + r, rows, stride=dil) if dil > 1 else pl.ds(start, rows), slice(None))
        part = value[src_start + r * src_stride:src_start + r * src_stride + rows]
        dst[idx] = dst[idx] + part if add else part


def _attn_fwd(z, cat_width, n_heads):
    T = z.shape[0]
    H = n_heads
    DA = H * HEAD_DIM
    nb = T // SUPER
    scale = HEAD_DIM ** -0.5

    def body(q_ref, k_ref, v_ref, kp_ref, vp_ref, cat_ref, o_ref, lse_ref, ob, lb):
        has_prev = pl.program_id(1) > 0
        for b, dil in enumerate(DILATIONS):
            L = SUPER // dil
            q3 = jnp.concatenate(_classes(q_ref, dil, L, dtype=BF16), axis=0).reshape(NBLK, BLK, HEAD_DIM)
            k3 = _keys_with_prev(k_ref, kp_ref, dil)
            v3 = _keys_with_prev(v_ref, vp_ref, dil)
            s = jnp.where(_band_mask(dil, has_prev), _bdot(q3, k3, 2, 2) * scale, -jnp.inf)
            m = jnp.max(s, axis=-1, keepdims=True)
            e = jnp.exp(s - m)
            den = jnp.sum(e, axis=-1, keepdims=True)
            o3 = _bdot((e * (1.0 / den)).astype(BF16), v3, 2, 1)
            lse3 = jnp.broadcast_to(m + jnp.log(den), (NBLK, BLK, HEAD_DIM))
            _put_classes(ob.at[b], o3.reshape(SUPER, HEAD_DIM), dil, L)
            _put_classes(lb.at[b], lse3.reshape(SUPER, HEAD_DIM), dil, L)
        l0, l1, l2 = lb[0], lb[1], lb[2]
        mx = jnp.maximum(jnp.maximum(l0, l1), l2)
        tot = mx + jnp.log(jnp.exp(l0 - mx) + jnp.exp(l1 - mx) + jnp.exp(l2 - mx))
        o = jnp.exp(l0 - tot) * ob[0] + jnp.exp(l1 - tot) * ob[1] + jnp.exp(l2 - tot) * ob[2]
        o_ref[...] = o
        cat_ref[...] = o.astype(BF16)
        lse_ref[...] = tot

    blk = (SUPER, HEAD_DIM)
    in_specs = [
        pl.BlockSpec(blk, lambda h, n: (n, h)),
        pl.BlockSpec(blk, lambda h, n: (n, H + h)),
        pl.BlockSpec(blk, lambda h, n: (n, 2 * H + h)),
        pl.BlockSpec(blk, lambda h, n: (jnp.maximum(n - 1, 0), H + h)),
        pl.BlockSpec(blk, lambda h, n: (jnp.maximum(n - 1, 0), 2 * H + h)),
    ]
    out_spec = pl.BlockSpec(blk, lambda h, n: (n, h))
    return pl.pallas_call(
        body, name="attn_fwd", grid=(H, nb), in_specs=in_specs, out_specs=[out_spec] * 3,
        out_shape=[jax.ShapeDtypeStruct((T, cat_width), BF16), jax.ShapeDtypeStruct((T, DA), F32),
                   jax.ShapeDtypeStruct((T, DA), F32)],
        scratch_shapes=[pltpu.VMEM((3, SUPER, HEAD_DIM), F32), pltpu.VMEM((3, SUPER, HEAD_DIM), F32)],
        compiler_params=_params(("parallel", "parallel")),
    )(z, z, z, z, z)


def _attn_bwd(z, dcat, o, lse, n_heads):
    T = z.shape[0]
    H = n_heads
    nb = T // SUPER
    scale = HEAD_DIM ** -0.5

    def body(q_ref, k_ref, v_ref, kp_ref, vp_ref, do_ref, o_ref, lse_ref, dz_q, dz_k, dz_v,
             dq_acc, dk_acc, dv_acc, dkp_acc, dvp_acc, dsum):
        i = pl.program_id(1)
        has_prev = i < nb - 1

        @pl.when(i == 0)
        def _():
            dk_acc[...] = jnp.zeros_like(dk_acc)
            dv_acc[...] = jnp.zeros_like(dv_acc)

        @pl.when(i > 0)
        def _():
            dk_acc[...] = dkp_acc[...]
            dv_acc[...] = dvp_acc[...]

        dq_acc[...] = jnp.zeros_like(dq_acc)
        dkp_acc[...] = jnp.zeros_like(dkp_acc)
        dvp_acc[...] = jnp.zeros_like(dvp_acc)
        dsum[...] = jnp.broadcast_to(jnp.sum(do_ref[...] * o_ref[...], axis=-1, keepdims=True), (SUPER, HEAD_DIM))
        for dil in DILATIONS:
            L = SUPER // dil
            wide = lambda ref: jnp.tile(jnp.concatenate(_classes(ref, dil, L), axis=0).reshape(NBLK, BLK, HEAD_DIM),
                                        (1, 1, 2))
            q3 = jnp.concatenate(_classes(q_ref, dil, L, dtype=BF16), axis=0).reshape(NBLK, BLK, HEAD_DIM)
            do3 = jnp.concatenate(_classes(do_ref, dil, L, dtype=BF16), axis=0).reshape(NBLK, BLK, HEAD_DIM)
            k3 = _keys_with_prev(k_ref, kp_ref, dil)
            v3 = _keys_with_prev(v_ref, vp_ref, dil)
            p = jnp.where(_band_mask(dil, has_prev), jnp.exp(_bdot(q3, k3, 2, 2) * scale - wide(lse_ref)), 0.0)
            ds = (p * (_bdot(do3, v3, 2, 2) - wide(dsum)) * scale).astype(BF16)
            dq = _bdot(ds, k3, 2, 1).reshape(SUPER, HEAD_DIM)
            dk = _bdot(ds, q3, 1, 1)
            dv = _bdot(p.astype(BF16), do3, 1, 1)
            _put_classes(dq_acc, dq, dil, L, add=True)
            for acc, prev_acc, g in ((dk_acc, dkp_acc, dk), (dv_acc, dvp_acc, dv)):
                _put_classes(acc, g[:, BLK:, :].reshape(SUPER, HEAD_DIM), dil, L, add=True)
                to_prev = g[:, :BLK, :].reshape(SUPER, HEAD_DIM)
                if L > BLK:
                    _put_classes(acc, to_prev, dil, L - BLK, add=True, src_start=BLK, src_stride=L)
                _put_classes(prev_acc, to_prev, dil, BLK, start=SUPER - BLK * dil, add=True, src_stride=L)
        dz_q[...] = dq_acc[...].astype(BF16)
        dz_k[...] = dk_acc[...].astype(BF16)
        dz_v[...] = dv_acc[...].astype(BF16)

    blk = (SUPER, HEAD_DIM)
    row = lambda i: nb - 1 - i
    in_specs = [
        pl.BlockSpec(blk, lambda h, i: (row(i), h)),
        pl.BlockSpec(blk, lambda h, i: (row(i), H + h)),
        pl.BlockSpec(blk, lambda h, i: (row(i), 2 * H + h)),
        pl.BlockSpec(blk, lambda h, i: (jnp.maximum(row(i) - 1, 0), H + h)),
        pl.BlockSpec(blk, lambda h, i: (jnp.maximum(row(i) - 1, 0), 2 * H + h)),
        pl.BlockSpec(blk, lambda h, i: (row(i), h)),
        pl.BlockSpec(blk, lambda h, i: (row(i), h)),
        pl.BlockSpec(blk, lambda h, i: (row(i), h)),
    ]
    out_spec = pl.BlockSpec(blk, lambda h, i: (row(i), h))
    return pl.pallas_call(
        body, name="attn_bwd", grid=(H, nb), in_specs=in_specs, out_specs=[out_spec] * 3,
        out_shape=[jax.ShapeDtypeStruct((T, H * HEAD_DIM), BF16)] * 3,
        scratch_shapes=[pltpu.VMEM(blk, F32) for _ in range(6)],
        compiler_params=_params(("parallel", "arbitrary")),
    )(z, z, z, z, z, dcat, o, lse)


def _glu(cv, cg):
    return cv * jax.nn.sigmoid(cg)


SUBLANES = 8


def _fill_shifted(sh, ext, rows):
    for b in range(1, SUBLANES):
        sh[b - 1, pl.ds(0, rows - SUBLANES), :] = ext[pl.ds(b, rows - SUBLANES), :]


def _window(sh, ext, offset, rows):
    b = offset % SUBLANES
    if b == 0:
        return ext[pl.ds(offset, rows), :]
    return sh[b - 1, pl.ds(offset - b, rows), :]


def _conv_fwd(z, cat, w_dw, b_dw, g_ln, b_ln, col0, tb=512):
    T = z.shape[0]
    DC = w_dw.shape[1]
    tb = _tile(T, tb, HALO)
    hb = tb // HALO
    cat_cb = cat.shape[1] // DC - 1

    def body(cv_ref, cg_ref, cvh_ref, cgh_ref, w_ref, bdw_ref, g_ref, b_ref, cat_in, cat_ref, c_ref, u_ext, sh):
        i = pl.program_id(0)
        halo = _glu(cvh_ref[...], cgh_ref[...])
        u_ext[pl.ds(0, HALO), :] = jnp.where(i > 0, halo, 0.0)
        u_ext[pl.ds(HALO, tb), :] = _glu(cv_ref[...], cg_ref[...])
        _fill_shifted(sh, u_ext, tb + HALO)
        acc = jnp.broadcast_to(bdw_ref[...], (tb, DC))
        for j in range(CONV_W):
            acc = acc + w_ref[pl.ds(j, 1), :] * _window(sh, u_ext, HALO - (CONV_W - 1) + j, tb)
        c_ref[...] = acc
        mu = jnp.mean(acc, axis=-1, keepdims=True)
        var = jnp.mean(jnp.square(acc - mu), axis=-1, keepdims=True)
        y = (acc - mu) * lax.rsqrt(var + EPS) * g_ref[...] + b_ref[...]
        cat_ref[...] = (y * jax.nn.sigmoid(y)).astype(BF16)

    cur = lambda cb: pl.BlockSpec((tb, DC), lambda i: (i, cb))
    halo = lambda cb: pl.BlockSpec((HALO, DC), lambda i: (jnp.maximum(i * hb - 1, 0), cb))
    whole = lambda a: pl.BlockSpec(a.shape, lambda i: (0, 0))
    return pl.pallas_call(
        body, name="conv_fwd", grid=(T // tb,),
        in_specs=[cur(col0), cur(col0 + 1), halo(col0), halo(col0 + 1), whole(w_dw), whole(b_dw), whole(g_ln),
                  whole(b_ln), ANY],
        out_specs=[pl.BlockSpec((tb, DC), lambda i: (i, cat_cb)), pl.BlockSpec((tb, DC), lambda i: (i, 0))],
        out_shape=[jax.ShapeDtypeStruct(cat.shape, cat.dtype), jax.ShapeDtypeStruct((T, DC), F32)],
        scratch_shapes=[pltpu.VMEM((tb + HALO, DC), F32), pltpu.VMEM((SUBLANES - 1, tb + HALO, DC), F32)],
        input_output_aliases={8: 0},
        compiler_params=_params(("parallel",)),
    )(z, z, z, z, w_dw, b_dw, g_ln, b_ln, cat)


def _conv_ln_bwd(dcat, c, g_ln, b_ln, tb=256, deps=()):
    T, DC = c.shape
    d_cb = dcat.shape[1] // DC - 1

    def fn(rows, consts):
        dov, cv_ = rows
        gv, bv = consts
        mu = jnp.mean(cv_, axis=-1, keepdims=True)
        xc = cv_ - mu
        rstd = lax.rsqrt(jnp.mean(jnp.square(xc), axis=-1, keepdims=True) + EPS)
        ln = xc * rstd
        y = ln * gv + bv
        sg = jax.nn.sigmoid(y)
        dy = dov * (sg * (1.0 + y * (1.0 - sg)))
        dln = dy * gv
        dc = rstd * (dln - jnp.mean(dln, axis=-1, keepdims=True) - ln * jnp.mean(dln * ln, axis=-1, keepdims=True))
        return (dc,), (_colsum(dy * ln), _colsum(dy), _colsum(dc))

    return _rowwise("conv_ln_bwd", fn, [(dcat, DC, d_cb), (c, DC, 0)], [g_ln, b_ln],
                    [((T, DC), F32, DC, 0)], [(1, DC)] * 3, tb, deps=deps)


def _conv_bwd(z, dc, w_dw, col0, tb=512):
    T, DC = dc.shape
    tb = _tile(T, tb, HALO)
    hb = tb // HALO
    nblk = T // tb

    def body(cv_ref, cg_ref, dc_ref, dcn_ref, w_ref, dcv_ref, dcg_ref, dw_ref, dc_ext, sh):
        i = pl.program_id(0)
        cv, cg = cv_ref[...], cg_ref[...]
        sg = jax.nn.sigmoid(cg)
        u = cv * sg
        dc_ext[pl.ds(0, tb), :] = dc_ref[...]
        dc_ext[pl.ds(tb, HALO), :] = jnp.where(i < nblk - 1, dcn_ref[...], 0.0)
        _fill_shifted(sh, dc_ext, tb + HALO)

        @pl.when(i == 0)
        def _():
            dw_ref[...] = jnp.zeros_like(dw_ref)

        du = jnp.zeros((tb, DC), F32)
        for j in range(CONV_W):
            d_j = _window(sh, dc_ext, CONV_W - 1 - j, tb)
            du = du + w_ref[pl.ds(j, 1), :] * d_j
            dw_ref[pl.ds(j, 1), :] += _colsum(u * d_j)
        dcv_ref[...] = (du * sg).astype(BF16)
        dcg_ref[...] = (du * cv * sg * (1.0 - sg)).astype(BF16)

    cur = lambda cb: pl.BlockSpec((tb, DC), lambda i: (i, cb))
    nxt = pl.BlockSpec((HALO, DC), lambda i: (jnp.minimum((i + 1) * hb, T // HALO - 1), 0))
    return pl.pallas_call(
        body, name="conv_bwd", grid=(nblk,),
        in_specs=[cur(col0), cur(col0 + 1), cur(0), nxt, pl.BlockSpec(w_dw.shape, lambda i: (0, 0))],
        out_specs=[cur(0), cur(0), pl.BlockSpec((HALO, DC), lambda i: (0, 0))],
        out_shape=[jax.ShapeDtypeStruct((T, DC), BF16), jax.ShapeDtypeStruct((T, DC), BF16),
                   jax.ShapeDtypeStruct((HALO, DC), F32)],
        scratch_shapes=[pltpu.VMEM((tb + HALO, DC), F32), pltpu.VMEM((SUBLANES - 1, tb + HALO, DC), F32)],
        compiler_params=_params(("arbitrary",)),
    )(z, z, dc, dc, w_dw)


def _ple_loss_head(n2, w_pgate, b_pgate, h2, p, w_ple, target, g_final, tb=256):
    T, D = h2.shape

    def fn(rows, consts):
        n2v, h2v, pv, tv = rows
        wp, bp, wple, gv = consts
        gt = jax.nn.sigmoid(lax.dot_general(n2v, wp, (((1,), (0,)), ((), ())), preferred_element_type=F32) + bp)
        ev = lax.dot_general(pv.astype(BF16), wple, (((1,), (0,)), ((), ())), preferred_element_type=F32)
        hv = h2v + ev * gt
        r = lax.rsqrt(jnp.mean(hv * hv, axis=-1, keepdims=True) + EPS)
        n = hv * r
        diff = n * gv - tv
        loss = 0.5 * jnp.sum(jnp.mean(jnp.square(diff), axis=-1, keepdims=True), axis=0, keepdims=True)
        dy = diff * (1.0 / D)
        dn = dy * gv
        dh = r * (dn - n * jnp.mean(dn * n, axis=-1, keepdims=True))
        du4 = dh * ev * gt * (1.0 - gt)
        return (dh, dh * gt, du4), (_colsum(dy * n), _colsum(du4), jnp.broadcast_to(loss, (1, 128)))

    return _rowwise("ple_loss_head", fn, [(n2, D, 0), (h2, D, 0), (p, p.shape[1], 0), (target, D, 0)],
                    [w_pgate, b_pgate, w_ple, g_final],
                    [((T, D), F32, D, 0), ((T, D), BF16, D, 0), ((T, D), BF16, D, 0)],
                    [(1, D), (1, D), (1, 128)], tb)


def _me():
    return lax.axis_index("x"), lax.axis_index("y"), lax.axis_index("c")


def _chips3(x, y):
    return [(1 - x, y), (x, 1 - y), (1 - x, 1 - y)]


def _peers7(x, y, c):
    for m in range(1, 8):
        yield m - 1, (x ^ (m >> 2), y ^ ((m >> 1) & 1), c ^ (m & 1))


def _shard_of(ref, axis, s):
    R, C = ref.shape
    if axis == 1:
        return ref.at[:, pl.ds(s * (C // 4), C // 4)]
    return ref.at[pl.ds(s * (R // 4), R // 4), :]


def _region(ref, axis, shard, half):
    R, C = ref.shape
    if axis == 1:
        cs, hr = C // 4, R // 2
        return ref.at[pl.ds(half * hr, hr), pl.ds(shard * cs, cs)]
    hr = R // 8
    return ref.at[pl.ds(shard * 2 * hr + half * hr, hr), :]


def _gather_now(shard, small):
    R, C = shard.shape
    cs = small.shape[1]

    def body(shard_ref, small_ref, out_ref, small_out, send, recv, fsend, frecv, lsem):
        x, y, c = _me()
        me_s = 2 * x + y
        sibling = (x, y, 1 - c)
        chips = _chips3(x, y)
        half = shard_ref.at[pl.ds(c * (R // 2), R // 2), :]
        locals_ = [pltpu.make_async_copy(shard_ref, _shard_of(out_ref, 1, me_s), lsem.at[0]),
                   pltpu.make_async_copy(small_ref, _shard_of(small_out, 1, me_s), lsem.at[1])]
        for cp in locals_:
            cp.start()
        firsts = []
        for j, (px, py) in enumerate(chips):
            firsts.append(pltpu.make_async_remote_copy(
                src_ref=half, dst_ref=_region(out_ref, 1, me_s, c), send_sem=send.at[0, j], recv_sem=recv.at[0, j],
                device_id=(px, py, c), device_id_type=MESH))
            firsts.append(pltpu.make_async_remote_copy(
                src_ref=small_ref, dst_ref=_shard_of(small_out, 1, me_s), send_sem=send.at[1, j],
                recv_sem=recv.at[1, j], device_id=(px, py, c), device_id_type=MESH))
        for cp in firsts:
            cp.start()
        relays = []
        for j, (px, py) in enumerate(chips):
            landed = _region(out_ref, 1, 2 * px + py, c)
            pltpu.make_async_remote_copy(
                src_ref=half, dst_ref=landed, send_sem=send.at[0, j], recv_sem=recv.at[0, j],
                device_id=(px, py, c), device_id_type=MESH).wait_recv()
            relay = pltpu.make_async_remote_copy(
                src_ref=landed, dst_ref=landed, send_sem=fsend.at[j], recv_sem=frecv.at[j],
                device_id=sibling, device_id_type=MESH)
            relay.start()
            relays.append(relay)
        for j, (px, py) in enumerate(chips):
            pltpu.make_async_remote_copy(
                src_ref=small_ref, dst_ref=_shard_of(small_out, 1, 2 * px + py), send_sem=send.at[1, j],
                recv_sem=recv.at[1, j], device_id=(px, py, c), device_id_type=MESH).wait_recv()
            theirs = _region(out_ref, 1, 2 * px + py, 1 - c)
            pltpu.make_async_remote_copy(
                src_ref=theirs, dst_ref=theirs, send_sem=fsend.at[j], recv_sem=frecv.at[j],
                device_id=sibling, device_id_type=MESH).wait_recv()
        for cp in firsts + relays:
            cp.wait_send()
        for cp in locals_:
            cp.wait()

    return pl.pallas_call(
        body, name="gather_now", in_specs=[ANY, ANY], out_specs=[ANY, ANY],
        out_shape=[jax.ShapeDtypeStruct((R, 4 * C), shard.dtype),
                   jax.ShapeDtypeStruct((small.shape[0], 4 * cs), small.dtype)],
        scratch_shapes=[pltpu.SemaphoreType.DMA((2, 3)), pltpu.SemaphoreType.DMA((2, 3)),
                        pltpu.SemaphoreType.DMA((3,)), pltpu.SemaphoreType.DMA((3,)), pltpu.SemaphoreType.DMA((2,))],
        compiler_params=_params(),
    )(shard, small)


def _exchange_small(small):
    def body(small_ref, out_ref, send, recv, lsem):
        x, y, c = _me()
        me = 4 * x + 2 * y + c
        own = pltpu.make_async_copy(small_ref, out_ref.at[me], lsem)
        own.start()
        sends = [pltpu.make_async_remote_copy(
            src_ref=small_ref, dst_ref=out_ref.at[me], send_sem=send.at[m], recv_sem=recv.at[m],
            device_id=peer, device_id_type=MESH) for m, peer in _peers7(x, y, c)]
        for cp in sends:
            cp.start()
        for m, (px, py, pc) in _peers7(x, y, c):
            pltpu.make_async_remote_copy(
                src_ref=small_ref, dst_ref=out_ref.at[4 * px + 2 * py + pc], send_sem=send.at[m], recv_sem=recv.at[m],
                device_id=(px, py, pc), device_id_type=MESH).wait_recv()
        for cp in sends:
            cp.wait_send()
        own.wait()

    return pl.pallas_call(
        body, name="exchange_small", in_specs=[ANY], out_specs=ANY,
        out_shape=jax.ShapeDtypeStruct((8,) + small.shape, small.dtype),
        scratch_shapes=[pltpu.SemaphoreType.DMA((7,)), pltpu.SemaphoreType.DMA((7,)), pltpu.SemaphoreType.DMA(())],
        compiler_params=_params(),
    )(small)


def _sum_slots(name, slots, tr=256):
    S, R, C = slots.shape
    tr = _tile(R, tr, 16)

    def body(s_ref, o_ref):
        acc = s_ref[0].astype(F32)
        for s in range(1, S):
            acc = acc + s_ref[s].astype(F32)
        o_ref[...] = acc

    return pl.pallas_call(
        body, name=name, grid=(R // tr,), in_specs=[pl.BlockSpec((S, tr, C), lambda i: (0, i, 0))],
        out_specs=pl.BlockSpec((tr, C), lambda i: (i, 0)), out_shape=jax.ShapeDtypeStruct((R, C), F32),
        compiler_params=_params(("parallel",)),
    )(slots)


def _place_shards(name, shards, axes):
    n = len(shards)

    def full_shape(k):
        R, C = shards[k].shape
        return (R, 4 * C) if axes[k] == 1 else (4 * R, C)

    def body(*refs):
        x, y, c = _me()
        copies = [pltpu.make_async_copy(refs[k], _shard_of(refs[n + k], axes[k], 2 * x + y), refs[2 * n].at[k])
                  for k in range(n)]
        for cp in copies:
            cp.start()
        for cp in copies:
            cp.wait()

    return pl.pallas_call(
        body, name=name, in_specs=[ANY] * n, out_specs=[ANY] * n,
        out_shape=[jax.ShapeDtypeStruct(full_shape(k), shards[k].dtype) for k in range(n)],
        scratch_shapes=[pltpu.SemaphoreType.DMA((n,))], compiler_params=_params(),
    )(*shards)


def _split_start(name, body, arrays, sem_shape, after=None):
    n = len(arrays)
    extra = [] if after is None else [after]

    def kernel_body(*refs):
        body(refs[:n], refs[n + len(extra)], refs[n + len(extra) + 1])
        token = refs[-1]
        token[...] = jnp.zeros_like(token)

    sems = pltpu.SemaphoreType.DMA(sem_shape)
    return pl.pallas_call(
        kernel_body, name=name,
        out_shape=(sems, sems, *[pltpu.HBM(a.shape, a.dtype) for a in arrays], jax.ShapeDtypeStruct(TOKEN, F32)),
        in_specs=(HBM,) * n + (ANY,) * len(extra),
        out_specs=(SEM, SEM) + (HBM,) * n + (pl.BlockSpec(memory_space=pltpu.VMEM),),
        input_output_aliases={k: 2 + k for k in range(n)}, compiler_params=pltpu.CompilerParams(has_side_effects=EFFECT),
    )(*[pltpu.with_memory_space_constraint(a, pltpu.HBM) for a in arrays], *extra)


def _split_wait(name, body, started, after):
    send, recv, *arrays = started[:-1]
    n = len(arrays)

    def kernel_body(*refs):
        body(refs[:n], refs[n], refs[n + 1])

    return pl.pallas_call(
        kernel_body, name=name, out_shape=tuple(pltpu.HBM(a.shape, a.dtype) for a in arrays),
        in_specs=(HBM,) * n + (SEM, SEM, ANY), out_specs=(HBM,) * n, input_output_aliases={k: k for k in range(n)},
        compiler_params=pltpu.CompilerParams(has_side_effects=EFFECT),
    )(*arrays, send, recv, after)


def _gather_copies(refs, send, recv, axes, landing):
    n = len(axes)
    x, y, c = _me()
    copies = []
    for k in range(n):
        for j, (px, py) in enumerate(_chips3(x, y)):
            s = 2 * px + py if landing else 2 * x + y
            copies.append(pltpu.make_async_remote_copy(
                src_ref=refs[k], dst_ref=_shard_of(refs[n + k], axes[k], s), send_sem=send.at[3 * k + j],
                recv_sem=recv.at[3 * k + j], device_id=(px, py, c), device_id_type=MESH))
    return copies


def _gather_start(name, shards, fulls, axes, after):
    def body(refs, send, recv):
        for cp in _gather_copies(refs, send, recv, axes, False):
            cp.start()

    return _split_start(name, body, list(shards) + list(fulls), (3 * len(axes),), after)


def _gather_wait(name, started, axes, after):
    def body(refs, send, recv):
        for cp in _gather_copies(refs, send, recv, axes, True):
            cp.wait_send()
            cp.wait_recv()

    return _split_wait(name, body, started, after)[len(axes):]


def _piece_shape(shape, axis):
    R, C = shape
    return (R // 2, C // 4) if axis == 1 else (R // 8, C)


def _scatter_copies(refs, send, recv, axes):
    n = len(axes)
    x, y, c = _me()
    return [pltpu.make_async_remote_copy(
        src_ref=_region(refs[k], axes[k], 2 * px + py, pc), dst_ref=refs[n + k].at[m], send_sem=send.at[7 * k + m],
        recv_sem=recv.at[7 * k + m], device_id=(px, py, pc), device_id_type=MESH)
        for k in range(n) for m, (px, py, pc) in _peers7(x, y, c)]


def _scatter_start(name, gs, axes):
    def body(refs, send, recv):
        for cp in _scatter_copies(refs, send, recv, axes):
            cp.start()

    lands = [lax.empty((7,) + _piece_shape(g.shape, ax), g.dtype) for g, ax in zip(gs, axes)]
    return _split_start(name, body, list(gs) + lands, (7 * len(axes),))


def _scatter_wait(name, started, axes, after):
    def body(refs, send, recv):
        for cp in _scatter_copies(refs, send, recv, axes):
            cp.wait_send()
            cp.wait_recv()

    out = _split_wait(name, body, started, after)
    return out[:len(axes)], out[len(axes):]


def _own_piece(g, axis):
    x, y, c = _me()
    pr, pc_ = _piece_shape(g.shape, axis)
    if axis == 1:
        return lax.dynamic_slice(g, (c * pr, (2 * x + y) * pc_), (pr, pc_))
    return lax.dynamic_slice(g, ((2 * x + y) * 2 * pr + c * pr, 0), (pr, pc_))


def _sum_pieces(name, own, slots, tr=256):
    S, R, C = slots.shape
    tr = _tile(R, tr, 16)
    nblk = R // tr
    c_arr = lax.axis_index("c").astype(jnp.int32).reshape(1)

    def body(c_ref, own_ref, s_ref, o_ref):
        acc = own_ref[...].astype(F32)
        for s in range(S):
            acc = acc + s_ref[s].astype(F32)
        o_ref[...] = acc

    grid_spec = pltpu.PrefetchScalarGridSpec(
        num_scalar_prefetch=1, grid=(nblk,),
        in_specs=[pl.BlockSpec((tr, C), lambda i, c_ref: (i, 0)), pl.BlockSpec((S, tr, C), lambda i, c_ref: (0, i, 0))],
        out_specs=pl.BlockSpec((tr, C), lambda i, c_ref: (c_ref[0] * nblk + i, 0)))
    return pl.pallas_call(
        body, name=name, grid_spec=grid_spec, out_shape=jax.ShapeDtypeStruct((2 * R, C), F32),
        compiler_params=_params(("parallel",)),
    )(c_arr, own, slots)


def _half_copies(refs, send, recv, mine):
    x, y, c = _me()
    copies = []
    for k, ref in enumerate(refs):
        hr = ref.shape[0] // 2
        rows = ref.at[pl.ds((c if mine else 1 - c) * hr, hr), :]
        copies.append(pltpu.make_async_remote_copy(
            src_ref=rows, dst_ref=rows, send_sem=send.at[k], recv_sem=recv.at[k], device_id=(x, y, 1 - c),
            device_id_type=MESH))
    return copies


def _join_start(name, bufs):
    def body(refs, send, recv):
        for cp in _half_copies(refs, send, recv, True):
            cp.start()

    return _split_start(name, body, list(bufs), (len(bufs),))


def _join_wait(name, started, after):
    def body(refs, send, recv):
        for cp in _half_copies(refs, send, recv, False):
            cp.wait_send()
            cp.wait_recv()

    return _split_wait(name, body, started, after)


def _adamw(name, w, g, m, v, tr=256, deps=()):
    R, C = w.shape
    tr = _tile(R, tr, 8)
    c1 = 1.0 - ADAM_B1 ** ADAM_STEP
    c2 = 1.0 - ADAM_B2 ** ADAM_STEP

    def body(w_ref, g_ref, m_ref, v_ref, *rest):
        d_ref, nm_ref, nv_ref, go_ref = rest[len(deps):]
        gv = g_ref[...]
        go_ref[...] = gv
        nm = ADAM_B1 * m_ref[...] + (1.0 - ADAM_B1) * gv
        nv = ADAM_B2 * v_ref[...] + (1.0 - ADAM_B2) * jnp.square(gv)
        d_ref[...] = -ADAM_LR * ((nm / c1) / (jnp.sqrt(nv / c2) + ADAM_EPS) + ADAM_WD * w_ref[...])
        nm_ref[...] = nm
        nv_ref[...] = nv

    spec = pl.BlockSpec((tr, C), lambda i: (i, 0))
    return pl.pallas_call(
        body, name=name, grid=(R // tr,), in_specs=[spec] * 4 + [pl.BlockSpec(TOKEN, lambda i: (0, 0))] * len(deps),
        out_specs=[spec] * 4, out_shape=[jax.ShapeDtypeStruct((R, C), F32)] * 4,
        compiler_params=_params(("parallel",)),
    )(w, g, m, v, *deps)


def kernel(x, p, g_mix, w_in, w_dw, b_dw, g_conv_ln, b_conv_ln, w_out, g_ffn, w_gate, w_up, w_down, g_ple, w_pgate, b_pgate, w_ple, g_final, loss_target, m_g_mix, m_w_in, m_w_dw, m_b_dw, m_g_conv_ln, m_b_conv_ln, m_w_out, m_g_ffn, m_w_gate, m_w_up, m_w_down, m_g_ple, m_w_pgate, m_b_pgate, m_w_ple, m_g_final, v_g_mix, v_w_in, v_w_dw, v_b_dw, v_g_conv_ln, v_b_conv_ln, v_w_out, v_g_ffn, v_w_gate, v_w_up, v_w_down, v_g_ple, v_w_pgate, v_b_pgate, v_w_ple, v_g_final):
    T, D = x.shape[1], x.shape[2]
    DC = b_dw.shape[1]
    DA = D - DC
    H = DA // HEAD_DIM
    DP = p.shape[3]
    assert T % SUPER == 0 and DA == DC
    x2 = x.reshape(T, D)
    p2 = p.reshape(T, DP)
    tgt = loss_target.reshape(T, D)
    g_final2 = g_final.reshape(1, D)

    big = dict(w_in=(w_in[0], 1), w_out=(w_out[0], 0), w_gate=(w_gate[0], 1), w_up=(w_up[0], 1),
               w_down=(w_down[0], 0), w_pgate=(w_pgate[0], 0), w_ple=(w_ple[0], 1))
    axis_of = {k: big[k][1] for k in big}
    dw_shard = jnp.pad(w_dw.reshape(CONV_W, -1), ((0, HALO - CONV_W), (0, 0)))
    w_in_full, w_dw_full = _gather_now(big["w_in"][0].astype(BF16), dw_shard)
    later = ["w_out", "w_gate", "w_up", "w_down", "w_pgate", "w_ple"]
    shard16 = {k: big[k][0].astype(BF16) for k in later}
    placed = dict(zip(later, _place_shards("place_shards", [shard16[k] for k in later], [axis_of[k] for k in later])))
    travelling = {}

    def fetch(groups, after):
        for keys in groups:
            travelling[keys] = _gather_start("gather_start_" + keys[0], [shard16[k] for k in keys],
                                             [placed[k] for k in keys], [axis_of[k] for k in keys], after)
        return [travelling[keys][-1] for keys in groups]

    def weights(keys, after):
        return _gather_wait("gather_wait_" + keys[0], travelling[keys], [axis_of[k] for k in keys], after)

    issued = fetch([("w_out",), ("w_gate", "w_up")], w_in_full)

    a = _rms_fwd("rms_mix", x2, g_mix, deps=issued)
    z = _matmul("mm_in", [[(a, w_in_full)]], "nn", [F32], _plain)[0]
    cat, o_attn, lse = _attn_fwd(z, D, H)
    cat, conv_c = _conv_fwd(z, cat, w_dw_full, b_dw, g_conv_ln, b_conv_ln, 3 * DA // DC)
    W = dict(w_in=w_in_full)
    W["w_out"], = weights(("w_out",), cat)
    def residual_and_norm(accs, extras):
        h = accs[0] + extras[0]
        r = lax.rsqrt(jnp.mean(h * h, axis=-1, keepdims=True) + EPS)
        return h, (h * r) * extras[1]

    h1, f = _matmul("mm_out", [[(cat, W["w_out"])]], "nn", [F32, BF16], residual_and_norm,
                    extras=[(x2, "mn"), (g_ffn, "n")], tm=512, tn=D, sub=256)
    W["w_gate"], W["w_up"] = weights(("w_gate", "w_up"), f)

    def swiglu(accs, extras):
        gt, up = accs
        return gt, up, (gt * jax.nn.sigmoid(gt)) * up

    gate, up, act = _matmul("mm_gate_up", [[(f, W["w_gate"])], [(f, W["w_up"])]], "nn", [BF16, BF16, BF16],
                            swiglu, tm=1024, tn=512, sub=256,
                            deps=fetch([("w_down",), ("w_pgate", "w_ple")], W["w_up"]))
    W["w_down"], = weights(("w_down",), act)
    h2 = _matmul("mm_down", [[(act, W["w_down"])]], "nn", [F32], _add_residual, extras=[(h1, "mn")], tm=512, tk=5632)[0]
    n2 = _rms_fwd("rms_ple", h2, g_ple)
    W["w_pgate"], W["w_ple"] = weights(("w_pgate", "w_ple"), n2)
    sent, joined = {}, {}

    def send_grads(keys, gs):
        sent[keys] = _scatter_start("scatter_start_" + keys[0], gs, [axis_of[k] for k in keys])
        return sent[keys][-1]

    def reduce_grads(groups, after):
        keys_all, halves = (), []
        for keys in groups:
            axes = [axis_of[k] for k in keys]
            g_thru, lands = _scatter_wait("scatter_wait_" + keys[0], sent[keys], axes, after)
            halves += [_sum_pieces("sum_" + k, _own_piece(g, ax), land)
                       for k, g, ax, land in zip(keys, g_thru, axes, lands)]
            keys_all += keys
        joined[keys_all] = _join_start("join_start_" + keys_all[0], halves)
        return joined[keys_all][-1]

    dh3, de, du4, dg_final, db_pgate, loss_part = _ple_loss_head(n2, W["w_pgate"], b_pgate, h2, p2, W["w_ple"], tgt,
                                                                 g_final2)
    t_ple = send_grads(("w_ple", "w_pgate"),
                       [_matmul("mm_dw_ple", [[(p2, de)]], "tn", [BF16], _plain, tk=1024)[0],
                        _matmul("mm_dw_pgate", [[(n2, du4)]], "tn", [BF16], _plain, tk=2048)[0]])
    dh2, dh2_16, dg_ple = _rms_bwd("rms_ple_bwd", du4, h2, g_ple, dh3, weight=W["w_pgate"])
    t_down = send_grads(("w_down",), [_matmul("mm_dw_down", [[(act, dh2_16)]], "tn", [BF16], _plain, tm=1408, tn=2048,
                                              tk=1024, deps=[t_ple])[0]])

    def swiglu_bwd(accs, extras):
        gt, up = extras[0].astype(F32), extras[1].astype(F32)
        sg = jax.nn.sigmoid(gt)
        dact = accs[0]
        return dact * up * (sg * (1.0 + gt * (1.0 - sg))), dact * (gt * sg)

    dgate, dup = _matmul("mm_dact", [[(dh2_16, W["w_down"])]], "nt", [BF16, BF16], swiglu_bwd,
                         extras=[(gate, "mn"), (up, "mn")], tm=2048, tn=512, sub=256, deps=[t_down])
    t_up = send_grads(("w_gate", "w_up"),
                      [_matmul("mm_dw_gate", [[(f, dgate)]], "tn", [BF16], _plain, tn=1408, tk=2048)[0],
                       _matmul("mm_dw_up", [[(f, dup)]], "tn", [BF16], _plain, tn=1408, tk=2048)[0]])
    df = _matmul("mm_df", [[(dgate, W["w_gate"]), (dup, W["w_up"])]], "nt", [F32], _plain, tm=1024, tn=256,
                 tk=5632, deps=[t_up], vmem=VMEM_LIMIT_LARGE)[0]
    dh1, dh1_16, dg_ffn = _rms_bwd("rms_ffn_bwd", df, h1, g_ffn, dh2)
    t_out = send_grads(("w_out",), [_matmul("mm_dw_out", [[(cat, dh1_16)]], "tn", [BF16], _plain, tk=2048)[0]])
    dcat = _matmul("mm_dcat", [[(dh1_16, W["w_out"])]], "nt", [F32], _plain, deps=[t_out])[0]
    j1 = reduce_grads([("w_ple", "w_pgate"), ("w_down",)], dcat)
    dc, dg_ln, db_ln, db_dw = _conv_ln_bwd(dcat, conv_c, g_conv_ln, b_conv_ln, deps=[j1])
    dcv, dcg, dw_dw = _conv_bwd(z, dc, w_dw_full, 3 * DA // DC)
    dq, dk, dv = _attn_bwd(z, dcat, o_attn, lse, H)
    dz = [dq, dk, dv, dcv, dcg]
    dw_in = jnp.concatenate([_matmul("mm_dw_in%d" % n, [[(a, part)]], "tn", [BF16], _plain, tk=2048)[0]
                             for n, part in enumerate(dz)], axis=1)
    t_in = send_grads(("w_in",), [dw_in])
    j2 = reduce_grads([("w_gate", "w_up"), ("w_out",)], dw_in)
    da = _matmul("mm_da", [[(part, W["w_in"], n) for n, part in enumerate(dz)]], "nt", [F32], _plain, tm=512,
                 tk=DA, deps=[t_in, j2])[0]
    grad_x, _, dg_mix = _rms_bwd("rms_mix_bwd", da, x2, g_mix, dh1)

    wide = [dg_mix, dg_ffn, dg_ple, db_pgate, dg_final,
            jnp.concatenate([db_dw, dg_ln], axis=1), jnp.concatenate([db_ln, jnp.zeros_like(db_ln)], axis=1),
            jnp.pad(loss_part, ((0, 0), (0, D - 128))),
            dw_dw.reshape(HALO * DC // D, D)]
    small = jnp.concatenate(wide, axis=0)
    small = jnp.pad(small, ((0, -small.shape[0] % 8), (0, 0)))
    small_sum = _sum_slots("sum_small", _exchange_small(small))
    j3 = reduce_grads([("w_in",)], small_sum)

    grads, deltas, new_m, new_v = {}, {}, {}, {}
    moments = dict(w_in=(m_w_in, v_w_in), w_out=(m_w_out, v_w_out), w_gate=(m_w_gate, v_w_gate),
                   w_up=(m_w_up, v_w_up), w_down=(m_w_down, v_w_down), w_pgate=(m_w_pgate, v_w_pgate),
                   w_ple=(m_w_ple, v_w_ple))
    last, deps = small_sum, [j3]
    for keys in list(joined):
        for k, g_k in zip(keys, _join_wait("join_wait_" + keys[0], joined[keys], last)):
            d_, m_, v_, g_ = _adamw("adamw_" + k, big[k][0], g_k, moments[k][0][0], moments[k][1][0], deps=deps)
            grads[k], deltas[k], new_m[k], new_v[k] = g_[None], d_[None], m_[None], v_[None]
            last, deps = d_, ()

    half = lambda r, lo: small_sum[r:r + 1, lo * DC:(lo + 1) * DC]
    vec = dict(g_mix=small_sum[0:1], g_ffn=small_sum[1:2], g_ple=small_sum[2:3], b_pgate=small_sum[3:4],
               g_final=small_sum[4:5], b_dw=half(5, 0), g_conv_ln=half(5, 1), b_conv_ln=half(6, 0))
    loss = small_sum[7, 0]
    dw_dw_sum = small_sum[8:8 + HALO * DC // D].reshape(HALO, DC)
    s_me = 2 * lax.axis_index("x") + lax.axis_index("y")
    cs = w_dw.shape[3]
    vec["w_dw"] = lax.dynamic_slice(dw_dw_sum, (0, s_me * cs), (CONV_W, cs))
    small_w = dict(g_mix=(g_mix, m_g_mix, v_g_mix), g_ffn=(g_ffn, m_g_ffn, v_g_ffn), g_ple=(g_ple, m_g_ple, v_g_ple),
                   b_pgate=(b_pgate, m_b_pgate, v_b_pgate), g_final=(g_final, m_g_final, v_g_final),
                   b_dw=(b_dw, m_b_dw, v_b_dw), g_conv_ln=(g_conv_ln, m_g_conv_ln, v_g_conv_ln),
                   b_conv_ln=(b_conv_ln, m_b_conv_ln, v_b_conv_ln), w_dw=(w_dw, m_w_dw, v_w_dw))
    for k, (w_, m_, v_) in small_w.items():
        shape = w_.shape
        g2 = vec[k]
        to2 = lambda t: t.reshape(g2.shape)
        d_, nm_, nv_, g_ = _adamw("adamw_" + k, to2(w_), g2, to2(m_), to2(v_))
        grads[k], deltas[k], new_m[k], new_v[k] = (t.reshape(shape) for t in (g_, d_, nm_, nv_))

    order = ["g_mix", "w_in", "w_dw", "b_dw", "g_conv_ln", "b_conv_ln", "w_out", "g_ffn", "w_gate", "w_up", "w_down",
             "g_ple", "w_pgate", "b_pgate", "w_ple", "g_final"]
    return (loss, grad_x.reshape(x.shape), *[grads[k] for k in order], *[deltas[k] for k in order],
            *[new_m[k] for k in order], *[new_v[k] for k in order])
```

```python
import functools

import jax
import jax.numpy as jnp
from jax import lax
from jax.experimental import pallas as pl
from jax.experimental.pallas import tpu as pltpu

F32 = jnp.float32
BF16 = jnp.bfloat16

EPS = 1e-6
HEAD_DIM = 128
BLK = 128
DILATIONS = (1, 4, 16)
SUPER = BLK * DILATIONS[-1]
CONV_W = 31
HALO = 32
ADAM_LR, ADAM_B1, ADAM_B2, ADAM_EPS, ADAM_WD, ADAM_STEP = 0.001, 0.9, 0.999, 1e-08, 0.01, 10

V7X_VMEM_BYTES = 64 * 1024 * 1024
VMEM_LIMIT = V7X_VMEM_BYTES * 3 // 4
VMEM_LIMIT_LARGE = V7X_VMEM_BYTES * 15 // 16
MESH = pl.DeviceIdType.MESH
ANY = pl.BlockSpec(memory_space=pl.ANY)
HBM = pl.BlockSpec(memory_space=pltpu.HBM)
SEM = pl.BlockSpec(memory_space=pltpu.SEMAPHORE)
EFFECT = pltpu.SideEffectType.DATAFLOW_SIDE_EFFECTING
TOKEN = (8, 128)


def _params(semantics=None, vmem=VMEM_LIMIT, **kw):
    return pltpu.CompilerParams(dimension_semantics=semantics, vmem_limit_bytes=vmem, **kw)


def _tile(n, want, mult=128):
    if n <= want:
        return n
    for t in range(want - want % mult, 0, -mult):
        if n % t == 0:
            return t
    raise ValueError((n, want, mult))


_DIMS = {"nn": ((1,), (0,)), "nt": ((1,), (1,)), "tn": ((0,), (0,))}


def _matmul(name, groups, mode, out_dtypes, epilogue, extras=(), tm=1024, tn=1024, tk=2048, sub=None, deps=(),
            vmem=VMEM_LIMIT):
    a0, b0 = groups[0][0][:2]
    if mode == "nn":
        (M, K), N = a0.shape, b0.shape[1]
    elif mode == "nt":
        (M, K), N = a0.shape, b0.shape[0]
    else:
        (K, M), N = a0.shape, b0.shape[1]
    tm, tn, tk = _tile(M, tm), _tile(N, tn), _tile(K, tk)
    nk = K // tk
    if mode == "tn":
        a_spec = pl.BlockSpec((tk, tm), lambda i, j, k: (k, i))
    else:
        a_spec = pl.BlockSpec((tm, tk), lambda i, j, k: (i, k))
    operands, in_specs = [], []
    for grp in groups:
        for a, b, *k0 in grp:
            k0 = k0[0] if k0 else 0
            if mode == "nt":
                b_spec = pl.BlockSpec((tn, tk), functools.partial(lambda i, j, k, k0: (j, k + k0), k0=k0))
            else:
                b_spec = pl.BlockSpec((tk, tn), functools.partial(lambda i, j, k, k0: (k + k0, j), k0=k0))
            operands += [a, b]
            in_specs += [a_spec, b_spec]
    for arr, kind in extras:
        operands.append(arr)
        if kind == "mn":
            in_specs.append(pl.BlockSpec((tm, tn), lambda i, j, k: (i, j)))
        else:
            in_specs.append(pl.BlockSpec((1, tn), lambda i, j, k: (0, j)))
    for tok in deps:
        operands.append(tok)
        in_specs.append(pl.BlockSpec(TOKEN, lambda i, j, k: (0, 0)))
    n_pairs = [len(g) for g in groups]
    n_ex, n_out, n_grp, n_dep = len(extras), len(out_dtypes), len(groups), len(deps)
    out_shape = [jax.ShapeDtypeStruct((M, N), dt) for dt in out_dtypes]
    out_specs = [pl.BlockSpec((tm, tn), lambda i, j, k: (i, j)) for _ in out_dtypes]

    kinds = [kind for _, kind in extras]
    sub = tm if (sub is None or nk > 1 or mode == "tn") else sub
    assert tm % sub == 0

    def body(*refs):
        ex_refs = refs[2 * sum(n_pairs):2 * sum(n_pairs) + n_ex]
        pos = 2 * sum(n_pairs) + n_ex + n_dep
        out_refs = refs[pos:pos + n_out]
        acc_refs = refs[pos + n_out:]

        def products(rows):
            pos, parts = 0, []
            for g in range(n_grp):
                part = None
                for _ in range(n_pairs[g]):
                    a_ref, b_ref = refs[pos], refs[pos + 1]
                    pos += 2
                    d = lax.dot_general(a_ref[rows].astype(BF16), b_ref[...].astype(BF16),
                                        (_DIMS[mode], ((), ())), preferred_element_type=F32)
                    part = d if part is None else part + d
                parts.append(part)
            return parts

        def finish(accs, rows):
            outs = epilogue(accs, [e[rows] if kind == "mn" else e[...] for e, kind in zip(ex_refs, kinds)])
            for o_ref, o in zip(out_refs, outs):
                o_ref[rows] = o.astype(o_ref.dtype)

        if nk == 1:
            for r in range(tm // sub):
                rows = (pl.ds(r * sub, sub), slice(None)) if sub < tm else (slice(None), slice(None))
                finish(products(rows), rows)
        else:
            k = pl.program_id(2)
            parts = products((slice(None), slice(None)))

            @pl.when(k == 0)
            def _():
                for acc, part in zip(acc_refs, parts):
                    acc[...] = part

            @pl.when(k > 0)
            def _():
                for acc, part in zip(acc_refs, parts):
                    acc[...] += part

            @pl.when(k == nk - 1)
            def _():
                finish([acc[...] for acc in acc_refs], (slice(None), slice(None)))

    scratch = [pltpu.VMEM((tm, tn), F32) for _ in range(n_grp)] if nk > 1 else []
    return pl.pallas_call(
        body, name=name, grid=(M // tm, N // tn, nk),
        in_specs=in_specs, out_specs=out_specs, out_shape=out_shape, scratch_shapes=scratch,
        compiler_params=_params(("parallel", "parallel", "arbitrary"), vmem),
    )(*operands)


def _plain(accs, extras):
    return (accs[0],)


def _add_residual(accs, extras):
    return (accs[0] + extras[0],)


def _rowwise(name, fn, rows, consts, row_outs, acc_outs, tb, deps=()):
    T = rows[0][0].shape[0]
    tb = _tile(T, tb, 16)
    operands = [r[0] for r in rows] + list(consts) + list(deps)
    in_specs = [pl.BlockSpec((tb, c), functools.partial(lambda i, cb: (i, cb), cb=cb)) for _, c, cb in rows]
    in_specs += [pl.BlockSpec(c.shape, functools.partial(lambda i, nd: (0,) * nd, nd=c.ndim)) for c in consts]
    in_specs += [pl.BlockSpec(TOKEN, lambda i: (0, 0)) for _ in deps]
    out_shape = [jax.ShapeDtypeStruct(s, dt) for s, dt, _, _ in row_outs]
    out_specs = [pl.BlockSpec((tb, c), functools.partial(lambda i, cb: (i, cb), cb=cb)) for _, _, c, cb in row_outs]
    out_shape += [jax.ShapeDtypeStruct(s, F32) for s in acc_outs]
    out_specs += [pl.BlockSpec(s, functools.partial(lambda i, nd: (0,) * nd, nd=len(s))) for s in acc_outs]
    n_rows, n_consts, n_ro, n_dep = len(rows), len(consts), len(row_outs), len(deps)

    def body(*refs):
        row_refs = refs[:n_rows]
        const_refs = refs[n_rows:n_rows + n_consts]
        out_refs = refs[n_rows + n_consts + n_dep:]
        ro, ao = fn([r[...] for r in row_refs], [c[...] for c in const_refs])
        for o_ref, o in zip(out_refs[:n_ro], ro):
            o_ref[...] = o.astype(o_ref.dtype)
        if acc_outs:
            i = pl.program_id(0)

            @pl.when(i == 0)
            def _():
                for a_ref, a in zip(out_refs[n_ro:], ao):
                    a_ref[...] = a

            @pl.when(i > 0)
            def _():
                for a_ref, a in zip(out_refs[n_ro:], ao):
                    a_ref[...] += a

    return pl.pallas_call(
        body, name=name, grid=(T // tb,), in_specs=in_specs, out_specs=out_specs, out_shape=out_shape,
        compiler_params=_params(("arbitrary",) if acc_outs else ("parallel",)),
    )(*operands)


def _colsum(v):
    return jnp.sum(v, axis=0, keepdims=True)


def _rms_fwd(name, x, g, tb=512, deps=()):
    T, D = x.shape

    def fn(rows, consts):
        xv, gv = rows[0], consts[0]
        r = lax.rsqrt(jnp.mean(xv * xv, axis=-1, keepdims=True) + EPS)
        return ((xv * r) * gv,), ()

    return _rowwise(name, fn, [(x, D, 0)], [g], [((T, D), BF16, D, 0)], [], tb, deps=deps)[0]


def _rms_bwd(name, dy, x, g, resid, tb=256, weight=None):
    T, D = x.shape
    consts = [g] if weight is None else [g, weight]

    def fn(rows, consts):
        dyv, xv, rv = rows
        gv = consts[0]
        if weight is not None:
            dyv = lax.dot_general(dyv, consts[1], (((1,), (1,)), ((), ())), preferred_element_type=F32)
        r = lax.rsqrt(jnp.mean(xv * xv, axis=-1, keepdims=True) + EPS)
        n = xv * r
        dn = dyv * gv
        dx = r * (dn - n * jnp.mean(dn * n, axis=-1, keepdims=True))
        tot = rv + dx
        return (tot, tot), (_colsum(dyv * n),)

    return _rowwise(name, fn, [(dy, dy.shape[1], 0), (x, D, 0), (resid, D, 0)], consts,
                    [((T, D), F32, D, 0), ((T, D), BF16, D, 0)], [(1, D)], tb)


NBLK = SUPER // BLK


def _classes(ref, dil, rows, start=0, dtype=None):
    parts = [ref[pl.ds(start + r, rows, stride=dil), :] if dil > 1 else ref[pl.ds(start, rows), :]
             for r in range(dil)]
    if dtype is not None:
        parts = [p.astype(dtype) for p in parts]
    return parts


def _keys_with_prev(ref, prev_ref, dil):
    L = SUPER // dil
    own = _classes(ref, dil, L, dtype=BF16)
    last = _classes(prev_ref, dil, BLK, start=SUPER - BLK * dil, dtype=BF16)
    blocks = []
    for r in range(dil):
        ext = jnp.concatenate([last[r], own[r]], axis=0)
        blocks += [ext[j * BLK:(j + 2) * BLK] for j in range(L // BLK)]
    return jnp.stack(blocks, axis=0)


def _band_mask(dil, has_prev):
    qi = lax.broadcasted_iota(jnp.int32, (BLK, 2 * BLK), 0)
    kj = lax.broadcasted_iota(jnp.int32, (BLK, 2 * BLK), 1)
    own = jnp.logical_and(kj >= BLK, kj - BLK <= qi)
    prev = jnp.logical_and(kj < BLK, kj >= qi)
    b = lax.broadcasted_iota(jnp.int32, (NBLK, 1, 1), 0)
    first = (b & (SUPER // (BLK * dil) - 1)) == 0
    prev_ok = jnp.logical_or(jnp.logical_not(first), has_prev)
    return jnp.logical_or(own[None], jnp.logical_and(prev[None], prev_ok))


def _bdot(a, b, ca, cb):
    return lax.dot_general(a, b, (((ca,), (cb,)), ((0,), (0,))), preferred_element_type=F32)


def _put_classes(dst, value, dil, rows, start=0, add=False, src_start=0, src_stride=None):
    src_stride = rows if src_stride is None else src_stride
    for r in range(dil):
        idx = (pl.ds(start + r, rows, stride=dil) if dil > 1 else pl.ds(start, rows), slice(None))
        part = value[src_start + r * src_stride:src_start + r * src_stride + rows]
        dst[idx] = dst[idx] + part if add else part


def _attn_fwd(z, cat_width, n_heads):
    T = z.shape[0]
    H = n_heads
    DA = H * HEAD_DIM
    nb = T // SUPER
    scale = HEAD_DIM ** -0.5

    def body(q_ref, k_ref, v_ref, kp_ref, vp_ref, cat_ref, o_ref, lse_ref, ob, lb):
        has_prev = pl.program_id(1) > 0
        for b, dil in enumerate(DILATIONS):
            L = SUPER // dil
            q3 = jnp.concatenate(_classes(q_ref, dil, L, dtype=BF16), axis=0).reshape(NBLK, BLK, HEAD_DIM)
            k3 = _keys_with_prev(k_ref, kp_ref, dil)
            v3 = _keys_with_prev(v_ref, vp_ref, dil)
            s = jnp.where(_band_mask(dil, has_prev), _bdot(q3, k3, 2, 2) * scale, -jnp.inf)
            m = jnp.max(s, axis=-1, keepdims=True)
            e = jnp.exp(s - m)
            den = jnp.sum(e, axis=-1, keepdims=True)
            o3 = _bdot((e * (1.0 / den)).astype(BF16), v3, 2, 1)
            lse3 = jnp.broadcast_to(m + jnp.log(den), (NBLK, BLK, HEAD_DIM))
            _put_classes(ob.at[b], o3.reshape(SUPER, HEAD_DIM), dil, L)
            _put_classes(lb.at[b], lse3.reshape(SUPER, HEAD_DIM), dil, L)
        l0, l1, l2 = lb[0], lb[1], lb[2]
        mx = jnp.maximum(jnp.maximum(l0, l1), l2)
        tot = mx + jnp.log(jnp.exp(l0 - mx) + jnp.exp(l1 - mx) + jnp.exp(l2 - mx))
        o = jnp.exp(l0 - tot) * ob[0] + jnp.exp(l1 - tot) * ob[1] + jnp.exp(l2 - tot) * ob[2]
        o_ref[...] = o
        cat_ref[...] = o.astype(BF16)
        lse_ref[...] = tot

    blk = (SUPER, HEAD_DIM)
    in_specs = [
        pl.BlockSpec(blk, lambda h, n: (n, h)),
        pl.BlockSpec(blk, lambda h, n: (n, H + h)),
        pl.BlockSpec(blk, lambda h, n: (n, 2 * H + h)),
        pl.BlockSpec(blk, lambda h, n: (jnp.maximum(n - 1, 0), H + h)),
        pl.BlockSpec(blk, lambda h, n: (jnp.maximum(n - 1, 0), 2 * H + h)),
    ]
    out_spec = pl.BlockSpec(blk, lambda h, n: (n, h))
    return pl.pallas_call(
        body, name="attn_fwd", grid=(H, nb), in_specs=in_specs, out_specs=[out_spec] * 3,
        out_shape=[jax.ShapeDtypeStruct((T, cat_width), BF16), jax.ShapeDtypeStruct((T, DA), F32),
                   jax.ShapeDtypeStruct((T, DA), F32)],
        scratch_shapes=[pltpu.VMEM((3, SUPER, HEAD_DIM), F32), pltpu.VMEM((3, SUPER, HEAD_DIM), F32)],
        compiler_params=_params(("parallel", "parallel")),
    )(z, z, z, z, z)


def _attn_bwd(z, dcat, o, lse, n_heads):
    T = z.shape[0]
    H = n_heads
    nb = T // SUPER
    scale = HEAD_DIM ** -0.5

    def body(q_ref, k_ref, v_ref, kp_ref, vp_ref, do_ref, o_ref, lse_ref, dz_q, dz_k, dz_v,
             dq_acc, dk_acc, dv_acc, dkp_acc, dvp_acc, dsum):
        i = pl.program_id(1)
        has_prev = i < nb - 1

        @pl.when(i == 0)
        def _():
            dk_acc[...] = jnp.zeros_like(dk_acc)
            dv_acc[...] = jnp.zeros_like(dv_acc)

        @pl.when(i > 0)
        def _():
            dk_acc[...] = dkp_acc[...]
            dv_acc[...] = dvp_acc[...]

        dq_acc[...] = jnp.zeros_like(dq_acc)
        dkp_acc[...] = jnp.zeros_like(dkp_acc)
        dvp_acc[...] = jnp.zeros_like(dvp_acc)
        dsum[...] = jnp.broadcast_to(jnp.sum(do_ref[...] * o_ref[...], axis=-1, keepdims=True), (SUPER, HEAD_DIM))
        for dil in DILATIONS:
            L = SUPER // dil
            wide = lambda ref: jnp.tile(jnp.concatenate(_classes(ref, dil, L), axis=0).reshape(NBLK, BLK, HEAD_DIM),
                                        (1, 1, 2))
            q3 = jnp.concatenate(_classes(q_ref, dil, L, dtype=BF16), axis=0).reshape(NBLK, BLK, HEAD_DIM)
            do3 = jnp.concatenate(_classes(do_ref, dil, L, dtype=BF16), axis=0).reshape(NBLK, BLK, HEAD_DIM)
            k3 = _keys_with_prev(k_ref, kp_ref, dil)
            v3 = _keys_with_prev(v_ref, vp_ref, dil)
            p = jnp.where(_band_mask(dil, has_prev), jnp.exp(_bdot(q3, k3, 2, 2) * scale - wide(lse_ref)), 0.0)
            ds = (p * (_bdot(do3, v3, 2, 2) - wide(dsum)) * scale).astype(BF16)
            dq = _bdot(ds, k3, 2, 1).reshape(SUPER, HEAD_DIM)
            dk = _bdot(ds, q3, 1, 1)
            dv = _bdot(p.astype(BF16), do3, 1, 1)
            _put_classes(dq_acc, dq, dil, L, add=True)
            for acc, prev_acc, g in ((dk_acc, dkp_acc, dk), (dv_acc, dvp_acc, dv)):
                _put_classes(acc, g[:, BLK:, :].reshape(SUPER, HEAD_DIM), dil, L, add=True)
                to_prev = g[:, :BLK, :].reshape(SUPER, HEAD_DIM)
                if L > BLK:
                    _put_classes(acc, to_prev, dil, L - BLK, add=True, src_start=BLK, src_stride=L)
                _put_classes(prev_acc, to_prev, dil, BLK, start=SUPER - BLK * dil, add=True, src_stride=L)
        dz_q[...] = dq_acc[...].astype(BF16)
        dz_k[...] = dk_acc[...].astype(BF16)
        dz_v[...] = dv_acc[...].astype(BF16)

    blk = (SUPER, HEAD_DIM)
    row = lambda i: nb - 1 - i
    in_specs = [
        pl.BlockSpec(blk, lambda h, i: (row(i), h)),
        pl.BlockSpec(blk, lambda h, i: (row(i), H + h)),
        pl.BlockSpec(blk, lambda h, i: (row(i), 2 * H + h)),
        pl.BlockSpec(blk, lambda h, i: (jnp.maximum(row(i) - 1, 0), H + h)),
        pl.BlockSpec(blk, lambda h, i: (jnp.maximum(row(i) - 1, 0), 2 * H + h)),
        pl.BlockSpec(blk, lambda h, i: (row(i), h)),
        pl.BlockSpec(blk, lambda h, i: (row(i), h)),
        pl.BlockSpec(blk, lambda h, i: (row(i), h)),
    ]
    out_spec = pl.BlockSpec(blk, lambda h, i: (row(i), h))
    return pl.pallas_call(
        body, name="attn_bwd", grid=(H, nb), in_specs=in_specs, out_specs=[out_spec] * 3,
        out_shape=[jax.ShapeDtypeStruct((T, H * HEAD_DIM), BF16)] * 3,
        scratch_shapes=[pltpu.VMEM(blk, F32) for _ in range(6)],
        compiler_params=_params(("parallel", "arbitrary")),
    )(z, z, z, z, z, dcat, o, lse)


def _glu(cv, cg):
    return cv * jax.nn.sigmoid(cg)


SUBLANES = 8


def _fill_shifted(sh, ext, rows):
    for b in range(1, SUBLANES):
        sh[b - 1, pl.ds(0, rows - SUBLANES), :] = ext[pl.ds(b, rows - SUBLANES), :]


def _window(sh, ext, offset, rows):
    b = offset % SUBLANES
    if b == 0:
        return ext[pl.ds(offset, rows), :]
    return sh[b - 1, pl.ds(offset - b, rows), :]


def _conv_fwd(z, cat, w_dw, b_dw, g_ln, b_ln, col0, tb=512):
    T = z.shape[0]
    DC = w_dw.shape[1]
    tb = _tile(T, tb, HALO)
    hb = tb // HALO
    cat_cb = cat.shape[1] // DC - 1

    def body(cv_ref, cg_ref, cvh_ref, cgh_ref, w_ref, bdw_ref, g_ref, b_ref, cat_in, cat_ref, c_ref, u_ext, sh):
        i = pl.program_id(0)
        halo = _glu(cvh_ref[...], cgh_ref[...])
        u_ext[pl.ds(0, HALO), :] = jnp.where(i > 0, halo, 0.0)
        u_ext[pl.ds(HALO, tb), :] = _glu(cv_ref[...], cg_ref[...])
        _fill_shifted(sh, u_ext, tb + HALO)
        acc = jnp.broadcast_to(bdw_ref[...], (tb, DC))
        for j in range(CONV_W):
            acc = acc + w_ref[pl.ds(j, 1), :] * _window(sh, u_ext, HALO - (CONV_W - 1) + j, tb)
        c_ref[...] = acc
        mu = jnp.mean(acc, axis=-1, keepdims=True)
        var = jnp.mean(jnp.square(acc - mu), axis=-1, keepdims=True)
        y = (acc - mu) * lax.rsqrt(var + EPS) * g_ref[...] + b_ref[...]
        cat_ref[...] = (y * jax.nn.sigmoid(y)).astype(BF16)

    cur = lambda cb: pl.BlockSpec((tb, DC), lambda i: (i, cb))
    halo = lambda cb: pl.BlockSpec((HALO, DC), lambda i: (jnp.maximum(i * hb - 1, 0), cb))
    whole = lambda a: pl.BlockSpec(a.shape, lambda i: (0, 0))
    return pl.pallas_call(
        body, name="conv_fwd", grid=(T // tb,),
        in_specs=[cur(col0), cur(col0 + 1), halo(col0), halo(col0 + 1), whole(w_dw), whole(b_dw), whole(g_ln),
                  whole(b_ln), ANY],
        out_specs=[pl.BlockSpec((tb, DC), lambda i: (i, cat_cb)), pl.BlockSpec((tb, DC), lambda i: (i, 0))],
        out_shape=[jax.ShapeDtypeStruct(cat.shape, cat.dtype), jax.ShapeDtypeStruct((T, DC), F32)],
        scratch_shapes=[pltpu.VMEM((tb + HALO, DC), F32), pltpu.VMEM((SUBLANES - 1, tb + HALO, DC), F32)],
        input_output_aliases={8: 0},
        compiler_params=_params(("parallel",)),
    )(z, z, z, z, w_dw, b_dw, g_ln, b_ln, cat)


def _conv_ln_bwd(dcat, c, g_ln, b_ln, tb=256, deps=()):
    T, DC = c.shape
    d_cb = dcat.shape[1] // DC - 1

    def fn(rows, consts):
        dov, cv_ = rows
        gv, bv = consts
        mu = jnp.mean(cv_, axis=-1, keepdims=True)
        xc = cv_ - mu
        rstd = lax.rsqrt(jnp.mean(jnp.square(xc), axis=-1, keepdims=True) + EPS)
        ln = xc * rstd
        y = ln * gv + bv
        sg = jax.nn.sigmoid(y)
        dy = dov * (sg * (1.0 + y * (1.0 - sg)))
        dln = dy * gv
        dc = rstd * (dln - jnp.mean(dln, axis=-1, keepdims=True) - ln * jnp.mean(dln * ln, axis=-1, keepdims=True))
        return (dc,), (_colsum(dy * ln), _colsum(dy), _colsum(dc))

    return _rowwise("conv_ln_bwd", fn, [(dcat, DC, d_cb), (c, DC, 0)], [g_ln, b_ln],
                    [((T, DC), F32, DC, 0)], [(1, DC)] * 3, tb, deps=deps)


def _conv_bwd(z, dc, w_dw, col0, tb=512):
    T, DC = dc.shape
    tb = _tile(T, tb, HALO)
    hb = tb // HALO
    nblk = T // tb

    def body(cv_ref, cg_ref, dc_ref, dcn_ref, w_ref, dcv_ref, dcg_ref, dw_ref, dc_ext, sh):
        i = pl.program_id(0)
        cv, cg = cv_ref[...], cg_ref[...]
        sg = jax.nn.sigmoid(cg)
        u = cv * sg
        dc_ext[pl.ds(0, tb), :] = dc_ref[...]
        dc_ext[pl.ds(tb, HALO), :] = jnp.where(i < nblk - 1, dcn_ref[...], 0.0)
        _fill_shifted(sh, dc_ext, tb + HALO)

        @pl.when(i == 0)
        def _():
            dw_ref[...] = jnp.zeros_like(dw_ref)

        du = jnp.zeros((tb, DC), F32)
        for j in range(CONV_W):
            d_j = _window(sh, dc_ext, CONV_W - 1 - j, tb)
            du = du + w_ref[pl.ds(j, 1), :] * d_j
            dw_ref[pl.ds(j, 1), :] += _colsum(u * d_j)
        dcv_ref[...] = (du * sg).astype(BF16)
        dcg_ref[...] = (du * cv * sg * (1.0 - sg)).astype(BF16)

    cur = lambda cb: pl.BlockSpec((tb, DC), lambda i: (i, cb))
    nxt = pl.BlockSpec((HALO, DC), lambda i: (jnp.minimum((i + 1) * hb, T // HALO - 1), 0))
    return pl.pallas_call(
        body, name="conv_bwd", grid=(nblk,),
        in_specs=[cur(col0), cur(col0 + 1), cur(0), nxt, pl.BlockSpec(w_dw.shape, lambda i: (0, 0))],
        out_specs=[cur(0), cur(0), pl.BlockSpec((HALO, DC), lambda i: (0, 0))],
        out_shape=[jax.ShapeDtypeStruct((T, DC), BF16), jax.ShapeDtypeStruct((T, DC), BF16),
                   jax.ShapeDtypeStruct((HALO, DC), F32)],
        scratch_shapes=[pltpu.VMEM((tb + HALO, DC), F32), pltpu.VMEM((SUBLANES - 1, tb + HALO, DC), F32)],
        compiler_params=_params(("arbitrary",)),
    )(z, z, dc, dc, w_dw)


def _ple_loss_head(n2, w_pgate, b_pgate, h2, p, w_ple, target, g_final, tb=256):
    T, D = h2.shape

    def fn(rows, consts):
        n2v, h2v, pv, tv = rows
        wp, bp, wple, gv = consts
        gt = jax.nn.sigmoid(lax.dot_general(n2v, wp, (((1,), (0,)), ((), ())), preferred_element_type=F32) + bp)
        ev = lax.dot_general(pv.astype(BF16), wple, (((1,), (0,)), ((), ())), preferred_element_type=F32)
        hv = h2v + ev * gt
        r = lax.rsqrt(jnp.mean(hv * hv, axis=-1, keepdims=True) + EPS)
        n = hv * r
        diff = n * gv - tv
        loss = 0.5 * jnp.sum(jnp.mean(jnp.square(diff), axis=-1, keepdims=True), axis=0, keepdims=True)
        dy = diff * (1.0 / D)
        dn = dy * gv
        dh = r * (dn - n * jnp.mean(dn * n, axis=-1, keepdims=True))
        du4 = dh * ev * gt * (1.0 - gt)
        return (dh, dh * gt, du4), (_colsum(dy * n), _colsum(du4), jnp.broadcast_to(loss, (1, 128)))

    return _rowwise("ple_loss_head", fn, [(n2, D, 0), (h2, D, 0), (p, p.shape[1], 0), (target, D, 0)],
                    [w_pgate, b_pgate, w_ple, g_final],
                    [((T, D), F32, D, 0), ((T, D), BF16, D, 0), ((T, D), BF16, D, 0)],
                    [(1, D), (1, D), (1, 128)], tb)


def _me():
    return lax.axis_index("x"), lax.axis_index("y"), lax.axis_index("c")


def _chips3(x, y):
    return [(1 - x, y), (x, 1 - y), (1 - x, 1 - y)]


def _peers7(x, y, c):
    for m in range(1, 8):
        yield m - 1, (x ^ (m >> 2), y ^ ((m >> 1) & 1), c ^ (m & 1))


def _shard_of(ref, axis, s):
    R, C = ref.shape
    if axis == 1:
        return ref.at[:, pl.ds(s * (C // 4), C // 4)]
    return ref.at[pl.ds(s * (R // 4), R // 4), :]


def _region(ref, axis, shard, half):
    R, C = ref.shape
    if axis == 1:
        cs, hr = C // 4, R // 2
        return ref.at[pl.ds(half * hr, hr), pl.ds(shard * cs, cs)]
    hr = R // 8
    return ref.at[pl.ds(shard * 2 * hr + half * hr, hr), :]


def _gather_now(shard, small):
    R, C = shard.shape
    cs = small.shape[1]

    def body(shard_ref, small_ref, out_ref, small_out, send, recv, fsend, frecv, lsem):
        x, y, c = _me()
        me_s = 2 * x + y
        sibling = (x, y, 1 - c)
        chips = _chips3(x, y)
        half = shard_ref.at[pl.ds(c * (R // 2), R // 2), :]
        locals_ = [pltpu.make_async_copy(shard_ref, _shard_of(out_ref, 1, me_s), lsem.at[0]),
                   pltpu.make_async_copy(small_ref, _shard_of(small_out, 1, me_s), lsem.at[1])]
        for cp in locals_:
            cp.start()
        firsts = []
        for j, (px, py) in enumerate(chips):
            firsts.append(pltpu.make_async_remote_copy(
                src_ref=half, dst_ref=_region(out_ref, 1, me_s, c), send_sem=send.at[0, j], recv_sem=recv.at[0, j],
                device_id=(px, py, c), device_id_type=MESH))
            firsts.append(pltpu.make_async_remote_copy(
                src_ref=small_ref, dst_ref=_shard_of(small_out, 1, me_s), send_sem=send.at[1, j],
                recv_sem=recv.at[1, j], device_id=(px, py, c), device_id_type=MESH))
        for cp in firsts:
            cp.start()
        relays = []
        for j, (px, py) in enumerate(chips):
            landed = _region(out_ref, 1, 2 * px + py, c)
            pltpu.make_async_remote_copy(
                src_ref=half, dst_ref=landed, send_sem=send.at[0, j], recv_sem=recv.at[0, j],
                device_id=(px, py, c), device_id_type=MESH).wait_recv()
            relay = pltpu.make_async_remote_copy(
                src_ref=landed, dst_ref=landed, send_sem=fsend.at[j], recv_sem=frecv.at[j],
                device_id=sibling, device_id_type=MESH)
            relay.start()
            relays.append(relay)
        for j, (px, py) in enumerate(chips):
            pltpu.make_async_remote_copy(
                src_ref=small_ref, dst_ref=_shard_of(small_out, 1, 2 * px + py), send_sem=send.at[1, j],
                recv_sem=recv.at[1, j], device_id=(px, py, c), device_id_type=MESH).wait_recv()
            theirs = _region(out_ref, 1, 2 * px + py, 1 - c)
            pltpu.make_async_remote_copy(
                src_ref=theirs, dst_ref=theirs, send_sem=fsend.at[j], recv_sem=frecv.at[j],
                device_id=sibling, device_id_type=MESH).wait_recv()
        for cp in firsts + relays:
            cp.wait_send()
        for cp in locals_:
            cp.wait()

    return pl.pallas_call(
        body, name="gather_now", in_specs=[ANY, ANY], out_specs=[ANY, ANY],
        out_shape=[jax.ShapeDtypeStruct((R, 4 * C), shard.dtype),
                   jax.ShapeDtypeStruct((small.shape[0], 4 * cs), small.dtype)],
        scratch_shapes=[pltpu.SemaphoreType.DMA((2, 3)), pltpu.SemaphoreType.DMA((2, 3)),
                        pltpu.SemaphoreType.DMA((3,)), pltpu.SemaphoreType.DMA((3,)), pltpu.SemaphoreType.DMA((2,))],
        compiler_params=_params(),
    )(shard, small)


def _exchange_small(small):
    def body(small_ref, out_ref, send, recv, lsem):
        x, y, c = _me()
        me = 4 * x + 2 * y + c
        own = pltpu.make_async_copy(small_ref, out_ref.at[me], lsem)
        own.start()
        sends = [pltpu.make_async_remote_copy(
            src_ref=small_ref, dst_ref=out_ref.at[me], send_sem=send.at[m], recv_sem=recv.at[m],
            device_id=peer, device_id_type=MESH) for m, peer in _peers7(x, y, c)]
        for cp in sends:
            cp.start()
        for m, (px, py, pc) in _peers7(x, y, c):
            pltpu.make_async_remote_copy(
                src_ref=small_ref, dst_ref=out_ref.at[4 * px + 2 * py + pc], send_sem=send.at[m], recv_sem=recv.at[m],
                device_id=(px, py, pc), device_id_type=MESH).wait_recv()
        for cp in sends:
            cp.wait_send()
        own.wait()

    return pl.pallas_call(
        body, name="exchange_small", in_specs=[ANY], out_specs=ANY,
        out_shape=jax.ShapeDtypeStruct((8,) + small.shape, small.dtype),
        scratch_shapes=[pltpu.SemaphoreType.DMA((7,)), pltpu.SemaphoreType.DMA((7,)), pltpu.SemaphoreType.DMA(())],
        compiler_params=_params(),
    )(small)


def _sum_slots(name, slots, tr=256):
    S, R, C = slots.shape
    tr = _tile(R, tr, 16)

    def body(s_ref, o_ref):
        acc = s_ref[0].astype(F32)
        for s in range(1, S):
            acc = acc + s_ref[s].astype(F32)
        o_ref[...] = acc

    return pl.pallas_call(
        body, name=name, grid=(R // tr,), in_specs=[pl.BlockSpec((S, tr, C), lambda i: (0, i, 0))],
        out_specs=pl.BlockSpec((tr, C), lambda i: (i, 0)), out_shape=jax.ShapeDtypeStruct((R, C), F32),
        compiler_params=_params(("parallel",)),
    )(slots)


def _place_shard(name, shard, axis):
    R, C = shard.shape
    full_shape = (R, 4 * C) if axis == 1 else (4 * R, C)

    def body(shard_ref, full_ref, sem):
        x, y, c = _me()
        cp = pltpu.make_async_copy(shard_ref, _shard_of(full_ref, axis, 2 * x + y), sem)
        cp.start()
        cp.wait()

    return pl.pallas_call(
        body, name=name, in_specs=[ANY], out_specs=ANY, out_shape=jax.ShapeDtypeStruct(full_shape, shard.dtype),
        scratch_shapes=[pltpu.SemaphoreType.DMA(())], compiler_params=_params(),
    )(shard)


def _split_start(name, body, arrays, sem_shape, after=None):
    n = len(arrays)
    extra = [] if after is None else [after]

    def kernel_body(*refs):
        body(refs[:n], refs[n + len(extra)], refs[n + len(extra) + 1])
        token = refs[-1]
        token[...] = jnp.zeros_like(token)

    sems = pltpu.SemaphoreType.DMA(sem_shape)
    return pl.pallas_call(
        kernel_body, name=name,
        out_shape=(sems, sems, *[pltpu.HBM(a.shape, a.dtype) for a in arrays], jax.ShapeDtypeStruct(TOKEN, F32)),
        in_specs=(HBM,) * n + (ANY,) * len(extra),
        out_specs=(SEM, SEM) + (HBM,) * n + (pl.BlockSpec(memory_space=pltpu.VMEM),),
        input_output_aliases={k: 2 + k for k in range(n)}, compiler_params=pltpu.CompilerParams(has_side_effects=EFFECT),
    )(*[pltpu.with_memory_space_constraint(a, pltpu.HBM) for a in arrays], *extra)


def _split_wait(name, body, started, after):
    send, recv, *arrays = started[:-1]
    n = len(arrays)

    def kernel_body(*refs):
        body(refs[:n], refs[n], refs[n + 1])

    return pl.pallas_call(
        kernel_body, name=name, out_shape=tuple(pltpu.HBM(a.shape, a.dtype) for a in arrays),
        in_specs=(HBM,) * n + (SEM, SEM, ANY), out_specs=(HBM,) * n, input_output_aliases={k: k for k in range(n)},
        compiler_params=pltpu.CompilerParams(has_side_effects=EFFECT),
    )(*arrays, send, recv, after)


def _gather_copies(refs, send, recv, axes, landing):
    n = len(axes)
    x, y, c = _me()
    copies = []
    for k in range(n):
        for j, (px, py) in enumerate(_chips3(x, y)):
            s = 2 * px + py if landing else 2 * x + y
            copies.append(pltpu.make_async_remote_copy(
                src_ref=refs[k], dst_ref=_shard_of(refs[n + k], axes[k], s), send_sem=send.at[3 * k + j],
                recv_sem=recv.at[3 * k + j], device_id=(px, py, c), device_id_type=MESH))
    return copies


def _gather_start(name, shards, fulls, axes, after):
    def body(refs, send, recv):
        for cp in _gather_copies(refs, send, recv, axes, False):
            cp.start()

    return _split_start(name, body, list(shards) + list(fulls), (3 * len(axes),), after)


def _gather_wait(name, started, axes, after):
    def body(refs, send, recv):
        for cp in _gather_copies(refs, send, recv, axes, True):
            cp.wait_send()
            cp.wait_recv()

    return _split_wait(name, body, started, after)[len(axes):]


def _piece_shape(shape, axis):
    R, C = shape
    return (R // 2, C // 4) if axis == 1 else (R // 8, C)


def _scatter_copies(refs, send, recv, axes):
    n = len(axes)
    x, y, c = _me()
    return [pltpu.make_async_remote_copy(
        src_ref=_region(refs[k], axes[k], 2 * px + py, pc), dst_ref=refs[n + k].at[m], send_sem=send.at[7 * k + m],
        recv_sem=recv.at[7 * k + m], device_id=(px, py, pc), device_id_type=MESH)
        for k in range(n) for m, (px, py, pc) in _peers7(x, y, c)]


def _scatter_start(name, gs, axes):
    def body(refs, send, recv):
        for cp in _scatter_copies(refs, send, recv, axes):
            cp.start()

    lands = [lax.empty((7,) + _piece_shape(g.shape, ax), g.dtype) for g, ax in zip(gs, axes)]
    return _split_start(name, body, list(gs) + lands, (7 * len(axes),))


def _scatter_wait(name, started, axes, after):
    def body(refs, send, recv):
        for cp in _scatter_copies(refs, send, recv, axes):
            cp.wait_send()
            cp.wait_recv()

    out = _split_wait(name, body, started, after)
    return out[:len(axes)], out[len(axes):]


def _own_piece(g, axis):
    x, y, c = _me()
    pr, pc_ = _piece_shape(g.shape, axis)
    if axis == 1:
        return lax.dynamic_slice(g, (c * pr, (2 * x + y) * pc_), (pr, pc_))
    return lax.dynamic_slice(g, ((2 * x + y) * 2 * pr + c * pr, 0), (pr, pc_))


def _sum_pieces(name, own, slots, tr=256):
    S, R, C = slots.shape
    tr = _tile(R, tr, 16)
    nblk = R // tr
    c_arr = lax.axis_index("c").astype(jnp.int32).reshape(1)

    def body(c_ref, own_ref, s_ref, o_ref):
        acc = own_ref[...].astype(F32)
        for s in range(S):
            acc = acc + s_ref[s].astype(F32)
        o_ref[...] = acc

    grid_spec = pltpu.PrefetchScalarGridSpec(
        num_scalar_prefetch=1, grid=(nblk,),
        in_specs=[pl.BlockSpec((tr, C), lambda i, c_ref: (i, 0)), pl.BlockSpec((S, tr, C), lambda i, c_ref: (0, i, 0))],
        out_specs=pl.BlockSpec((tr, C), lambda i, c_ref: (c_ref[0] * nblk + i, 0)))
    return pl.pallas_call(
        body, name=name, grid_spec=grid_spec, out_shape=jax.ShapeDtypeStruct((2 * R, C), F32),
        compiler_params=_params(("parallel",)),
    )(c_arr, own, slots)


def _half_copies(refs, send, recv, mine):
    x, y, c = _me()
    copies = []
    for k, ref in enumerate(refs):
        hr = ref.shape[0] // 2
        rows = ref.at[pl.ds((c if mine else 1 - c) * hr, hr), :]
        copies.append(pltpu.make_async_remote_copy(
            src_ref=rows, dst_ref=rows, send_sem=send.at[k], recv_sem=recv.at[k], device_id=(x, y, 1 - c),
            device_id_type=MESH))
    return copies


def _join_start(name, bufs):
    def body(refs, send, recv):
        for cp in _half_copies(refs, send, recv, True):
            cp.start()

    return _split_start(name, body, list(bufs), (len(bufs),))


def _join_wait(name, started, after):
    def body(refs, send, recv):
        for cp in _half_copies(refs, send, recv, False):
            cp.wait_send()
            cp.wait_recv()

    return _split_wait(name, body, started, after)


def _adamw(name, w, g, m, v, tr=256, deps=()):
    R, C = w.shape
    tr = _tile(R, tr, 8)
    c1 = 1.0 - ADAM_B1 ** ADAM_STEP
    c2 = 1.0 - ADAM_B2 ** ADAM_STEP

    def body(w_ref, g_ref, m_ref, v_ref, *rest):
        d_ref, nm_ref, nv_ref, go_ref = rest[len(deps):]
        gv = g_ref[...]
        go_ref[...] = gv
        nm = ADAM_B1 * m_ref[...] + (1.0 - ADAM_B1) * gv
        nv = ADAM_B2 * v_ref[...] + (1.0 - ADAM_B2) * jnp.square(gv)
        d_ref[...] = -ADAM_LR * ((nm / c1) / (jnp.sqrt(nv / c2) + ADAM_EPS) + ADAM_WD * w_ref[...])
        nm_ref[...] = nm
        nv_ref[...] = nv

    spec = pl.BlockSpec((tr, C), lambda i: (i, 0))
    return pl.pallas_call(
        body, name=name, grid=(R // tr,), in_specs=[spec] * 4 + [pl.BlockSpec(TOKEN, lambda i: (0, 0))] * len(deps),
        out_specs=[spec] * 4, out_shape=[jax.ShapeDtypeStruct((R, C), F32)] * 4,
        compiler_params=_params(("parallel",)),
    )(w, g, m, v, *deps)


def kernel(x, p, g_mix, w_in, w_dw, b_dw, g_conv_ln, b_conv_ln, w_out, g_ffn, w_gate, w_up, w_down, g_ple, w_pgate, b_pgate, w_ple, g_final, loss_target, m_g_mix, m_w_in, m_w_dw, m_b_dw, m_g_conv_ln, m_b_conv_ln, m_w_out, m_g_ffn, m_w_gate, m_w_up, m_w_down, m_g_ple, m_w_pgate, m_b_pgate, m_w_ple, m_g_final, v_g_mix, v_w_in, v_w_dw, v_b_dw, v_g_conv_ln, v_b_conv_ln, v_w_out, v_g_ffn, v_w_gate, v_w_up, v_w_down, v_g_ple, v_w_pgate, v_b_pgate, v_w_ple, v_g_final):
    T, D = x.shape[1], x.shape[2]
    DC = b_dw.shape[1]
    DA = D - DC
    H = DA // HEAD_DIM
    DP = p.shape[3]
    assert T % SUPER == 0 and DA == DC
    x2 = x.reshape(T, D)
    p2 = p.reshape(T, DP)
    tgt = loss_target.reshape(T, D)
    g_final2 = g_final.reshape(1, D)

    big = dict(w_in=(w_in[0], 1), w_out=(w_out[0], 0), w_gate=(w_gate[0], 1), w_up=(w_up[0], 1),
               w_down=(w_down[0], 0), w_pgate=(w_pgate[0], 0), w_ple=(w_ple[0], 1))
    axis_of = {k: big[k][1] for k in big}
    dw_shard = jnp.pad(w_dw.reshape(CONV_W, -1), ((0, HALO - CONV_W), (0, 0)))
    w_in_full, w_dw_full = _gather_now(big["w_in"][0].astype(BF16), dw_shard)
    later = ["w_out", "w_gate", "w_up", "w_down", "w_pgate", "w_ple"]
    shard16 = {k: big[k][0].astype(BF16) for k in later}
    placed = {k: _place_shard("place_" + k, shard16[k], axis_of[k]) for k in later}
    travelling = {}

    def fetch(groups, after):
        for keys in groups:
            travelling[keys] = _gather_start("gather_start_" + keys[0], [shard16[k] for k in keys],
                                             [placed[k] for k in keys], [axis_of[k] for k in keys], after)
        return [travelling[keys][-1] for keys in groups]

    def weights(keys, after):
        return _gather_wait("gather_wait_" + keys[0], travelling[keys], [axis_of[k] for k in keys], after)

    issued = fetch([("w_out",), ("w_gate", "w_up")], w_in_full)

    a = _rms_fwd("rms_mix", x2, g_mix, deps=issued)
    z = _matmul("mm_in", [[(a, w_in_full)]], "nn", [F32], _plain)[0]
    cat, o_attn, lse = _attn_fwd(z, D, H)
    cat, conv_c = _conv_fwd(z, cat, w_dw_full, b_dw, g_conv_ln, b_conv_ln, 3 * DA // DC)
    W = dict(w_in=w_in_full)
    W["w_out"], = weights(("w_out",), cat)
    def residual_and_norm(accs, extras):
        h = accs[0] + extras[0]
        r = lax.rsqrt(jnp.mean(h * h, axis=-1, keepdims=True) + EPS)
        return h, (h * r) * extras[1]

    h1, f = _matmul("mm_out", [[(cat, W["w_out"])]], "nn", [F32, BF16], residual_and_norm,
                    extras=[(x2, "mn"), (g_ffn, "n")], tm=512, tn=D, sub=256)
    W["w_gate"], W["w_up"] = weights(("w_gate", "w_up"), f)

    def swiglu(accs, extras):
        gt, up = accs
        return gt, up, (gt * jax.nn.sigmoid(gt)) * up

    gate, up, act = _matmul("mm_gate_up", [[(f, W["w_gate"])], [(f, W["w_up"])]], "nn", [BF16, BF16, BF16],
                            swiglu, tm=1024, tn=512, sub=256,
                            deps=fetch([("w_down",), ("w_pgate", "w_ple")], W["w_up"]))
    W["w_down"], = weights(("w_down",), act)
    h2 = _matmul("mm_down", [[(act, W["w_down"])]], "nn", [F32], _add_residual, extras=[(h1, "mn")], tm=512, tk=5632)[0]
    n2 = _rms_fwd("rms_ple", h2, g_ple)
    W["w_pgate"], W["w_ple"] = weights(("w_pgate", "w_ple"), n2)
    sent, joined = {}, {}

    def send_grads(keys, gs):
        sent[keys] = _scatter_start("scatter_start_" + keys[0], gs, [axis_of[k] for k in keys])
        return sent[keys][-1]

    def reduce_grads(groups, after):
        keys_all, halves = (), []
        for keys in groups:
            axes = [axis_of[k] for k in keys]
            g_thru, lands = _scatter_wait("scatter_wait_" + keys[0], sent[keys], axes, after)
            halves += [_sum_pieces("sum_" + k, _own_piece(g, ax), land)
                       for k, g, ax, land in zip(keys, g_thru, axes, lands)]
            keys_all += keys
        joined[keys_all] = _join_start("join_start_" + keys_all[0], halves)
        return joined[keys_all][-1]

    dh3, de, du4, dg_final, db_pgate, loss_part = _ple_loss_head(n2, W["w_pgate"], b_pgate, h2, p2, W["w_ple"], tgt,
                                                                 g_final2)
    t_ple = send_grads(("w_ple", "w_pgate"),
                       [_matmul("mm_dw_ple", [[(p2, de)]], "tn", [BF16], _plain, tk=1024)[0],
                        _matmul("mm_dw_pgate", [[(n2, du4)]], "tn", [BF16], _plain, tk=2048)[0]])
    dh2, dh2_16, dg_ple = _rms_bwd("rms_ple_bwd", du4, h2, g_ple, dh3, weight=W["w_pgate"])
    t_down = send_grads(("w_down",), [_matmul("mm_dw_down", [[(act, dh2_16)]], "tn", [BF16], _plain, tm=1408, tn=2048,
                                              tk=1024, deps=[t_ple])[0]])

    def swiglu_bwd(accs, extras):
        gt, up = extras[0].astype(F32), extras[1].astype(F32)
        sg = jax.nn.sigmoid(gt)
        dact = accs[0]
        return dact * up * (sg * (1.0 + gt * (1.0 - sg))), dact * (gt * sg)

    dgate, dup = _matmul("mm_dact", [[(dh2_16, W["w_down"])]], "nt", [BF16, BF16], swiglu_bwd,
                         extras=[(gate, "mn"), (up, "mn")], tm=2048, tn=512, sub=256, deps=[t_down])
    t_up = send_grads(("w_gate", "w_up"),
                      [_matmul("mm_dw_gate", [[(f, dgate)]], "tn", [BF16], _plain, tn=1408, tk=2048)[0],
                       _matmul("mm_dw_up", [[(f, dup)]], "tn", [BF16], _plain, tn=1408, tk=2048)[0]])
    df = _matmul("mm_df", [[(dgate, W["w_gate"]), (dup, W["w_up"])]], "nt", [F32], _plain, tm=1024, tn=256,
                 tk=5632, deps=[t_up], vmem=VMEM_LIMIT_LARGE)[0]
    dh1, dh1_16, dg_ffn = _rms_bwd("rms_ffn_bwd", df, h1, g_ffn, dh2)
    t_out = send_grads(("w_out",), [_matmul("mm_dw_out", [[(cat, dh1_16)]], "tn", [BF16], _plain, tk=2048)[0]])
    dcat = _matmul("mm_dcat", [[(dh1_16, W["w_out"])]], "nt", [F32], _plain, deps=[t_out])[0]
    j1 = reduce_grads([("w_ple", "w_pgate"), ("w_down",)], dcat)
    dc, dg_ln, db_ln, db_dw = _conv_ln_bwd(dcat, conv_c, g_conv_ln, b_conv_ln, deps=[j1])
    dcv, dcg, dw_dw = _conv_bwd(z, dc, w_dw_full, 3 * DA // DC)
    dq, dk, dv = _attn_bwd(z, dcat, o_attn, lse, H)
    dz = [dq, dk, dv, dcv, dcg]
    dw_in = jnp.concatenate([_matmul("mm_dw_in%d" % n, [[(a, part)]], "tn", [BF16], _plain, tk=2048)[0]
                             for n, part in enumerate(dz)], axis=1)
    t_in = send_grads(("w_in",), [dw_in])
    j2 = reduce_grads([("w_gate", "w_up"), ("w_out",)], dw_in)
    da = _matmul("mm_da", [[(part, W["w_in"], n) for n, part in enumerate(dz)]], "nt", [F32], _plain, tm=512,
                 tk=DA, deps=[t_in, j2])[0]
    grad_x, _, dg_mix = _rms_bwd("rms_mix_bwd", da, x2, g_mix, dh1)

    wide = [dg_mix, dg_ffn, dg_ple, db_pgate, dg_final,
            jnp.concatenate([db_dw, dg_ln], axis=1), jnp.concatenate([db_ln, jnp.zeros_like(db_ln)], axis=1),
            jnp.pad(loss_part, ((0, 0), (0, D - 128))),
            dw_dw.reshape(HALO * DC // D, D)]
    small = jnp.concatenate(wide, axis=0)
    small = jnp.pad(small, ((0, -small.shape[0] % 8), (0, 0)))
    small_sum = _sum_slots("sum_small", _exchange_small(small))
    j3 = reduce_grads([("w_in",)], small_sum)

    grads, deltas, new_m, new_v = {}, {}, {}, {}
    moments = dict(w_in=(m_w_in, v_w_in), w_out=(m_w_out, v_w_out), w_gate=(m_w_gate, v_w_gate),
                   w_up=(m_w_up, v_w_up), w_down=(m_w_down, v_w_down), w_pgate=(m_w_pgate, v_w_pgate),
                   w_ple=(m_w_ple, v_w_ple))
    last, deps = small_sum, [j3]
    for keys in list(joined):
        for k, g_k in zip(keys, _join_wait("join_wait_" + keys[0], joined[keys], last)):
            d_, m_, v_, g_ = _adamw("adamw_" + k, big[k][0], g_k, moments[k][0][0], moments[k][1][0], deps=deps)
            grads[k], deltas[k], new_m[k], new_v[k] = g_[None], d_[None], m_[None], v_[None]
            last, deps = d_, ()

    half = lambda r, lo: small_sum[r:r + 1, lo * DC:(lo + 1) * DC]
    vec = dict(g_mix=small_sum[0:1], g_ffn=small_sum[1:2], g_ple=small_sum[2:3], b_pgate=small_sum[3:4],
               g_final=small_sum[4:5], b_dw=half(5, 0), g_conv_ln=half(5, 1), b_conv_ln=half(6, 0))
    loss = small_sum[7, 0]
    dw_dw_sum = small_sum[8:8 + HALO * DC // D].reshape(HALO, DC)
    s_me = 2 * lax.axis_index("x") + lax.axis_index("y")
    cs = w_dw.shape[3]
    vec["w_dw"] = lax.dynamic_slice(dw_dw_sum, (0, s_me * cs), (CONV_W, cs))
    small_w = dict(g_mix=(g_mix, m_g_mix, v_g_mix), g_ffn=(g_ffn, m_g_ffn, v_g_ffn), g_ple=(g_ple, m_g_ple, v_g_ple),
                   b_pgate=(b_pgate, m_b_pgate, v_b_pgate), g_final=(g_final, m_g_final, v_g_final),
                   b_dw=(b_dw, m_b_dw, v_b_dw), g_conv_ln=(g_conv_ln, m_g_conv_ln, v_g_conv_ln),
                   b_conv_ln=(b_conv_ln, m_b_conv_ln, v_b_conv_ln), w_dw=(w_dw, m_w_dw, v_w_dw))
    for k, (w_, m_, v_) in small_w.items():
        shape = w_.shape
        g2 = vec[k]
        to2 = lambda t: t.reshape(g2.shape)
        d_, nm_, nv_, g_ = _adamw("adamw_" + k, to2(w_), g2, to2(m_), to2(v_))
        grads[k], deltas[k], new_m[k], new_v[k] = (t.reshape(shape) for t in (g_, d_, nm_, nv_))

    order = ["g_mix", "w_in", "w_dw", "b_dw", "g_conv_ln", "b_conv_ln", "w_out", "g_ffn", "w_gate", "w_up", "w_down",
             "g_ple", "w_pgate", "b_pgate", "w_ple", "g_final"]
    return (loss, grad_x.reshape(x.shape), *[grads[k] for k in order], *[deltas[k] for k in order],
            *[new_m[k] for k in order], *[new_v[k] for k in order])
```

```python
import functools

import jax
import jax.numpy as jnp
from jax import lax
from jax.experimental import pallas as pl
from jax.experimental.pallas import tpu as pltpu

F32 = jnp.float32
BF16 = jnp.bfloat16

EPS = 1e-6
HEAD_DIM = 128
BLK = 128
DILATIONS = (1, 4, 16)
SUPER = BLK * DILATIONS[-1]
CONV_W = 31
HALO = 32
ADAM_LR, ADAM_B1, ADAM_B2, ADAM_EPS, ADAM_WD, ADAM_STEP = 0.001, 0.9, 0.999, 1e-08, 0.01, 10

V7X_VMEM_BYTES = 64 * 1024 * 1024
VMEM_LIMIT = V7X_VMEM_BYTES * 3 // 4
VMEM_LIMIT_LARGE = V7X_VMEM_BYTES * 15 // 16
MESH = pl.DeviceIdType.MESH
ANY = pl.BlockSpec(memory_space=pl.ANY)
HBM = pl.BlockSpec(memory_space=pltpu.HBM)
SEM = pl.BlockSpec(memory_space=pltpu.SEMAPHORE)
EFFECT = pltpu.SideEffectType.DATAFLOW_SIDE_EFFECTING
TOKEN = (8, 128)


def _params(semantics=None, vmem=VMEM_LIMIT, **kw):
    return pltpu.CompilerParams(dimension_semantics=semantics, vmem_limit_bytes=vmem, **kw)


def _tile(n, want, mult=128):
    if n <= want:
        return n
    for t in range(want - want % mult, 0, -mult):
        if n % t == 0:
            return t
    raise ValueError((n, want, mult))


_DIMS = {"nn": ((1,), (0,)), "nt": ((1,), (1,)), "tn": ((0,), (0,))}


def _matmul(name, groups, mode, out_dtypes, epilogue, extras=(), tm=1024, tn=1024, tk=2048, sub=None, deps=(),
            vmem=VMEM_LIMIT):
    a0, b0 = groups[0][0][:2]
    if mode == "nn":
        (M, K), N = a0.shape, b0.shape[1]
    elif mode == "nt":
        (M, K), N = a0.shape, b0.shape[0]
    else:
        (K, M), N = a0.shape, b0.shape[1]
    tm, tn, tk = _tile(M, tm), _tile(N, tn), _tile(K, tk)
    nk = K // tk
    if mode == "tn":
        a_spec = pl.BlockSpec((tk, tm), lambda i, j, k: (k, i))
    else:
        a_spec = pl.BlockSpec((tm, tk), lambda i, j, k: (i, k))
    operands, in_specs = [], []
    for grp in groups:
        for a, b, *k0 in grp:
            k0 = k0[0] if k0 else 0
            if mode == "nt":
                b_spec = pl.BlockSpec((tn, tk), functools.partial(lambda i, j, k, k0: (j, k + k0), k0=k0))
            else:
                b_spec = pl.BlockSpec((tk, tn), functools.partial(lambda i, j, k, k0: (k + k0, j), k0=k0))
            operands += [a, b]
            in_specs += [a_spec, b_spec]
    for arr, kind in extras:
        operands.append(arr)
        if kind == "mn":
            in_specs.append(pl.BlockSpec((tm, tn), lambda i, j, k: (i, j)))
        else:
            in_specs.append(pl.BlockSpec((1, tn), lambda i, j, k: (0, j)))
    for tok in deps:
        operands.append(tok)
        in_specs.append(pl.BlockSpec(TOKEN, lambda i, j, k: (0, 0)))
    n_pairs = [len(g) for g in groups]
    n_ex, n_out, n_grp, n_dep = len(extras), len(out_dtypes), len(groups), len(deps)
    out_shape = [jax.ShapeDtypeStruct((M, N), dt) for dt in out_dtypes]
    out_specs = [pl.BlockSpec((tm, tn), lambda i, j, k: (i, j)) for _ in out_dtypes]

    kinds = [kind for _, kind in extras]
    sub = tm if (sub is None or nk > 1 or mode == "tn") else sub
    assert tm % sub == 0

    def body(*refs):
        ex_refs = refs[2 * sum(n_pairs):2 * sum(n_pairs) + n_ex]
        pos = 2 * sum(n_pairs) + n_ex + n_dep
        out_refs = refs[pos:pos + n_out]
        acc_refs = refs[pos + n_out:]

        def products(rows):
            pos, parts = 0, []
            for g in range(n_grp):
                part = None
                for _ in range(n_pairs[g]):
                    a_ref, b_ref = refs[pos], refs[pos + 1]
                    pos += 2
                    d = lax.dot_general(a_ref[rows].astype(BF16), b_ref[...].astype(BF16),
                                        (_DIMS[mode], ((), ())), preferred_element_type=F32)
                    part = d if part is None else part + d
                parts.append(part)
            return parts

        def finish(accs, rows):
            outs = epilogue(accs, [e[rows] if kind == "mn" else e[...] for e, kind in zip(ex_refs, kinds)])
            for o_ref, o in zip(out_refs, outs):
                o_ref[rows] = o.astype(o_ref.dtype)

        if nk == 1:
            for r in range(tm // sub):
                rows = (pl.ds(r * sub, sub), slice(None)) if sub < tm else (slice(None), slice(None))
                finish(products(rows), rows)
        else:
            k = pl.program_id(2)
            parts = products((slice(None), slice(None)))

            @pl.when(k == 0)
            def _():
                for acc, part in zip(acc_refs, parts):
                    acc[...] = part

            @pl.when(k > 0)
            def _():
                for acc, part in zip(acc_refs, parts):
                    acc[...] += part

            @pl.when(k == nk - 1)
            def _():
                finish([acc[...] for acc in acc_refs], (slice(None), slice(None)))

    scratch = [pltpu.VMEM((tm, tn), F32) for _ in range(n_grp)] if nk > 1 else []
    return pl.pallas_call(
        body, name=name, grid=(M // tm, N // tn, nk),
        in_specs=in_specs, out_specs=out_specs, out_shape=out_shape, scratch_shapes=scratch,
        compiler_params=_params(("parallel", "parallel", "arbitrary"), vmem),
    )(*operands)


def _plain(accs, extras):
    return (accs[0],)


def _add_residual(accs, extras):
    return (accs[0] + extras[0],)


def _rowwise(name, fn, rows, consts, row_outs, acc_outs, tb, deps=(), vmem=VMEM_LIMIT):
    T = rows[0][0].shape[0]
    tb = _tile(T, tb, 16)
    operands = [r[0] for r in rows] + list(consts) + list(deps)
    in_specs = [pl.BlockSpec((tb, c), functools.partial(lambda i, cb: (i, cb), cb=cb)) for _, c, cb in rows]
    in_specs += [pl.BlockSpec(c.shape, functools.partial(lambda i, nd: (0,) * nd, nd=c.ndim)) for c in consts]
    in_specs += [pl.BlockSpec(TOKEN, lambda i: (0, 0)) for _ in deps]
    out_shape = [jax.ShapeDtypeStruct(s, dt) for s, dt, _, _ in row_outs]
    out_specs = [pl.BlockSpec((tb, c), functools.partial(lambda i, cb: (i, cb), cb=cb)) for _, _, c, cb in row_outs]
    out_shape += [jax.ShapeDtypeStruct(s, F32) for s in acc_outs]
    out_specs += [pl.BlockSpec(s, functools.partial(lambda i, nd: (0,) * nd, nd=len(s))) for s in acc_outs]
    n_rows, n_consts, n_ro, n_dep = len(rows), len(consts), len(row_outs), len(deps)

    def body(*refs):
        row_refs = refs[:n_rows]
        const_refs = refs[n_rows:n_rows + n_consts]
        out_refs = refs[n_rows + n_consts + n_dep:]
        ro, ao = fn([r[...] for r in row_refs], [c[...] for c in const_refs])
        for o_ref, o in zip(out_refs[:n_ro], ro):
            o_ref[...] = o.astype(o_ref.dtype)
        if acc_outs:
            i = pl.program_id(0)

            @pl.when(i == 0)
            def _():
                for a_ref, a in zip(out_refs[n_ro:], ao):
                    a_ref[...] = a

            @pl.when(i > 0)
            def _():
                for a_ref, a in zip(out_refs[n_ro:], ao):
                    a_ref[...] += a

    return pl.pallas_call(
        body, name=name, grid=(T // tb,), in_specs=in_specs, out_specs=out_specs, out_shape=out_shape,
        compiler_params=_params(("arbitrary",) if acc_outs else ("parallel",), vmem),
    )(*operands)


def _colsum(v):
    return jnp.sum(v, axis=0, keepdims=True)


def _rms_fwd(name, x, g, tb=512, deps=()):
    T, D = x.shape

    def fn(rows, consts):
        xv, gv = rows[0], consts[0]
        r = lax.rsqrt(jnp.mean(xv * xv, axis=-1, keepdims=True) + EPS)
        return ((xv * r) * gv,), ()

    return _rowwise(name, fn, [(x, D, 0)], [g], [((T, D), BF16, D, 0)], [], tb, deps=deps)[0]


def _rms_bwd(name, dy, x, g, resid, tb=256):
    T, D = x.shape

    def fn(rows, consts):
        dyv, xv, rv = rows
        gv = consts[0]
        r = lax.rsqrt(jnp.mean(xv * xv, axis=-1, keepdims=True) + EPS)
        n = xv * r
        dn = dyv * gv
        dx = r * (dn - n * jnp.mean(dn * n, axis=-1, keepdims=True))
        tot = rv + dx
        return (tot, tot), (_colsum(dyv * n),)

    return _rowwise(name, fn, [(dy, D, 0), (x, D, 0), (resid, D, 0)], [g],
                    [((T, D), F32, D, 0), ((T, D), BF16, D, 0)], [(1, D)], tb)


NBLK = SUPER // BLK


def _classes(ref, dil, rows, start=0, dtype=None):
    parts = [ref[pl.ds(start + r, rows, stride=dil), :] if dil > 1 else ref[pl.ds(start, rows), :]
             for r in range(dil)]
    if dtype is not None:
        parts = [p.astype(dtype) for p in parts]
    return parts


def _keys_with_prev(ref, prev_ref, dil):
    L = SUPER // dil
    own = _classes(ref, dil, L, dtype=BF16)
    last = _classes(prev_ref, dil, BLK, start=SUPER - BLK * dil, dtype=BF16)
    blocks = []
    for r in range(dil):
        ext = jnp.concatenate([last[r], own[r]], axis=0)
        blocks += [ext[j * BLK:(j + 2) * BLK] for j in range(L // BLK)]
    return jnp.stack(blocks, axis=0)


def _band_mask(dil, has_prev):
    qi = lax.broadcasted_iota(jnp.int32, (BLK, 2 * BLK), 0)
    kj = lax.broadcasted_iota(jnp.int32, (BLK, 2 * BLK), 1)
    own = jnp.logical_and(kj >= BLK, kj - BLK <= qi)
    prev = jnp.logical_and(kj < BLK, kj >= qi)
    b = lax.broadcasted_iota(jnp.int32, (NBLK, 1, 1), 0)
    first = (b & (SUPER // (BLK * dil) - 1)) == 0
    prev_ok = jnp.logical_or(jnp.logical_not(first), has_prev)
    return jnp.logical_or(own[None], jnp.logical_and(prev[None], prev_ok))


def _bdot(a, b, ca, cb):
    return lax.dot_general(a, b, (((ca,), (cb,)), ((0,), (0,))), preferred_element_type=F32)


def _put_classes(dst, value, dil, rows, start=0, add=False, src_start=0, src_stride=None):
    src_stride = rows if src_stride is None else src_stride
    for r in range(dil):
        idx = (pl.ds(start + r, rows, stride=dil) if dil > 1 else pl.ds(start, rows), slice(None))
        part = value[src_start + r * src_stride:src_start + r * src_stride + rows]
        dst[idx] = dst[idx] + part if add else part


def _attn_fwd(z, cat_width, n_heads):
    T = z.shape[0]
    H = n_heads
    DA = H * HEAD_DIM
    nb = T // SUPER
    scale = HEAD_DIM ** -0.5

    def body(q_ref, k_ref, v_ref, kp_ref, vp_ref, cat_ref, o_ref, lse_ref, ob, lb):
        has_prev = pl.program_id(1) > 0
        for b, dil in enumerate(DILATIONS):
            L = SUPER // dil
            q3 = jnp.concatenate(_classes(q_ref, dil, L, dtype=BF16), axis=0).reshape(NBLK, BLK, HEAD_DIM)
            k3 = _keys_with_prev(k_ref, kp_ref, dil)
            v3 = _keys_with_prev(v_ref, vp_ref, dil)
            s = jnp.where(_band_mask(dil, has_prev), _bdot(q3, k3, 2, 2) * scale, -jnp.inf)
            m = jnp.max(s, axis=-1, keepdims=True)
            e = jnp.exp(s - m)
            den = jnp.sum(e, axis=-1, keepdims=True)
            o3 = _bdot((e * (1.0 / den)).astype(BF16), v3, 2, 1)
            lse3 = jnp.broadcast_to(m + jnp.log(den), (NBLK, BLK, HEAD_DIM))
            _put_classes(ob.at[b], o3.reshape(SUPER, HEAD_DIM), dil, L)
            _put_classes(lb.at[b], lse3.reshape(SUPER, HEAD_DIM), dil, L)
        l0, l1, l2 = lb[0], lb[1], lb[2]
        mx = jnp.maximum(jnp.maximum(l0, l1), l2)
        tot = mx + jnp.log(jnp.exp(l0 - mx) + jnp.exp(l1 - mx) + jnp.exp(l2 - mx))
        o = jnp.exp(l0 - tot) * ob[0] + jnp.exp(l1 - tot) * ob[1] + jnp.exp(l2 - tot) * ob[2]
        o_ref[...] = o
        cat_ref[...] = o.astype(BF16)
        lse_ref[...] = tot

    blk = (SUPER, HEAD_DIM)
    in_specs = [
        pl.BlockSpec(blk, lambda h, n: (n, h)),
        pl.BlockSpec(blk, lambda h, n: (n, H + h)),
        pl.BlockSpec(blk, lambda h, n: (n, 2 * H + h)),
        pl.BlockSpec(blk, lambda h, n: (jnp.maximum(n - 1, 0), H + h)),
        pl.BlockSpec(blk, lambda h, n: (jnp.maximum(n - 1, 0), 2 * H + h)),
    ]
    out_spec = pl.BlockSpec(blk, lambda h, n: (n, h))
    return pl.pallas_call(
        body, name="attn_fwd", grid=(H, nb), in_specs=in_specs, out_specs=[out_spec] * 3,
        out_shape=[jax.ShapeDtypeStruct((T, cat_width), BF16), jax.ShapeDtypeStruct((T, DA), F32),
                   jax.ShapeDtypeStruct((T, DA), F32)],
        scratch_shapes=[pltpu.VMEM((3, SUPER, HEAD_DIM), F32), pltpu.VMEM((3, SUPER, HEAD_DIM), F32)],
        compiler_params=_params(("parallel", "parallel")),
    )(z, z, z, z, z)


def _attn_bwd(z, dcat, o, lse, n_heads):
    T = z.shape[0]
    H = n_heads
    nb = T // SUPER
    scale = HEAD_DIM ** -0.5

    def body(q_ref, k_ref, v_ref, kp_ref, vp_ref, do_ref, o_ref, lse_ref, dz_q, dz_k, dz_v,
             dq_acc, dk_acc, dv_acc, dkp_acc, dvp_acc, dsum):
        i = pl.program_id(1)
        has_prev = i < nb - 1

        @pl.when(i == 0)
        def _():
            dk_acc[...] = jnp.zeros_like(dk_acc)
            dv_acc[...] = jnp.zeros_like(dv_acc)

        @pl.when(i > 0)
        def _():
            dk_acc[...] = dkp_acc[...]
            dv_acc[...] = dvp_acc[...]

        dq_acc[...] = jnp.zeros_like(dq_acc)
        dkp_acc[...] = jnp.zeros_like(dkp_acc)
        dvp_acc[...] = jnp.zeros_like(dvp_acc)
        dsum[...] = jnp.broadcast_to(jnp.sum(do_ref[...] * o_ref[...], axis=-1, keepdims=True), (SUPER, HEAD_DIM))
        for dil in DILATIONS:
            L = SUPER // dil
            wide = lambda ref: jnp.tile(jnp.concatenate(_classes(ref, dil, L), axis=0).reshape(NBLK, BLK, HEAD_DIM),
                                        (1, 1, 2))
            q3 = jnp.concatenate(_classes(q_ref, dil, L, dtype=BF16), axis=0).reshape(NBLK, BLK, HEAD_DIM)
            do3 = jnp.concatenate(_classes(do_ref, dil, L, dtype=BF16), axis=0).reshape(NBLK, BLK, HEAD_DIM)
            k3 = _keys_with_prev(k_ref, kp_ref, dil)
            v3 = _keys_with_prev(v_ref, vp_ref, dil)
            p = jnp.where(_band_mask(dil, has_prev), jnp.exp(_bdot(q3, k3, 2, 2) * scale - wide(lse_ref)), 0.0)
            ds = (p * (_bdot(do3, v3, 2, 2) - wide(dsum)) * scale).astype(BF16)
            dq = _bdot(ds, k3, 2, 1).reshape(SUPER, HEAD_DIM)
            dk = _bdot(ds, q3, 1, 1)
            dv = _bdot(p.astype(BF16), do3, 1, 1)
            _put_classes(dq_acc, dq, dil, L, add=True)
            for acc, prev_acc, g in ((dk_acc, dkp_acc, dk), (dv_acc, dvp_acc, dv)):
                _put_classes(acc, g[:, BLK:, :].reshape(SUPER, HEAD_DIM), dil, L, add=True)
                to_prev = g[:, :BLK, :].reshape(SUPER, HEAD_DIM)
                if L > BLK:
                    _put_classes(acc, to_prev, dil, L - BLK, add=True, src_start=BLK, src_stride=L)
                _put_classes(prev_acc, to_prev, dil, BLK, start=SUPER - BLK * dil, add=True, src_stride=L)
        dz_q[...] = dq_acc[...].astype(BF16)
        dz_k[...] = dk_acc[...].astype(BF16)
        dz_v[...] = dv_acc[...].astype(BF16)

    blk = (SUPER, HEAD_DIM)
    row = lambda i: nb - 1 - i
    in_specs = [
        pl.BlockSpec(blk, lambda h, i: (row(i), h)),
        pl.BlockSpec(blk, lambda h, i: (row(i), H + h)),
        pl.BlockSpec(blk, lambda h, i: (row(i), 2 * H + h)),
        pl.BlockSpec(blk, lambda h, i: (jnp.maximum(row(i) - 1, 0), H + h)),
        pl.BlockSpec(blk, lambda h, i: (jnp.maximum(row(i) - 1, 0), 2 * H + h)),
        pl.BlockSpec(blk, lambda h, i: (row(i), h)),
        pl.BlockSpec(blk, lambda h, i: (row(i), h)),
        pl.BlockSpec(blk, lambda h, i: (row(i), h)),
    ]
    out_spec = pl.BlockSpec(blk, lambda h, i: (row(i), h))
    return pl.pallas_call(
        body, name="attn_bwd", grid=(H, nb), in_specs=in_specs, out_specs=[out_spec] * 3,
        out_shape=[jax.ShapeDtypeStruct((T, H * HEAD_DIM), BF16)] * 3,
        scratch_shapes=[pltpu.VMEM(blk, F32) for _ in range(6)],
        compiler_params=_params(("parallel", "arbitrary")),
    )(z, z, z, z, z, dcat, o, lse)


def _glu(cv, cg):
    return cv * jax.nn.sigmoid(cg)


SUBLANES = 8


def _fill_shifted(sh, ext, rows):
    for b in range(1, SUBLANES):
        sh[b - 1, pl.ds(0, rows - SUBLANES), :] = ext[pl.ds(b, rows - SUBLANES), :]


def _window(sh, ext, offset, rows):
    b = offset % SUBLANES
    if b == 0:
        return ext[pl.ds(offset, rows), :]
    return sh[b - 1, pl.ds(offset - b, rows), :]


def _conv_fwd(z, cat, w_dw, b_dw, g_ln, b_ln, col0, tb=512):
    T = z.shape[0]
    DC = w_dw.shape[1]
    tb = _tile(T, tb, HALO)
    hb = tb // HALO
    cat_cb = cat.shape[1] // DC - 1

    def body(cv_ref, cg_ref, cvh_ref, cgh_ref, w_ref, bdw_ref, g_ref, b_ref, cat_in, cat_ref, c_ref, u_ext, sh):
        i = pl.program_id(0)
        halo = _glu(cvh_ref[...], cgh_ref[...])
        u_ext[pl.ds(0, HALO), :] = jnp.where(i > 0, halo, 0.0)
        u_ext[pl.ds(HALO, tb), :] = _glu(cv_ref[...], cg_ref[...])
        _fill_shifted(sh, u_ext, tb + HALO)
        acc = jnp.broadcast_to(bdw_ref[...], (tb, DC))
        for j in range(CONV_W):
            acc = acc + w_ref[pl.ds(j, 1), :] * _window(sh, u_ext, HALO - (CONV_W - 1) + j, tb)
        c_ref[...] = acc
        mu = jnp.mean(acc, axis=-1, keepdims=True)
        var = jnp.mean(jnp.square(acc - mu), axis=-1, keepdims=True)
        y = (acc - mu) * lax.rsqrt(var + EPS) * g_ref[...] + b_ref[...]
        cat_ref[...] = (y * jax.nn.sigmoid(y)).astype(BF16)

    cur = lambda cb: pl.BlockSpec((tb, DC), lambda i: (i, cb))
    halo = lambda cb: pl.BlockSpec((HALO, DC), lambda i: (jnp.maximum(i * hb - 1, 0), cb))
    whole = lambda a: pl.BlockSpec(a.shape, lambda i: (0, 0))
    return pl.pallas_call(
        body, name="conv_fwd", grid=(T // tb,),
        in_specs=[cur(col0), cur(col0 + 1), halo(col0), halo(col0 + 1), whole(w_dw), whole(b_dw), whole(g_ln),
                  whole(b_ln), ANY],
        out_specs=[pl.BlockSpec((tb, DC), lambda i: (i, cat_cb)), pl.BlockSpec((tb, DC), lambda i: (i, 0))],
        out_shape=[jax.ShapeDtypeStruct(cat.shape, cat.dtype), jax.ShapeDtypeStruct((T, DC), F32)],
        scratch_shapes=[pltpu.VMEM((tb + HALO, DC), F32), pltpu.VMEM((SUBLANES - 1, tb + HALO, DC), F32)],
        input_output_aliases={8: 0},
        compiler_params=_params(("parallel",)),
    )(z, z, z, z, w_dw, b_dw, g_ln, b_ln, cat)


def _conv_ln_bwd(dcat, c, g_ln, b_ln, tb=256, deps=()):
    T, DC = c.shape
    d_cb = dcat.shape[1] // DC - 1

    def fn(rows, consts):
        dov, cv_ = rows
        gv, bv = consts
        mu = jnp.mean(cv_, axis=-1, keepdims=True)
        xc = cv_ - mu
        rstd = lax.rsqrt(jnp.mean(jnp.square(xc), axis=-1, keepdims=True) + EPS)
        ln = xc * rstd
        y = ln * gv + bv
        sg = jax.nn.sigmoid(y)
        dy = dov * (sg * (1.0 + y * (1.0 - sg)))
        dln = dy * gv
        dc = rstd * (dln - jnp.mean(dln, axis=-1, keepdims=True) - ln * jnp.mean(dln * ln, axis=-1, keepdims=True))
        return (dc,), (_colsum(dy * ln), _colsum(dy), _colsum(dc))

    return _rowwise("conv_ln_bwd", fn, [(dcat, DC, d_cb), (c, DC, 0)], [g_ln, b_ln],
                    [((T, DC), F32, DC, 0)], [(1, DC)] * 3, tb, deps=deps)


def _conv_bwd(z, dc, w_dw, col0, tb=512):
    T, DC = dc.shape
    tb = _tile(T, tb, HALO)
    hb = tb // HALO
    nblk = T // tb

    def body(cv_ref, cg_ref, dc_ref, dcn_ref, w_ref, dcv_ref, dcg_ref, dw_ref, dc_ext, sh):
        i = pl.program_id(0)
        cv, cg = cv_ref[...], cg_ref[...]
        sg = jax.nn.sigmoid(cg)
        u = cv * sg
        dc_ext[pl.ds(0, tb), :] = dc_ref[...]
        dc_ext[pl.ds(tb, HALO), :] = jnp.where(i < nblk - 1, dcn_ref[...], 0.0)
        _fill_shifted(sh, dc_ext, tb + HALO)

        @pl.when(i == 0)
        def _():
            dw_ref[...] = jnp.zeros_like(dw_ref)

        du = jnp.zeros((tb, DC), F32)
        for j in range(CONV_W):
            d_j = _window(sh, dc_ext, CONV_W - 1 - j, tb)
            du = du + w_ref[pl.ds(j, 1), :] * d_j
            dw_ref[pl.ds(j, 1), :] += _colsum(u * d_j)
        dcv_ref[...] = (du * sg).astype(BF16)
        dcg_ref[...] = (du * cv * sg * (1.0 - sg)).astype(BF16)

    cur = lambda cb: pl.BlockSpec((tb, DC), lambda i: (i, cb))
    nxt = pl.BlockSpec((HALO, DC), lambda i: (jnp.minimum((i + 1) * hb, T // HALO - 1), 0))
    return pl.pallas_call(
        body, name="conv_bwd", grid=(nblk,),
        in_specs=[cur(col0), cur(col0 + 1), cur(0), nxt, pl.BlockSpec(w_dw.shape, lambda i: (0, 0))],
        out_specs=[cur(0), cur(0), pl.BlockSpec((HALO, DC), lambda i: (0, 0))],
        out_shape=[jax.ShapeDtypeStruct((T, DC), BF16), jax.ShapeDtypeStruct((T, DC), BF16),
                   jax.ShapeDtypeStruct((HALO, DC), F32)],
        scratch_shapes=[pltpu.VMEM((tb + HALO, DC), F32), pltpu.VMEM((SUBLANES - 1, tb + HALO, DC), F32)],
        compiler_params=_params(("arbitrary",)),
    )(z, z, dc, dc, w_dw)


def _ple_loss_head(n2, w_pgate, b_pgate, h2, p, w_ple, target, g_final, g_ple, tb=256):
    T, D = h2.shape

    def fn(rows, consts):
        n2v, h2v, pv, tv = rows
        wp, bp, wple, gv, gp = consts
        gt = jax.nn.sigmoid(lax.dot_general(n2v, wp, (((1,), (0,)), ((), ())), preferred_element_type=F32) + bp)
        ev = lax.dot_general(pv.astype(BF16), wple, (((1,), (0,)), ((), ())), preferred_element_type=F32)
        hv = h2v + ev * gt
        r = lax.rsqrt(jnp.mean(hv * hv, axis=-1, keepdims=True) + EPS)
        n = hv * r
        diff = n * gv - tv
        loss = 0.5 * jnp.sum(jnp.mean(jnp.square(diff), axis=-1, keepdims=True), axis=0, keepdims=True)
        dy = diff * (1.0 / D)
        dn = dy * gv
        dh = r * (dn - n * jnp.mean(dn * n, axis=-1, keepdims=True))
        du4f = dh * ev * gt * (1.0 - gt)
        du4 = du4f.astype(BF16)
        dn2 = lax.dot_general(du4, wp, (((1,), (1,)), ((), ())), preferred_element_type=F32)
        r2 = lax.rsqrt(jnp.mean(h2v * h2v, axis=-1, keepdims=True) + EPS)
        nn = h2v * r2
        dnn = dn2 * gp
        tot = dh + r2 * (dnn - nn * jnp.mean(dnn * nn, axis=-1, keepdims=True))
        return ((tot, tot, dh * gt, du4),
                (_colsum(dy * n), _colsum(du4f), jnp.broadcast_to(loss, (1, 128)), _colsum(dn2 * nn)))

    return _rowwise("ple_loss_head", fn, [(n2, D, 0), (h2, D, 0), (p, p.shape[1], 0), (target, D, 0)],
                    [w_pgate, b_pgate, w_ple, g_final, g_ple],
                    [((T, D), F32, D, 0), ((T, D), BF16, D, 0), ((T, D), BF16, D, 0), ((T, D), BF16, D, 0)],
                    [(1, D), (1, D), (1, 128), (1, D)], tb, vmem=VMEM_LIMIT_LARGE)


def _me():
    return lax.axis_index("x"), lax.axis_index("y"), lax.axis_index("c")


def _chips3(x, y):
    return [(1 - x, y), (x, 1 - y), (1 - x, 1 - y)]


def _peers7(x, y, c):
    for m in range(1, 8):
        yield m - 1, (x ^ (m >> 2), y ^ ((m >> 1) & 1), c ^ (m & 1))


def _shard_of(ref, axis, s):
    R, C = ref.shape
    if axis == 1:
        return ref.at[:, pl.ds(s * (C // 4), C // 4)]
    return ref.at[pl.ds(s * (R // 4), R // 4), :]


def _region(ref, axis, shard, half):
    R, C = ref.shape
    if axis == 1:
        cs, hr = C // 4, R // 2
        return ref.at[pl.ds(half * hr, hr), pl.ds(shard * cs, cs)]
    hr = R // 8
    return ref.at[pl.ds(shard * 2 * hr + half * hr, hr), :]


def _gather_now(shard, small):
    R, C = shard.shape
    cs = small.shape[1]

    def body(shard_ref, small_ref, out_ref, small_out, send, recv, fsend, frecv, lsem):
        x, y, c = _me()
        me_s = 2 * x + y
        sibling = (x, y, 1 - c)
        chips = _chips3(x, y)
        half = shard_ref.at[pl.ds(c * (R // 2), R // 2), :]
        locals_ = [pltpu.make_async_copy(shard_ref, _shard_of(out_ref, 1, me_s), lsem.at[0]),
                   pltpu.make_async_copy(small_ref, _shard_of(small_out, 1, me_s), lsem.at[1])]
        for cp in locals_:
            cp.start()
        firsts = []
        for j, (px, py) in enumerate(chips):
            firsts.append(pltpu.make_async_remote_copy(
                src_ref=half, dst_ref=_region(out_ref, 1, me_s, c), send_sem=send.at[0, j], recv_sem=recv.at[0, j],
                device_id=(px, py, c), device_id_type=MESH))
            firsts.append(pltpu.make_async_remote_copy(
                src_ref=small_ref, dst_ref=_shard_of(small_out, 1, me_s), send_sem=send.at[1, j],
                recv_sem=recv.at[1, j], device_id=(px, py, c), device_id_type=MESH))
        for cp in firsts:
            cp.start()
        relays = []
        for j, (px, py) in enumerate(chips):
            landed = _region(out_ref, 1, 2 * px + py, c)
            pltpu.make_async_remote_copy(
                src_ref=half, dst_ref=landed, send_sem=send.at[0, j], recv_sem=recv.at[0, j],
                device_id=(px, py, c), device_id_type=MESH).wait_recv()
            relay = pltpu.make_async_remote_copy(
                src_ref=landed, dst_ref=landed, send_sem=fsend.at[j], recv_sem=frecv.at[j],
                device_id=sibling, device_id_type=MESH)
            relay.start()
            relays.append(relay)
        for j, (px, py) in enumerate(chips):
            pltpu.make_async_remote_copy(
                src_ref=small_ref, dst_ref=_shard_of(small_out, 1, 2 * px + py), send_sem=send.at[1, j],
                recv_sem=recv.at[1, j], device_id=(px, py, c), device_id_type=MESH).wait_recv()
            theirs = _region(out_ref, 1, 2 * px + py, 1 - c)
            pltpu.make_async_remote_copy(
                src_ref=theirs, dst_ref=theirs, send_sem=fsend.at[j], recv_sem=frecv.at[j],
                device_id=sibling, device_id_type=MESH).wait_recv()
        for cp in firsts + relays:
            cp.wait_send()
        for cp in locals_:
            cp.wait()

    return pl.pallas_call(
        body, name="gather_now", in_specs=[ANY, ANY], out_specs=[ANY, ANY],
        out_shape=[jax.ShapeDtypeStruct((R, 4 * C), shard.dtype),
                   jax.ShapeDtypeStruct((small.shape[0], 4 * cs), small.dtype)],
        scratch_shapes=[pltpu.SemaphoreType.DMA((2, 3)), pltpu.SemaphoreType.DMA((2, 3)),
                        pltpu.SemaphoreType.DMA((3,)), pltpu.SemaphoreType.DMA((3,)), pltpu.SemaphoreType.DMA((2,))],
        compiler_params=_params(),
    )(shard, small)


def _exchange_small(small):
    def body(small_ref, out_ref, send, recv, lsem):
        x, y, c = _me()
        me = 4 * x + 2 * y + c
        own = pltpu.make_async_copy(small_ref, out_ref.at[me], lsem)
        own.start()
        sends = [pltpu.make_async_remote_copy(
            src_ref=small_ref, dst_ref=out_ref.at[me], send_sem=send.at[m], recv_sem=recv.at[m],
            device_id=peer, device_id_type=MESH) for m, peer in _peers7(x, y, c)]
        for cp in sends:
            cp.start()
        for m, (px, py, pc) in _peers7(x, y, c):
            pltpu.make_async_remote_copy(
                src_ref=small_ref, dst_ref=out_ref.at[4 * px + 2 * py + pc], send_sem=send.at[m], recv_sem=recv.at[m],
                device_id=(px, py, pc), device_id_type=MESH).wait_recv()
        for cp in sends:
            cp.wait_send()
        own.wait()

    return pl.pallas_call(
        body, name="exchange_small", in_specs=[ANY], out_specs=ANY,
        out_shape=jax.ShapeDtypeStruct((8,) + small.shape, small.dtype),
        scratch_shapes=[pltpu.SemaphoreType.DMA((7,)), pltpu.SemaphoreType.DMA((7,)), pltpu.SemaphoreType.DMA(())],
        compiler_params=_params(),
    )(small)


def _sum_slots(name, slots, tr=256):
    S, R, C = slots.shape
    tr = _tile(R, tr, 16)

    def body(s_ref, o_ref):
        acc = s_ref[0].astype(F32)
        for s in range(1, S):
            acc = acc + s_ref[s].astype(F32)
        o_ref[...] = acc

    return pl.pallas_call(
        body, name=name, grid=(R // tr,), in_specs=[pl.BlockSpec((S, tr, C), lambda i: (0, i, 0))],
        out_specs=pl.BlockSpec((tr, C), lambda i: (i, 0)), out_shape=jax.ShapeDtypeStruct((R, C), F32),
        compiler_params=_params(("parallel",)),
    )(slots)


def _place_shard(name, shard, axis):
    R, C = shard.shape
    full_shape = (R, 4 * C) if axis == 1 else (4 * R, C)

    def body(shard_ref, full_ref, sem):
        x, y, c = _me()
        cp = pltpu.make_async_copy(shard_ref, _shard_of(full_ref, axis, 2 * x + y), sem)
        cp.start()
        cp.wait()

    return pl.pallas_call(
        body, name=name, in_specs=[ANY], out_specs=ANY, out_shape=jax.ShapeDtypeStruct(full_shape, shard.dtype),
        scratch_shapes=[pltpu.SemaphoreType.DMA(())], compiler_params=_params(),
    )(shard)


def _split_start(name, body, arrays, sem_shape, after=None):
    n = len(arrays)
    extra = [] if after is None else [after]

    def kernel_body(*refs):
        body(refs[:n], refs[n + len(extra)], refs[n + len(extra) + 1])
        token = refs[-1]
        token[...] = jnp.zeros_like(token)

    sems = pltpu.SemaphoreType.DMA(sem_shape)
    return pl.pallas_call(
        kernel_body, name=name,
        out_shape=(sems, sems, *[pltpu.HBM(a.shape, a.dtype) for a in arrays], jax.ShapeDtypeStruct(TOKEN, F32)),
        in_specs=(HBM,) * n + (ANY,) * len(extra),
        out_specs=(SEM, SEM) + (HBM,) * n + (pl.BlockSpec(memory_space=pltpu.VMEM),),
        input_output_aliases={k: 2 + k for k in range(n)}, compiler_params=pltpu.CompilerParams(has_side_effects=EFFECT),
    )(*[pltpu.with_memory_space_constraint(a, pltpu.HBM) for a in arrays], *extra)


def _split_wait(name, body, started, after):
    send, recv, *arrays = started[:-1]
    n = len(arrays)

    def kernel_body(*refs):
        body(refs[:n], refs[n], refs[n + 1])

    return pl.pallas_call(
        kernel_body, name=name, out_shape=tuple(pltpu.HBM(a.shape, a.dtype) for a in arrays),
        in_specs=(HBM,) * n + (SEM, SEM, ANY), out_specs=(HBM,) * n, input_output_aliases={k: k for k in range(n)},
        compiler_params=pltpu.CompilerParams(has_side_effects=EFFECT),
    )(*arrays, send, recv, after)


def _gather_copies(refs, send, recv, axes, landing):
    n = len(axes)
    x, y, c = _me()
    copies = []
    for k in range(n):
        for j, (px, py) in enumerate(_chips3(x, y)):
            s = 2 * px + py if landing else 2 * x + y
            copies.append(pltpu.make_async_remote_copy(
                src_ref=refs[k], dst_ref=_shard_of(refs[n + k], axes[k], s), send_sem=send.at[3 * k + j],
                recv_sem=recv.at[3 * k + j], device_id=(px, py, c), device_id_type=MESH))
    return copies


def _gather_start(name, shards, fulls, axes, after):
    def body(refs, send, recv):
        for cp in _gather_copies(refs, send, recv, axes, False):
            cp.start()

    return _split_start(name, body, list(shards) + list(fulls), (3 * len(axes),), after)


def _gather_wait(name, started, axes, after):
    def body(refs, send, recv):
        for cp in _gather_copies(refs, send, recv, axes, True):
            cp.wait_send()
            cp.wait_recv()

    return _split_wait(name, body, started, after)[len(axes):]


def _piece_shape(shape, axis):
    R, C = shape
    return (R // 2, C // 4) if axis == 1 else (R // 8, C)


def _scatter_copies(refs, send, recv, axes):
    n = len(axes)
    x, y, c = _me()
    return [pltpu.make_async_remote_copy(
        src_ref=_region(refs[k], axes[k], 2 * px + py, pc), dst_ref=refs[n + k].at[m], send_sem=send.at[7 * k + m],
        recv_sem=recv.at[7 * k + m], device_id=(px, py, pc), device_id_type=MESH)
        for k in range(n) for m, (px, py, pc) in _peers7(x, y, c)]


def _scatter_start(name, gs, axes):
    def body(refs, send, recv):
        for cp in _scatter_copies(refs, send, recv, axes):
            cp.start()

    lands = [lax.empty((7,) + _piece_shape(g.shape, ax), g.dtype) for g, ax in zip(gs, axes)]
    return _split_start(name, body, list(gs) + lands, (7 * len(axes),))


def _scatter_wait(name, started, axes, after):
    def body(refs, send, recv):
        for cp in _scatter_copies(refs, send, recv, axes):
            cp.wait_send()
            cp.wait_recv()

    out = _split_wait(name, body, started, after)
    return out[:len(axes)], out[len(axes):]


def _own_piece(g, axis):
    x, y, c = _me()
    pr, pc_ = _piece_shape(g.shape, axis)
    if axis == 1:
        return lax.dynamic_slice(g, (c * pr, (2 * x + y) * pc_), (pr, pc_))
    return lax.dynamic_slice(g, ((2 * x + y) * 2 * pr + c * pr, 0), (pr, pc_))


def _sum_pieces(name, own, slots, tr=256):
    S, R, C = slots.shape
    tr = _tile(R, tr, 16)
    nblk = R // tr
    c_arr = lax.axis_index("c").astype(jnp.int32).reshape(1)

    def body(c_ref, own_ref, s_ref, o_ref):
        acc = own_ref[...].astype(F32)
        for s in range(S):
            acc = acc + s_ref[s].astype(F32)
        o_ref[...] = acc

    grid_spec = pltpu.PrefetchScalarGridSpec(
        num_scalar_prefetch=1, grid=(nblk,),
        in_specs=[pl.BlockSpec((tr, C), lambda i, c_ref: (i, 0)), pl.BlockSpec((S, tr, C), lambda i, c_ref: (0, i, 0))],
        out_specs=pl.BlockSpec((tr, C), lambda i, c_ref: (c_ref[0] * nblk + i, 0)))
    return pl.pallas_call(
        body, name=name, grid_spec=grid_spec, out_shape=jax.ShapeDtypeStruct((2 * R, C), F32),
        compiler_params=_params(("parallel",)),
    )(c_arr, own, slots)


def _half_copies(refs, send, recv, mine):
    x, y, c = _me()
    copies = []
    for k, ref in enumerate(refs):
        hr = ref.shape[0] // 2
        rows = ref.at[pl.ds((c if mine else 1 - c) * hr, hr), :]
        copies.append(pltpu.make_async_remote_copy(
            src_ref=rows, dst_ref=rows, send_sem=send.at[k], recv_sem=recv.at[k], device_id=(x, y, 1 - c),
            device_id_type=MESH))
    return copies


def _join_start(name, bufs):
    def body(refs, send, recv):
        for cp in _half_copies(refs, send, recv, True):
            cp.start()

    return _split_start(name, body, list(bufs), (len(bufs),))


def _join_wait(name, started, after):
    def body(refs, send, recv):
        for cp in _half_copies(refs, send, recv, False):
            cp.wait_send()
            cp.wait_recv()

    return _split_wait(name, body, started, after)


def _adamw(name, w, g, m, v, tr=256, deps=()):
    R, C = w.shape
    tr = _tile(R, tr, 8)
    c1 = 1.0 - ADAM_B1 ** ADAM_STEP
    c2 = 1.0 - ADAM_B2 ** ADAM_STEP

    def body(w_ref, g_ref, m_ref, v_ref, *rest):
        d_ref, nm_ref, nv_ref, go_ref = rest[len(deps):]
        gv = g_ref[...]
        go_ref[...] = gv
        nm = ADAM_B1 * m_ref[...] + (1.0 - ADAM_B1) * gv
        nv = ADAM_B2 * v_ref[...] + (1.0 - ADAM_B2) * jnp.square(gv)
        d_ref[...] = -ADAM_LR * ((nm / c1) / (jnp.sqrt(nv / c2) + ADAM_EPS) + ADAM_WD * w_ref[...])
        nm_ref[...] = nm
        nv_ref[...] = nv

    spec = pl.BlockSpec((tr, C), lambda i: (i, 0))
    return pl.pallas_call(
        body, name=name, grid=(R // tr,), in_specs=[spec] * 4 + [pl.BlockSpec(TOKEN, lambda i: (0, 0))] * len(deps),
        out_specs=[spec] * 4, out_shape=[jax.ShapeDtypeStruct((R, C), F32)] * 4,
        compiler_params=_params(("parallel",)),
    )(w, g, m, v, *deps)


def kernel(x, p, g_mix, w_in, w_dw, b_dw, g_conv_ln, b_conv_ln, w_out, g_ffn, w_gate, w_up, w_down, g_ple, w_pgate, b_pgate, w_ple, g_final, loss_target, m_g_mix, m_w_in, m_w_dw, m_b_dw, m_g_conv_ln, m_b_conv_ln, m_w_out, m_g_ffn, m_w_gate, m_w_up, m_w_down, m_g_ple, m_w_pgate, m_b_pgate, m_w_ple, m_g_final, v_g_mix, v_w_in, v_w_dw, v_b_dw, v_g_conv_ln, v_b_conv_ln, v_w_out, v_g_ffn, v_w_gate, v_w_up, v_w_down, v_g_ple, v_w_pgate, v_b_pgate, v_w_ple, v_g_final):
    T, D = x.shape[1], x.shape[2]
    DC = b_dw.shape[1]
    DA = D - DC
    H = DA // HEAD_DIM
    DP = p.shape[3]
    assert T % SUPER == 0 and DA == DC
    x2 = x.reshape(T, D)
    p2 = p.reshape(T, DP)
    tgt = loss_target.reshape(T, D)
    g_final2 = g_final.reshape(1, D)

    big = dict(w_in=(w_in[0], 1), w_out=(w_out[0], 0), w_gate=(w_gate[0], 1), w_up=(w_up[0], 1),
               w_down=(w_down[0], 0), w_pgate=(w_pgate[0], 0), w_ple=(w_ple[0], 1))
    axis_of = {k: big[k][1] for k in big}
    dw_shard = jnp.pad(w_dw.reshape(CONV_W, -1), ((0, HALO - CONV_W), (0, 0)))
    w_in_full, w_dw_full = _gather_now(big["w_in"][0].astype(BF16), dw_shard)
    later = ["w_out", "w_gate", "w_up", "w_down", "w_pgate", "w_ple"]
    shard16 = {k: big[k][0].astype(BF16) for k in later}
    placed = {k: _place_shard("place_" + k, shard16[k], axis_of[k]) for k in later}
    travelling = {}

    def fetch(groups, after):
        for keys in groups:
            travelling[keys] = _gather_start("gather_start_" + keys[0], [shard16[k] for k in keys],
                                             [placed[k] for k in keys], [axis_of[k] for k in keys], after)
        return [travelling[keys][-1] for keys in groups]

    def weights(keys, after):
        return _gather_wait("gather_wait_" + keys[0], travelling[keys], [axis_of[k] for k in keys], after)

    issued = fetch([("w_out",), ("w_gate", "w_up")], w_in_full)

    a = _rms_fwd("rms_mix", x2, g_mix, deps=issued)
    z = _matmul("mm_in", [[(a, w_in_full)]], "nn", [F32], _plain)[0]
    cat, o_attn, lse = _attn_fwd(z, D, H)
    cat, conv_c = _conv_fwd(z, cat, w_dw_full, b_dw, g_conv_ln, b_conv_ln, 3 * DA // DC)
    W = dict(w_in=w_in_full)
    W["w_out"], = weights(("w_out",), cat)
    def residual_and_norm(accs, extras):
        h = accs[0] + extras[0]
        r = lax.rsqrt(jnp.mean(h * h, axis=-1, keepdims=True) + EPS)
        return h, (h * r) * extras[1]

    h1, f = _matmul("mm_out", [[(cat, W["w_out"])]], "nn", [F32, BF16], residual_and_norm,
                    extras=[(x2, "mn"), (g_ffn, "n")], tm=512, tn=D, sub=256)
    W["w_gate"], W["w_up"] = weights(("w_gate", "w_up"), f)

    def swiglu(accs, extras):
        gt, up = accs
        return gt, up, (gt * jax.nn.sigmoid(gt)) * up

    gate, up, act = _matmul("mm_gate_up", [[(f, W["w_gate"])], [(f, W["w_up"])]], "nn", [BF16, BF16, BF16],
                            swiglu, tm=1024, tn=512, sub=256,
                            deps=fetch([("w_down",), ("w_pgate", "w_ple")], W["w_up"]))
    W["w_down"], = weights(("w_down",), act)
    h2 = _matmul("mm_down", [[(act, W["w_down"])]], "nn", [F32], _add_residual, extras=[(h1, "mn")], tm=512, tk=5632)[0]
    n2 = _rms_fwd("rms_ple", h2, g_ple)
    W["w_pgate"], W["w_ple"] = weights(("w_pgate", "w_ple"), n2)
    sent, joined = {}, {}

    def send_grads(keys, gs):
        sent[keys] = _scatter_start("scatter_start_" + keys[0], gs, [axis_of[k] for k in keys])
        return sent[keys][-1]

    def reduce_grads(groups, after):
        keys_all, halves = (), []
        for keys in groups:
            axes = [axis_of[k] for k in keys]
            g_thru, lands = _scatter_wait("scatter_wait_" + keys[0], sent[keys], axes, after)
            halves += [_sum_pieces("sum_" + k, _own_piece(g, ax), land)
                       for k, g, ax, land in zip(keys, g_thru, axes, lands)]
            keys_all += keys
        joined[keys_all] = _join_start("join_start_" + keys_all[0], halves)
        return joined[keys_all][-1]

    dh2, dh2_16, de, du4, dg_final, db_pgate, loss_part, dg_ple = _ple_loss_head(
        n2, W["w_pgate"], b_pgate, h2, p2, W["w_ple"], tgt, g_final2, g_ple)
    t_ple = send_grads(("w_ple", "w_pgate"),
                       [_matmul("mm_dw_ple", [[(p2, de)]], "tn", [BF16], _plain, tk=1024)[0],
                        _matmul("mm_dw_pgate", [[(n2, du4)]], "tn", [BF16], _plain, tk=2048)[0]])
    t_down = send_grads(("w_down",), [_matmul("mm_dw_down", [[(act, dh2_16)]], "tn", [BF16], _plain, tm=1408, tn=2048,
                                              tk=1024, deps=[t_ple])[0]])

    def swiglu_bwd(accs, extras):
        gt, up = extras[0].astype(F32), extras[1].astype(F32)
        sg = jax.nn.sigmoid(gt)
        dact = accs[0]
        return dact * up * (sg * (1.0 + gt * (1.0 - sg))), dact * (gt * sg)

    dgate, dup = _matmul("mm_dact", [[(dh2_16, W["w_down"])]], "nt", [BF16, BF16], swiglu_bwd,
                         extras=[(gate, "mn"), (up, "mn")], tm=2048, tn=512, sub=256, deps=[t_down])
    t_up = send_grads(("w_gate", "w_up"),
                      [_matmul("mm_dw_gate", [[(f, dgate)]], "tn", [BF16], _plain, tn=1408, tk=2048)[0],
                       _matmul("mm_dw_up", [[(f, dup)]], "tn", [BF16], _plain, tn=1408, tk=2048)[0]])
    df = _matmul("mm_df", [[(dgate, W["w_gate"]), (dup, W["w_up"])]], "nt", [F32], _plain, tm=1024, tn=256,
                 tk=5632, deps=[t_up], vmem=VMEM_LIMIT_LARGE)[0]
    dh1, dh1_16, dg_ffn = _rms_bwd("rms_ffn_bwd", df, h1, g_ffn, dh2)
    t_out = send_grads(("w_out",), [_matmul("mm_dw_out", [[(cat, dh1_16)]], "tn", [BF16], _plain, tk=2048)[0]])
    dcat = _matmul("mm_dcat", [[(dh1_16, W["w_out"])]], "nt", [F32], _plain, deps=[t_out])[0]
    j1 = reduce_grads([("w_ple", "w_pgate"), ("w_down",)], dcat)
    dc, dg_ln, db_ln, db_dw = _conv_ln_bwd(dcat, conv_c, g_conv_ln, b_conv_ln, deps=[j1])
    dcv, dcg, dw_dw = _conv_bwd(z, dc, w_dw_full, 3 * DA // DC)
    dq, dk, dv = _attn_bwd(z, dcat, o_attn, lse, H)
    dz = [dq, dk, dv, dcv, dcg]
    dw_in = jnp.concatenate([_matmul("mm_dw_in%d" % n, [[(a, part)]], "tn", [BF16], _plain, tk=2048)[0]
                             for n, part in enumerate(dz)], axis=1)
    t_in = send_grads(("w_in",), [dw_in])
    j2 = reduce_grads([("w_gate", "w_up"), ("w_out",)], dw_in)
    da = _matmul("mm_da", [[(part, W["w_in"], n) for n, part in enumerate(dz)]], "nt", [F32], _plain, tm=512,
                 tk=DA, deps=[t_in, j2])[0]
    grad_x, _, dg_mix = _rms_bwd("rms_mix_bwd", da, x2, g_mix, dh1)

    wide = [dg_mix, dg_ffn, dg_ple, db_pgate, dg_final,
            jnp.concatenate([db_dw, dg_ln], axis=1), jnp.concatenate([db_ln, jnp.zeros_like(db_ln)], axis=1),
            jnp.pad(loss_part, ((0, 0), (0, D - 128))),
            dw_dw.reshape(HALO * DC // D, D)]
    small = jnp.concatenate(wide, axis=0)
    small = jnp.pad(small, ((0, -small.shape[0] % 8), (0, 0)))
    small_sum = _sum_slots("sum_small", _exchange_small(small))
    j3 = reduce_grads([("w_in",)], small_sum)

    grads, deltas, new_m, new_v = {}, {}, {}, {}
    moments = dict(w_in=(m_w_in, v_w_in), w_out=(m_w_out, v_w_out), w_gate=(m_w_gate, v_w_gate),
                   w_up=(m_w_up, v_w_up), w_down=(m_w_down, v_w_down), w_pgate=(m_w_pgate, v_w_pgate),
                   w_ple=(m_w_ple, v_w_ple))
    last, deps = small_sum, [j3]
    for keys in list(joined):
        for k, g_k in zip(keys, _join_wait("join_wait_" + keys[0], joined[keys], last)):
            d_, m_, v_, g_ = _adamw("adamw_" + k, big[k][0], g_k, moments[k][0][0], moments[k][1][0], deps=deps)
            grads[k], deltas[k], new_m[k], new_v[k] = g_[None], d_[None], m_[None], v_[None]
            last, deps = d_, ()

    half = lambda r, lo: small_sum[r:r + 1, lo * DC:(lo + 1) * DC]
    vec = dict(g_mix=small_sum[0:1], g_ffn=small_sum[1:2], g_ple=small_sum[2:3], b_pgate=small_sum[3:4],
               g_final=small_sum[4:5], b_dw=half(5, 0), g_conv_ln=half(5, 1), b_conv_ln=half(6, 0))
    loss = small_sum[7, 0]
    dw_dw_sum = small_sum[8:8 + HALO * DC // D].reshape(HALO, DC)
    s_me = 2 * lax.axis_index("x") + lax.axis_index("y")
    cs = w_dw.shape[3]
    vec["w_dw"] = lax.dynamic_slice(dw_dw_sum, (0, s_me * cs), (CONV_W, cs))
    small_w = dict(g_mix=(g_mix, m_g_mix, v_g_mix), g_ffn=(g_ffn, m_g_ffn, v_g_ffn), g_ple=(g_ple, m_g_ple, v_g_ple),
                   b_pgate=(b_pgate, m_b_pgate, v_b_pgate), g_final=(g_final, m_g_final, v_g_final),
                   b_dw=(b_dw, m_b_dw, v_b_dw), g_conv_ln=(g_conv_ln, m_g_conv_ln, v_g_conv_ln),
                   b_conv_ln=(b_conv_ln, m_b_conv_ln, v_b_conv_ln), w_dw=(w_dw, m_w_dw, v_w_dw))
    for k, (w_, m_, v_) in small_w.items():
        shape = w_.shape
        g2 = vec[k]
        to2 = lambda t: t.reshape(g2.shape)
        d_, nm_, nv_, g_ = _adamw("adamw_" + k, to2(w_), g2, to2(m_), to2(v_))
        grads[k], deltas[k], new_m[k], new_v[k] = (t.reshape(shape) for t in (g_, d_, nm_, nv_))

    order = ["g_mix", "w_in", "w_dw", "b_dw", "g_conv_ln", "b_conv_ln", "w_out", "g_ffn", "w_gate", "w_up", "w_down",
             "g_ple", "w_pgate", "b_pgate", "w_ple", "g_final"]
    return (loss, grad_x.reshape(x.shape), *[grads[k] for k in order], *[deltas[k] for k in order],
            *[new_m[k] for k in order], *[new_v[k] for k in order])
```

```python
import functools

import jax
import jax.numpy as jnp
from jax import lax
from jax.experimental import pallas as pl
from jax.experimental.pallas import tpu as pltpu

F32 = jnp.float32
BF16 = jnp.bfloat16

EPS = 1e-6
HEAD_DIM = 128
BLK = 128
DILATIONS = (1, 4, 16)
SUPER = BLK * DILATIONS[-1]
CONV_W = 31
HALO = 32
ADAM_LR, ADAM_B1, ADAM_B2, ADAM_EPS, ADAM_WD, ADAM_STEP = 0.001, 0.9, 0.999, 1e-08, 0.01, 10

V7X_VMEM_BYTES = 64 * 1024 * 1024
VMEM_LIMIT = V7X_VMEM_BYTES * 3 // 4
VMEM_LIMIT_LARGE = V7X_VMEM_BYTES * 15 // 16
MESH = pl.DeviceIdType.MESH
ANY = pl.BlockSpec(memory_space=pl.ANY)
HBM = pl.BlockSpec(memory_space=pltpu.HBM)
SEM = pl.BlockSpec(memory_space=pltpu.SEMAPHORE)
EFFECT = pltpu.SideEffectType.DATAFLOW_SIDE_EFFECTING
TOKEN = (8, 128)


def _params(semantics=None, vmem=VMEM_LIMIT, **kw):
    return pltpu.CompilerParams(dimension_semantics=semantics, vmem_limit_bytes=vmem, **kw)


def _tile(n, want, mult=128):
    if n <= want:
        return n
    for t in range(want - want % mult, 0, -mult):
        if n % t == 0:
            return t
    raise ValueError((n, want, mult))


_DIMS = {"nn": ((1,), (0,)), "nt": ((1,), (1,)), "tn": ((0,), (0,))}


def _matmul(name, groups, mode, out_dtypes, epilogue, extras=(), tm=1024, tn=1024, tk=2048, sub=None, deps=(),
            vmem=VMEM_LIMIT):
    a0, b0 = groups[0][0][:2]
    if mode == "nn":
        (M, K), N = a0.shape, b0.shape[1]
    elif mode == "nt":
        (M, K), N = a0.shape, b0.shape[0]
    else:
        (K, M), N = a0.shape, b0.shape[1]
    tm, tn, tk = _tile(M, tm), _tile(N, tn), _tile(K, tk)
    nk = K // tk
    if mode == "tn":
        a_spec = pl.BlockSpec((tk, tm), lambda i, j, k: (k, i))
    else:
        a_spec = pl.BlockSpec((tm, tk), lambda i, j, k: (i, k))
    operands, in_specs = [], []
    for grp in groups:
        for a, b, *k0 in grp:
            k0 = k0[0] if k0 else 0
            if mode == "nt":
                b_spec = pl.BlockSpec((tn, tk), functools.partial(lambda i, j, k, k0: (j, k + k0), k0=k0))
            else:
                b_spec = pl.BlockSpec((tk, tn), functools.partial(lambda i, j, k, k0: (k + k0, j), k0=k0))
            operands += [a, b]
            in_specs += [a_spec, b_spec]
    for arr, kind in extras:
        operands.append(arr)
        if kind == "mn":
            in_specs.append(pl.BlockSpec((tm, tn), lambda i, j, k: (i, j)))
        else:
            in_specs.append(pl.BlockSpec((1, tn), lambda i, j, k: (0, j)))
    for tok in deps:
        operands.append(tok)
        in_specs.append(pl.BlockSpec(TOKEN, lambda i, j, k: (0, 0)))
    n_pairs = [len(g) for g in groups]
    n_ex, n_out, n_grp, n_dep = len(extras), len(out_dtypes), len(groups), len(deps)
    out_shape = [jax.ShapeDtypeStruct((M, N), dt) for dt in out_dtypes]
    out_specs = [pl.BlockSpec((tm, tn), lambda i, j, k: (i, j)) for _ in out_dtypes]

    kinds = [kind for _, kind in extras]
    sub = tm if (sub is None or nk > 1 or mode == "tn") else sub
    assert tm % sub == 0

    def body(*refs):
        ex_refs = refs[2 * sum(n_pairs):2 * sum(n_pairs) + n_ex]
        pos = 2 * sum(n_pairs) + n_ex + n_dep
        out_refs = refs[pos:pos + n_out]
        acc_refs = refs[pos + n_out:]

        def products(rows):
            pos, parts = 0, []
            for g in range(n_grp):
                part = None
                for _ in range(n_pairs[g]):
                    a_ref, b_ref = refs[pos], refs[pos + 1]
                    pos += 2
                    d = lax.dot_general(a_ref[rows].astype(BF16), b_ref[...].astype(BF16),
                                        (_DIMS[mode], ((), ())), preferred_element_type=F32)
                    part = d if part is None else part + d
                parts.append(part)
            return parts

        def finish(accs, rows):
            outs = epilogue(accs, [e[rows] if kind == "mn" else e[...] for e, kind in zip(ex_refs, kinds)])
            for o_ref, o in zip(out_refs, outs):
                o_ref[rows] = o.astype(o_ref.dtype)

        if nk == 1:
            for r in range(tm // sub):
                rows = (pl.ds(r * sub, sub), slice(None)) if sub < tm else (slice(None), slice(None))
                finish(products(rows), rows)
        else:
            k = pl.program_id(2)
            parts = products((slice(None), slice(None)))

            @pl.when(k == 0)
            def _():
                for acc, part in zip(acc_refs, parts):
                    acc[...] = part

            @pl.when(k > 0)
            def _():
                for acc, part in zip(acc_refs, parts):
                    acc[...] += part

            @pl.when(k == nk - 1)
            def _():
                finish([acc[...] for acc in acc_refs], (slice(None), slice(None)))

    scratch = [pltpu.VMEM((tm, tn), F32) for _ in range(n_grp)] if nk > 1 else []
    return pl.pallas_call(
        body, name=name, grid=(M // tm, N // tn, nk),
        in_specs=in_specs, out_specs=out_specs, out_shape=out_shape, scratch_shapes=scratch,
        compiler_params=_params(("parallel", "parallel", "arbitrary"), vmem),
    )(*operands)


def _plain(accs, extras):
    return (accs[0],)


def _add_residual(accs, extras):
    return (accs[0] + extras[0],)


def _rowwise(name, fn, rows, consts, row_outs, acc_outs, tb, deps=(), vmem=VMEM_LIMIT):
    T = rows[0][0].shape[0]
    tb = _tile(T, tb, 16)
    operands = [r[0] for r in rows] + list(consts) + list(deps)
    in_specs = [pl.BlockSpec((tb, c), functools.partial(lambda i, cb: (i, cb), cb=cb)) for _, c, cb in rows]
    in_specs += [pl.BlockSpec(c.shape, functools.partial(lambda i, nd: (0,) * nd, nd=c.ndim)) for c in consts]
    in_specs += [pl.BlockSpec(TOKEN, lambda i: (0, 0)) for _ in deps]
    out_shape = [jax.ShapeDtypeStruct(s, dt) for s, dt, _, _ in row_outs]
    out_specs = [pl.BlockSpec((tb, c), functools.partial(lambda i, cb: (i, cb), cb=cb)) for _, _, c, cb in row_outs]
    out_shape += [jax.ShapeDtypeStruct(s, F32) for s in acc_outs]
    out_specs += [pl.BlockSpec(s, functools.partial(lambda i, nd: (0,) * nd, nd=len(s))) for s in acc_outs]
    n_rows, n_consts, n_ro, n_dep = len(rows), len(consts), len(row_outs), len(deps)

    def body(*refs):
        row_refs = refs[:n_rows]
        const_refs = refs[n_rows:n_rows + n_consts]
        out_refs = refs[n_rows + n_consts + n_dep:]
        ro, ao = fn([r[...] for r in row_refs], [c[...] for c in const_refs])
        for o_ref, o in zip(out_refs[:n_ro], ro):
            o_ref[...] = o.astype(o_ref.dtype)
        if acc_outs:
            i = pl.program_id(0)

            @pl.when(i == 0)
            def _():
                for a_ref, a in zip(out_refs[n_ro:], ao):
                    a_ref[...] = a

            @pl.when(i > 0)
            def _():
                for a_ref, a in zip(out_refs[n_ro:], ao):
                    a_ref[...] += a

    return pl.pallas_call(
        body, name=name, grid=(T // tb,), in_specs=in_specs, out_specs=out_specs, out_shape=out_shape,
        compiler_params=_params(("arbitrary",) if acc_outs else ("parallel",), vmem),
    )(*operands)


def _colsum(v):
    return jnp.sum(v, axis=0, keepdims=True)


def _rms_fwd(name, x, g, tb=512, deps=()):
    T, D = x.shape

    def fn(rows, consts):
        xv, gv = rows[0], consts[0]
        r = lax.rsqrt(jnp.mean(xv * xv, axis=-1, keepdims=True) + EPS)
        return ((xv * r) * gv,), ()

    return _rowwise(name, fn, [(x, D, 0)], [g], [((T, D), BF16, D, 0)], [], tb, deps=deps)[0]


def _rms_bwd(name, dy, x, g, resid, tb=256):
    T, D = x.shape

    def fn(rows, consts):
        dyv, xv, rv = rows
        gv = consts[0]
        r = lax.rsqrt(jnp.mean(xv * xv, axis=-1, keepdims=True) + EPS)
        n = xv * r
        dn = dyv * gv
        dx = r * (dn - n * jnp.mean(dn * n, axis=-1, keepdims=True))
        tot = rv + dx
        return (tot, tot), (_colsum(dyv * n),)

    return _rowwise(name, fn, [(dy, D, 0), (x, D, 0), (resid, D, 0)], [g],
                    [((T, D), F32, D, 0), ((T, D), BF16, D, 0)], [(1, D)], tb)


NBLK = SUPER // BLK


def _classes(ref, dil, rows, start=0, dtype=None):
    parts = [ref[pl.ds(start + r, rows, stride=dil), :] if dil > 1 else ref[pl.ds(start, rows), :]
             for r in range(dil)]
    if dtype is not None:
        parts = [p.astype(dtype) for p in parts]
    return parts


def _keys_with_prev(ref, prev_ref, dil):
    L = SUPER // dil
    own = _classes(ref, dil, L, dtype=BF16)
    last = _classes(prev_ref, dil, BLK, start=SUPER - BLK * dil, dtype=BF16)
    blocks = []
    for r in range(dil):
        ext = jnp.concatenate([last[r], own[r]], axis=0)
        blocks += [ext[j * BLK:(j + 2) * BLK] for j in range(L // BLK)]
    return jnp.stack(blocks, axis=0)


def _band_mask(dil, has_prev):
    qi = lax.broadcasted_iota(jnp.int32, (BLK, 2 * BLK), 0)
    kj = lax.broadcasted_iota(jnp.int32, (BLK, 2 * BLK), 1)
    own = jnp.logical_and(kj >= BLK, kj - BLK <= qi)
    prev = jnp.logical_and(kj < BLK, kj >= qi)
    b = lax.broadcasted_iota(jnp.int32, (NBLK, 1, 1), 0)
    first = (b & (SUPER // (BLK * dil) - 1)) == 0
    prev_ok = jnp.logical_or(jnp.logical_not(first), has_prev)
    return jnp.logical_or(own[None], jnp.logical_and(prev[None], prev_ok))


def _bdot(a, b, ca, cb):
    return lax.dot_general(a, b, (((ca,), (cb,)), ((0,), (0,))), preferred_element_type=F32)


def _put_classes(dst, value, dil, rows, start=0, add=False, src_start=0, src_stride=None):
    src_stride = rows if src_stride is None else src_stride
    for r in range(dil):
        idx = (pl.ds(start + r, rows, stride=dil) if dil > 1 else pl.ds(start, rows), slice(None))
        part = value[src_start + r * src_stride:src_start + r * src_stride + rows]
        dst[idx] = dst[idx] + part if add else part


def _attn_fwd(z, cat_width, n_heads):
    T = z.shape[0]
    H = n_heads
    DA = H * HEAD_DIM
    nb = T // SUPER
    scale = HEAD_DIM ** -0.5

    def body(q_ref, k_ref, v_ref, kp_ref, vp_ref, cat_ref, o_ref, lse_ref, ob, lb):
        has_prev = pl.program_id(1) > 0
        for b, dil in enumerate(DILATIONS):
            L = SUPER // dil
            q3 = jnp.concatenate(_classes(q_ref, dil, L, dtype=BF16), axis=0).reshape(NBLK, BLK, HEAD_DIM)
            k3 = _keys_with_prev(k_ref, kp_ref, dil)
            v3 = _keys_with_prev(v_ref, vp_ref, dil)
            s = jnp.where(_band_mask(dil, has_prev), _bdot(q3, k3, 2, 2) * scale, -jnp.inf)
            m = jnp.max(s, axis=-1, keepdims=True)
            e = jnp.exp(s - m)
            den = jnp.sum(e, axis=-1, keepdims=True)
            o3 = _bdot((e * (1.0 / den)).astype(BF16), v3, 2, 1)
            lse3 = jnp.broadcast_to(m + jnp.log(den), (NBLK, BLK, HEAD_DIM))
            _put_classes(ob.at[b], o3.reshape(SUPER, HEAD_DIM), dil, L)
            _put_classes(lb.at[b], lse3.reshape(SUPER, HEAD_DIM), dil, L)
        l0, l1, l2 = lb[0], lb[1], lb[2]
        mx = jnp.maximum(jnp.maximum(l0, l1), l2)
        tot = mx + jnp.log(jnp.exp(l0 - mx) + jnp.exp(l1 - mx) + jnp.exp(l2 - mx))
        o = jnp.exp(l0 - tot) * ob[0] + jnp.exp(l1 - tot) * ob[1] + jnp.exp(l2 - tot) * ob[2]
        o_ref[...] = o
        cat_ref[...] = o.astype(BF16)
        lse_ref[...] = tot

    blk = (SUPER, HEAD_DIM)
    in_specs = [
        pl.BlockSpec(blk, lambda h, n: (n, h)),
        pl.BlockSpec(blk, lambda h, n: (n, H + h)),
        pl.BlockSpec(blk, lambda h, n: (n, 2 * H + h)),
        pl.BlockSpec(blk, lambda h, n: (jnp.maximum(n - 1, 0), H + h)),
        pl.BlockSpec(blk, lambda h, n: (jnp.maximum(n - 1, 0), 2 * H + h)),
    ]
    out_spec = pl.BlockSpec(blk, lambda h, n: (n, h))
    return pl.pallas_call(
        body, name="attn_fwd", grid=(H, nb), in_specs=in_specs, out_specs=[out_spec] * 3,
        out_shape=[jax.ShapeDtypeStruct((T, cat_width), BF16), jax.ShapeDtypeStruct((T, DA), F32),
                   jax.ShapeDtypeStruct((T, DA), F32)],
        scratch_shapes=[pltpu.VMEM((3, SUPER, HEAD_DIM), F32), pltpu.VMEM((3, SUPER, HEAD_DIM), F32)],
        compiler_params=_params(("parallel", "parallel")),
    )(z, z, z, z, z)


def _attn_bwd(z, dcat, o, lse, n_heads):
    T = z.shape[0]
    H = n_heads
    nb = T // SUPER
    scale = HEAD_DIM ** -0.5

    def body(q_ref, k_ref, v_ref, kp_ref, vp_ref, do_ref, o_ref, lse_ref, dz_q, dz_k, dz_v,
             dq_acc, dk_acc, dv_acc, dkp_acc, dvp_acc, dsum):
        i = pl.program_id(1)
        has_prev = i < nb - 1

        @pl.when(i == 0)
        def _():
            dk_acc[...] = jnp.zeros_like(dk_acc)
            dv_acc[...] = jnp.zeros_like(dv_acc)

        @pl.when(i > 0)
        def _():
            dk_acc[...] = dkp_acc[...]
            dv_acc[...] = dvp_acc[...]

        dq_acc[...] = jnp.zeros_like(dq_acc)
        dkp_acc[...] = jnp.zeros_like(dkp_acc)
        dvp_acc[...] = jnp.zeros_like(dvp_acc)
        dsum[...] = jnp.broadcast_to(jnp.sum(do_ref[...] * o_ref[...], axis=-1, keepdims=True), (SUPER, HEAD_DIM))
        for dil in DILATIONS:
            L = SUPER // dil
            wide = lambda ref: jnp.tile(jnp.concatenate(_classes(ref, dil, L), axis=0).reshape(NBLK, BLK, HEAD_DIM),
                                        (1, 1, 2))
            q3 = jnp.concatenate(_classes(q_ref, dil, L, dtype=BF16), axis=0).reshape(NBLK, BLK, HEAD_DIM)
            do3 = jnp.concatenate(_classes(do_ref, dil, L, dtype=BF16), axis=0).reshape(NBLK, BLK, HEAD_DIM)
            k3 = _keys_with_prev(k_ref, kp_ref, dil)
            v3 = _keys_with_prev(v_ref, vp_ref, dil)
            p = jnp.where(_band_mask(dil, has_prev), jnp.exp(_bdot(q3, k3, 2, 2) * scale - wide(lse_ref)), 0.0)
            ds = (p * (_bdot(do3, v3, 2, 2) - wide(dsum)) * scale).astype(BF16)
            dq = _bdot(ds, k3, 2, 1).reshape(SUPER, HEAD_DIM)
            dk = _bdot(ds, q3, 1, 1)
            dv = _bdot(p.astype(BF16), do3, 1, 1)
            _put_classes(dq_acc, dq, dil, L, add=True)
            for acc, prev_acc, g in ((dk_acc, dkp_acc, dk), (dv_acc, dvp_acc, dv)):
                _put_classes(acc, g[:, BLK:, :].reshape(SUPER, HEAD_DIM), dil, L, add=True)
                to_prev = g[:, :BLK, :].reshape(SUPER, HEAD_DIM)
                if L > BLK:
                    _put_classes(acc, to_prev, dil, L - BLK, add=True, src_start=BLK, src_stride=L)
                _put_classes(prev_acc, to_prev, dil, BLK, start=SUPER - BLK * dil, add=True, src_stride=L)
        dz_q[...] = dq_acc[...].astype(BF16)
        dz_k[...] = dk_acc[...].astype(BF16)
        dz_v[...] = dv_acc[...].astype(BF16)

    blk = (SUPER, HEAD_DIM)
    row = lambda i: nb - 1 - i
    in_specs = [
        pl.BlockSpec(blk, lambda h, i: (row(i), h)),
        pl.BlockSpec(blk, lambda h, i: (row(i), H + h)),
        pl.BlockSpec(blk, lambda h, i: (row(i), 2 * H + h)),
        pl.BlockSpec(blk, lambda h, i: (jnp.maximum(row(i) - 1, 0), H + h)),
        pl.BlockSpec(blk, lambda h, i: (jnp.maximum(row(i) - 1, 0), 2 * H + h)),
        pl.BlockSpec(blk, lambda h, i: (row(i), h)),
        pl.BlockSpec(blk, lambda h, i: (row(i), h)),
        pl.BlockSpec(blk, lambda h, i: (row(i), h)),
    ]
    out_spec = pl.BlockSpec(blk, lambda h, i: (row(i), h))
    return pl.pallas_call(
        body, name="attn_bwd", grid=(H, nb), in_specs=in_specs, out_specs=[out_spec] * 3,
        out_shape=[jax.ShapeDtypeStruct((T, H * HEAD_DIM), BF16)] * 3,
        scratch_shapes=[pltpu.VMEM(blk, F32) for _ in range(6)],
        compiler_params=_params(("parallel", "arbitrary")),
    )(z, z, z, z, z, dcat, o, lse)


def _glu(cv, cg):
    return cv * jax.nn.sigmoid(cg)


SUBLANES = 8


def _fill_shifted(sh, ext, rows):
    for b in range(1, SUBLANES):
        sh[b - 1, pl.ds(0, rows - SUBLANES), :] = ext[pl.ds(b, rows - SUBLANES), :]


def _window(sh, ext, offset, rows):
    b = offset % SUBLANES
    if b == 0:
        return ext[pl.ds(offset, rows), :]
    return sh[b - 1, pl.ds(offset - b, rows), :]


def _conv_fwd(z, cat, w_dw, b_dw, g_ln, b_ln, col0, tb=512):
    T = z.shape[0]
    DC = w_dw.shape[1]
    tb = _tile(T, tb, HALO)
    hb = tb // HALO
    cat_cb = cat.shape[1] // DC - 1

    def body(cv_ref, cg_ref, cvh_ref, cgh_ref, w_ref, bdw_ref, g_ref, b_ref, cat_in, cat_ref, c_ref, u_ext, sh):
        i = pl.program_id(0)
        halo = _glu(cvh_ref[...], cgh_ref[...])
        u_ext[pl.ds(0, HALO), :] = jnp.where(i > 0, halo, 0.0)
        u_ext[pl.ds(HALO, tb), :] = _glu(cv_ref[...], cg_ref[...])
        _fill_shifted(sh, u_ext, tb + HALO)
        acc = jnp.broadcast_to(bdw_ref[...], (tb, DC))
        for j in range(CONV_W):
            acc = acc + w_ref[pl.ds(j, 1), :] * _window(sh, u_ext, HALO - (CONV_W - 1) + j, tb)
        c_ref[...] = acc
        mu = jnp.mean(acc, axis=-1, keepdims=True)
        var = jnp.mean(jnp.square(acc - mu), axis=-1, keepdims=True)
        y = (acc - mu) * lax.rsqrt(var + EPS) * g_ref[...] + b_ref[...]
        cat_ref[...] = (y * jax.nn.sigmoid(y)).astype(BF16)

    cur = lambda cb: pl.BlockSpec((tb, DC), lambda i: (i, cb))
    halo = lambda cb: pl.BlockSpec((HALO, DC), lambda i: (jnp.maximum(i * hb - 1, 0), cb))
    whole = lambda a: pl.BlockSpec(a.shape, lambda i: (0, 0))
    return pl.pallas_call(
        body, name="conv_fwd", grid=(T // tb,),
        in_specs=[cur(col0), cur(col0 + 1), halo(col0), halo(col0 + 1), whole(w_dw), whole(b_dw), whole(g_ln),
                  whole(b_ln), ANY],
        out_specs=[pl.BlockSpec((tb, DC), lambda i: (i, cat_cb)), pl.BlockSpec((tb, DC), lambda i: (i, 0))],
        out_shape=[jax.ShapeDtypeStruct(cat.shape, cat.dtype), jax.ShapeDtypeStruct((T, DC), F32)],
        scratch_shapes=[pltpu.VMEM((tb + HALO, DC), F32), pltpu.VMEM((SUBLANES - 1, tb + HALO, DC), F32)],
        input_output_aliases={8: 0},
        compiler_params=_params(("parallel",)),
    )(z, z, z, z, w_dw, b_dw, g_ln, b_ln, cat)


def _conv_ln_bwd(dcat, c, g_ln, b_ln, tb=256, deps=()):
    T, DC = c.shape
    d_cb = dcat.shape[1] // DC - 1

    def fn(rows, consts):
        dov, cv_ = rows
        gv, bv = consts
        mu = jnp.mean(cv_, axis=-1, keepdims=True)
        xc = cv_ - mu
        rstd = lax.rsqrt(jnp.mean(jnp.square(xc), axis=-1, keepdims=True) + EPS)
        ln = xc * rstd
        y = ln * gv + bv
        sg = jax.nn.sigmoid(y)
        dy = dov * (sg * (1.0 + y * (1.0 - sg)))
        dln = dy * gv
        dc = rstd * (dln - jnp.mean(dln, axis=-1, keepdims=True) - ln * jnp.mean(dln * ln, axis=-1, keepdims=True))
        return (dc,), (_colsum(dy * ln), _colsum(dy), _colsum(dc))

    return _rowwise("conv_ln_bwd", fn, [(dcat, DC, d_cb), (c, DC, 0)], [g_ln, b_ln],
                    [((T, DC), F32, DC, 0)], [(1, DC)] * 3, tb, deps=deps)


def _conv_bwd(z, dc, w_dw, col0, tb=512):
    T, DC = dc.shape
    tb = _tile(T, tb, HALO)
    hb = tb // HALO
    nblk = T // tb

    def body(cv_ref, cg_ref, dc_ref, dcn_ref, w_ref, dcv_ref, dcg_ref, dw_ref, dc_ext, sh):
        i = pl.program_id(0)
        cv, cg = cv_ref[...], cg_ref[...]
        sg = jax.nn.sigmoid(cg)
        u = cv * sg
        dc_ext[pl.ds(0, tb), :] = dc_ref[...]
        dc_ext[pl.ds(tb, HALO), :] = jnp.where(i < nblk - 1, dcn_ref[...], 0.0)
        _fill_shifted(sh, dc_ext, tb + HALO)

        @pl.when(i == 0)
        def _():
            dw_ref[...] = jnp.zeros_like(dw_ref)

        du = jnp.zeros((tb, DC), F32)
        for j in range(CONV_W):
            d_j = _window(sh, dc_ext, CONV_W - 1 - j, tb)
            du = du + w_ref[pl.ds(j, 1), :] * d_j
            dw_ref[pl.ds(j, 1), :] += _colsum(u * d_j)
        dcv_ref[...] = (du * sg).astype(BF16)
        dcg_ref[...] = (du * cv * sg * (1.0 - sg)).astype(BF16)

    cur = lambda cb: pl.BlockSpec((tb, DC), lambda i: (i, cb))
    nxt = pl.BlockSpec((HALO, DC), lambda i: (jnp.minimum((i + 1) * hb, T // HALO - 1), 0))
    return pl.pallas_call(
        body, name="conv_bwd", grid=(nblk,),
        in_specs=[cur(col0), cur(col0 + 1), cur(0), nxt, pl.BlockSpec(w_dw.shape, lambda i: (0, 0))],
        out_specs=[cur(0), cur(0), pl.BlockSpec((HALO, DC), lambda i: (0, 0))],
        out_shape=[jax.ShapeDtypeStruct((T, DC), BF16), jax.ShapeDtypeStruct((T, DC), BF16),
                   jax.ShapeDtypeStruct((HALO, DC), F32)],
        scratch_shapes=[pltpu.VMEM((tb + HALO, DC), F32), pltpu.VMEM((SUBLANES - 1, tb + HALO, DC), F32)],
        compiler_params=_params(("arbitrary",)),
    )(z, z, dc, dc, w_dw)


def _ple_loss_head(n2, w_pgate, b_pgate, h2, p, w_ple, target, g_final, g_ple, tb=256):
    T, D = h2.shape

    def fn(rows, consts):
        n2v, h2v, pv, tv = rows
        wp, bp, wple, gv, gp = consts
        gt = jax.nn.sigmoid(lax.dot_general(n2v, wp, (((1,), (0,)), ((), ())), preferred_element_type=F32) + bp)
        ev = lax.dot_general(pv.astype(BF16), wple, (((1,), (0,)), ((), ())), preferred_element_type=F32)
        hv = h2v + ev * gt
        r = lax.rsqrt(jnp.mean(hv * hv, axis=-1, keepdims=True) + EPS)
        n = hv * r
        diff = n * gv - tv
        loss = 0.5 * jnp.sum(jnp.mean(jnp.square(diff), axis=-1, keepdims=True), axis=0, keepdims=True)
        dy = diff * (1.0 / D)
        dn = dy * gv
        dh = r * (dn - n * jnp.mean(dn * n, axis=-1, keepdims=True))
        du4f = dh * ev * gt * (1.0 - gt)
        du4 = du4f.astype(BF16)
        dn2 = lax.dot_general(du4, wp, (((1,), (1,)), ((), ())), preferred_element_type=F32)
        r2 = lax.rsqrt(jnp.mean(h2v * h2v, axis=-1, keepdims=True) + EPS)
        nn = h2v * r2
        dnn = dn2 * gp
        tot = dh + r2 * (dnn - nn * jnp.mean(dnn * nn, axis=-1, keepdims=True))
        return ((tot, tot, dh * gt, du4),
                (_colsum(dy * n), _colsum(du4f), jnp.broadcast_to(loss, (1, 128)), _colsum(dn2 * nn)))

    return _rowwise("ple_loss_head", fn, [(n2, D, 0), (h2, D, 0), (p, p.shape[1], 0), (target, D, 0)],
                    [w_pgate, b_pgate, w_ple, g_final, g_ple],
                    [((T, D), F32, D, 0), ((T, D), BF16, D, 0), ((T, D), BF16, D, 0), ((T, D), BF16, D, 0)],
                    [(1, D), (1, D), (1, 128), (1, D)], tb, vmem=VMEM_LIMIT_LARGE)


def _me():
    return lax.axis_index("x"), lax.axis_index("y"), lax.axis_index("c")


def _chips3(x, y):
    return [(1 - x, y), (x, 1 - y), (1 - x, 1 - y)]


def _peers7(x, y, c):
    for m in range(1, 8):
        yield m - 1, (x ^ (m >> 2), y ^ ((m >> 1) & 1), c ^ (m & 1))


def _shard_of(ref, axis, s):
    R, C = ref.shape
    if axis == 1:
        return ref.at[:, pl.ds(s * (C // 4), C // 4)]
    return ref.at[pl.ds(s * (R // 4), R // 4), :]


def _region(ref, axis, shard, half):
    R, C = ref.shape
    if axis == 1:
        cs, hr = C // 4, R // 2
        return ref.at[pl.ds(half * hr, hr), pl.ds(shard * cs, cs)]
    hr = R // 8
    return ref.at[pl.ds(shard * 2 * hr + half * hr, hr), :]


def _gather_now(shard, small):
    R, C = shard.shape
    cs = small.shape[1]

    def body(shard_ref, small_ref, out_ref, small_out, send, recv, fsend, frecv, lsem):
        x, y, c = _me()
        me_s = 2 * x + y
        sibling = (x, y, 1 - c)
        chips = _chips3(x, y)
        half = shard_ref.at[pl.ds(c * (R // 2), R // 2), :]
        locals_ = [pltpu.make_async_copy(shard_ref, _shard_of(out_ref, 1, me_s), lsem.at[0]),
                   pltpu.make_async_copy(small_ref, _shard_of(small_out, 1, me_s), lsem.at[1])]
        for cp in locals_:
            cp.start()
        firsts = []
        for j, (px, py) in enumerate(chips):
            firsts.append(pltpu.make_async_remote_copy(
                src_ref=half, dst_ref=_region(out_ref, 1, me_s, c), send_sem=send.at[0, j], recv_sem=recv.at[0, j],
                device_id=(px, py, c), device_id_type=MESH))
            firsts.append(pltpu.make_async_remote_copy(
                src_ref=small_ref, dst_ref=_shard_of(small_out, 1, me_s), send_sem=send.at[1, j],
                recv_sem=recv.at[1, j], device_id=(px, py, c), device_id_type=MESH))
        for cp in firsts:
            cp.start()
        relays = []
        for j, (px, py) in enumerate(chips):
            landed = _region(out_ref, 1, 2 * px + py, c)
            pltpu.make_async_remote_copy(
                src_ref=half, dst_ref=landed, send_sem=send.at[0, j], recv_sem=recv.at[0, j],
                device_id=(px, py, c), device_id_type=MESH).wait_recv()
            relay = pltpu.make_async_remote_copy(
                src_ref=landed, dst_ref=landed, send_sem=fsend.at[j], recv_sem=frecv.at[j],
                device_id=sibling, device_id_type=MESH)
            relay.start()
            relays.append(relay)
        for j, (px, py) in enumerate(chips):
            pltpu.make_async_remote_copy(
                src_ref=small_ref, dst_ref=_shard_of(small_out, 1, 2 * px + py), send_sem=send.at[1, j],
                recv_sem=recv.at[1, j], device_id=(px, py, c), device_id_type=MESH).wait_recv()
            theirs = _region(out_ref, 1, 2 * px + py, 1 - c)
            pltpu.make_async_remote_copy(
                src_ref=theirs, dst_ref=theirs, send_sem=fsend.at[j], recv_sem=frecv.at[j],
                device_id=sibling, device_id_type=MESH).wait_recv()
        for cp in firsts + relays:
            cp.wait_send()
        for cp in locals_:
            cp.wait()

    return pl.pallas_call(
        body, name="gather_now", in_specs=[ANY, ANY], out_specs=[ANY, ANY],
        out_shape=[jax.ShapeDtypeStruct((R, 4 * C), shard.dtype),
                   jax.ShapeDtypeStruct((small.shape[0], 4 * cs), small.dtype)],
        scratch_shapes=[pltpu.SemaphoreType.DMA((2, 3)), pltpu.SemaphoreType.DMA((2, 3)),
                        pltpu.SemaphoreType.DMA((3,)), pltpu.SemaphoreType.DMA((3,)), pltpu.SemaphoreType.DMA((2,))],
        compiler_params=_params(),
    )(shard, small)


def _exchange_small(small):
    def body(small_ref, out_ref, send, recv, lsem):
        x, y, c = _me()
        me = 4 * x + 2 * y + c
        own = pltpu.make_async_copy(small_ref, out_ref.at[me], lsem)
        own.start()
        sends = [pltpu.make_async_remote_copy(
            src_ref=small_ref, dst_ref=out_ref.at[me], send_sem=send.at[m], recv_sem=recv.at[m],
            device_id=peer, device_id_type=MESH) for m, peer in _peers7(x, y, c)]
        for cp in sends:
            cp.start()
        for m, (px, py, pc) in _peers7(x, y, c):
            pltpu.make_async_remote_copy(
                src_ref=small_ref, dst_ref=out_ref.at[4 * px + 2 * py + pc], send_sem=send.at[m], recv_sem=recv.at[m],
                device_id=(px, py, pc), device_id_type=MESH).wait_recv()
        for cp in sends:
            cp.wait_send()
        own.wait()

    return pl.pallas_call(
        body, name="exchange_small", in_specs=[ANY], out_specs=ANY,
        out_shape=jax.ShapeDtypeStruct((8,) + small.shape, small.dtype),
        scratch_shapes=[pltpu.SemaphoreType.DMA((7,)), pltpu.SemaphoreType.DMA((7,)), pltpu.SemaphoreType.DMA(())],
        compiler_params=_params(),
    )(small)


def _sum_slots(name, slots, tr=256):
    S, R, C = slots.shape
    tr = _tile(R, tr, 16)

    def body(s_ref, o_ref):
        acc = s_ref[0].astype(F32)
        for s in range(1, S):
            acc = acc + s_ref[s].astype(F32)
        o_ref[...] = acc

    return pl.pallas_call(
        body, name=name, grid=(R // tr,), in_specs=[pl.BlockSpec((S, tr, C), lambda i: (0, i, 0))],
        out_specs=pl.BlockSpec((tr, C), lambda i: (i, 0)), out_shape=jax.ShapeDtypeStruct((R, C), F32),
        compiler_params=_params(("parallel",)),
    )(slots)


def _place_shard(name, shard, axis):
    R, C = shard.shape
    full_shape = (R, 4 * C) if axis == 1 else (4 * R, C)

    def body(shard_ref, full_ref, sem):
        x, y, c = _me()
        cp = pltpu.make_async_copy(shard_ref, _shard_of(full_ref, axis, 2 * x + y), sem)
        cp.start()
        cp.wait()

    return pl.pallas_call(
        body, name=name, in_specs=[ANY], out_specs=ANY, out_shape=jax.ShapeDtypeStruct(full_shape, shard.dtype),
        scratch_shapes=[pltpu.SemaphoreType.DMA(())], compiler_params=_params(),
    )(shard)


def _split_start(name, body, arrays, sem_shape, after=None):
    n = len(arrays)
    extra = [] if after is None else [after]

    def kernel_body(*refs):
        body(refs[:n], refs[n + len(extra)], refs[n + len(extra) + 1])
        token = refs[-1]
        token[...] = jnp.zeros_like(token)

    sems = pltpu.SemaphoreType.DMA(sem_shape)
    return pl.pallas_call(
        kernel_body, name=name,
        out_shape=(sems, sems, *[pltpu.HBM(a.shape, a.dtype) for a in arrays], jax.ShapeDtypeStruct(TOKEN, F32)),
        in_specs=(HBM,) * n + (ANY,) * len(extra),
        out_specs=(SEM, SEM) + (HBM,) * n + (pl.BlockSpec(memory_space=pltpu.VMEM),),
        input_output_aliases={k: 2 + k for k in range(n)}, compiler_params=pltpu.CompilerParams(has_side_effects=EFFECT),
    )(*[pltpu.with_memory_space_constraint(a, pltpu.HBM) for a in arrays], *extra)


def _split_wait(name, body, started, after):
    send, recv, *arrays = started[:-1]
    n = len(arrays)

    def kernel_body(*refs):
        body(refs[:n], refs[n], refs[n + 1])

    return pl.pallas_call(
        kernel_body, name=name, out_shape=tuple(pltpu.HBM(a.shape, a.dtype) for a in arrays),
        in_specs=(HBM,) * n + (SEM, SEM, ANY), out_specs=(HBM,) * n, input_output_aliases={k: k for k in range(n)},
        compiler_params=pltpu.CompilerParams(has_side_effects=EFFECT),
    )(*arrays, send, recv, after)


def _gather_copies(refs, send, recv, axes, landing):
    n = len(axes)
    x, y, c = _me()
    copies = []
    for k in range(n):
        for j, (px, py) in enumerate(_chips3(x, y)):
            s = 2 * px + py if landing else 2 * x + y
            copies.append(pltpu.make_async_remote_copy(
                src_ref=refs[k], dst_ref=_shard_of(refs[n + k], axes[k], s), send_sem=send.at[3 * k + j],
                recv_sem=recv.at[3 * k + j], device_id=(px, py, c), device_id_type=MESH))
    return copies


def _gather_start(name, shards, fulls, axes, after):
    def body(refs, send, recv):
        for cp in _gather_copies(refs, send, recv, axes, False):
            cp.start()

    return _split_start(name, body, list(shards) + list(fulls), (3 * len(axes),), after)


def _gather_wait(name, started, axes, after):
    def body(refs, send, recv):
        for cp in _gather_copies(refs, send, recv, axes, True):
            cp.wait_send()
            cp.wait_recv()

    return _split_wait(name, body, started, after)[len(axes):]


def _piece_shape(shape, axis):
    R, C = shape
    return (R // 2, C // 4) if axis == 1 else (R // 8, C)


def _scatter_copies(refs, send, recv, axes):
    n = len(axes)
    x, y, c = _me()
    return [pltpu.make_async_remote_copy(
        src_ref=_region(refs[k], axes[k], 2 * px + py, pc), dst_ref=refs[n + k].at[m], send_sem=send.at[7 * k + m],
        recv_sem=recv.at[7 * k + m], device_id=(px, py, pc), device_id_type=MESH)
        for k in range(n) for m, (px, py, pc) in _peers7(x, y, c)]


def _scatter_start(name, gs, axes):
    def body(refs, send, recv):
        for cp in _scatter_copies(refs, send, recv, axes):
            cp.start()

    lands = [lax.empty((7,) + _piece_shape(g.shape, ax), g.dtype) for g, ax in zip(gs, axes)]
    return _split_start(name, body, list(gs) + lands, (7 * len(axes),))


def _scatter_wait(name, started, axes, after):
    def body(refs, send, recv):
        for cp in _scatter_copies(refs, send, recv, axes):
            cp.wait_send()
            cp.wait_recv()

    out = _split_wait(name, body, started, after)
    return out[:len(axes)], out[len(axes):]


def _sum_pieces(name, g, axis, slots, tr=256):
    S, R, C = slots.shape
    tr = _tile(R, tr, 16)
    nblk = R // tr
    place = jnp.stack([lax.axis_index("c"), 2 * lax.axis_index("x") + lax.axis_index("y")]).astype(jnp.int32)

    def body(at_ref, own_ref, s_ref, o_ref):
        acc = own_ref[...].astype(F32)
        for s in range(S):
            acc = acc + s_ref[s].astype(F32)
        o_ref[...] = acc

    if axis == 1:
        own_map = lambda i, at: (at[0] * nblk + i, at[1])
    else:
        own_map = lambda i, at: ((2 * at[1] + at[0]) * nblk + i, 0)
    grid_spec = pltpu.PrefetchScalarGridSpec(
        num_scalar_prefetch=1, grid=(nblk,),
        in_specs=[pl.BlockSpec((tr, C), own_map), pl.BlockSpec((S, tr, C), lambda i, at: (0, i, 0))],
        out_specs=pl.BlockSpec((tr, C), lambda i, at: (at[0] * nblk + i, 0)))
    return pl.pallas_call(
        body, name=name, grid_spec=grid_spec, out_shape=jax.ShapeDtypeStruct((2 * R, C), F32),
        compiler_params=_params(("parallel",)),
    )(place, g, slots)


def _half_copies(refs, send, recv, mine):
    x, y, c = _me()
    copies = []
    for k, ref in enumerate(refs):
        hr = ref.shape[0] // 2
        rows = ref.at[pl.ds((c if mine else 1 - c) * hr, hr), :]
        copies.append(pltpu.make_async_remote_copy(
            src_ref=rows, dst_ref=rows, send_sem=send.at[k], recv_sem=recv.at[k], device_id=(x, y, 1 - c),
            device_id_type=MESH))
    return copies


def _join_start(name, bufs):
    def body(refs, send, recv):
        for cp in _half_copies(refs, send, recv, True):
            cp.start()

    return _split_start(name, body, list(bufs), (len(bufs),))


def _join_wait(name, started, after):
    def body(refs, send, recv):
        for cp in _half_copies(refs, send, recv, False):
            cp.wait_send()
            cp.wait_recv()

    return _split_wait(name, body, started, after)


def _adamw(name, w, g, m, v, tr=256, deps=()):
    R, C = w.shape
    tr = _tile(R, tr, 8)
    c1 = 1.0 - ADAM_B1 ** ADAM_STEP
    c2 = 1.0 - ADAM_B2 ** ADAM_STEP

    def body(w_ref, g_ref, m_ref, v_ref, *rest):
        d_ref, nm_ref, nv_ref, go_ref = rest[len(deps):]
        gv = g_ref[...]
        go_ref[...] = gv
        nm = ADAM_B1 * m_ref[...] + (1.0 - ADAM_B1) * gv
        nv = ADAM_B2 * v_ref[...] + (1.0 - ADAM_B2) * jnp.square(gv)
        d_ref[...] = -ADAM_LR * ((nm / c1) / (jnp.sqrt(nv / c2) + ADAM_EPS) + ADAM_WD * w_ref[...])
        nm_ref[...] = nm
        nv_ref[...] = nv

    spec = pl.BlockSpec((tr, C), lambda i: (i, 0))
    return pl.pallas_call(
        body, name=name, grid=(R // tr,), in_specs=[spec] * 4 + [pl.BlockSpec(TOKEN, lambda i: (0, 0))] * len(deps),
        out_specs=[spec] * 4, out_shape=[jax.ShapeDtypeStruct((R, C), F32)] * 4,
        compiler_params=_params(("parallel",)),
    )(w, g, m, v, *deps)


def kernel(x, p, g_mix, w_in, w_dw, b_dw, g_conv_ln, b_conv_ln, w_out, g_ffn, w_gate, w_up, w_down, g_ple, w_pgate, b_pgate, w_ple, g_final, loss_target, m_g_mix, m_w_in, m_w_dw, m_b_dw, m_g_conv_ln, m_b_conv_ln, m_w_out, m_g_ffn, m_w_gate, m_w_up, m_w_down, m_g_ple, m_w_pgate, m_b_pgate, m_w_ple, m_g_final, v_g_mix, v_w_in, v_w_dw, v_b_dw, v_g_conv_ln, v_b_conv_ln, v_w_out, v_g_ffn, v_w_gate, v_w_up, v_w_down, v_g_ple, v_w_pgate, v_b_pgate, v_w_ple, v_g_final):
    T, D = x.shape[1], x.shape[2]
    DC = b_dw.shape[1]
    DA = D - DC
    H = DA // HEAD_DIM
    DP = p.shape[3]
    assert T % SUPER == 0 and DA == DC
    x2 = x.reshape(T, D)
    p2 = p.reshape(T, DP)
    tgt = loss_target.reshape(T, D)
    g_final2 = g_final.reshape(1, D)

    big = dict(w_in=(w_in[0], 1), w_out=(w_out[0], 0), w_gate=(w_gate[0], 1), w_up=(w_up[0], 1),
               w_down=(w_down[0], 0), w_pgate=(w_pgate[0], 0), w_ple=(w_ple[0], 1))
    axis_of = {k: big[k][1] for k in big}
    dw_shard = jnp.pad(w_dw.reshape(CONV_W, -1), ((0, HALO - CONV_W), (0, 0)))
    w_in_full, w_dw_full = _gather_now(big["w_in"][0].astype(BF16), dw_shard)
    later = ["w_out", "w_gate", "w_up", "w_down", "w_pgate", "w_ple"]
    shard16 = {k: big[k][0].astype(BF16) for k in later}
    placed = {k: _place_shard("place_" + k, shard16[k], axis_of[k]) for k in later}
    travelling = {}

    def fetch(groups, after):
        for keys in groups:
            travelling[keys] = _gather_start("gather_start_" + keys[0], [shard16[k] for k in keys],
                                             [placed[k] for k in keys], [axis_of[k] for k in keys], after)
        return [travelling[keys][-1] for keys in groups]

    def weights(keys, after):
        return _gather_wait("gather_wait_" + keys[0], travelling[keys], [axis_of[k] for k in keys], after)

    issued = fetch([("w_out",), ("w_gate", "w_up")], w_in_full)

    a = _rms_fwd("rms_mix", x2, g_mix, deps=issued)
    z = _matmul("mm_in", [[(a, w_in_full)]], "nn", [F32], _plain)[0]
    cat, o_attn, lse = _attn_fwd(z, D, H)
    cat, conv_c = _conv_fwd(z, cat, w_dw_full, b_dw, g_conv_ln, b_conv_ln, 3 * DA // DC)
    W = dict(w_in=w_in_full)
    W["w_out"], = weights(("w_out",), cat)
    def residual_and_norm(accs, extras):
        h = accs[0] + extras[0]
        r = lax.rsqrt(jnp.mean(h * h, axis=-1, keepdims=True) + EPS)
        return h, (h * r) * extras[1]

    h1, f = _matmul("mm_out", [[(cat, W["w_out"])]], "nn", [F32, BF16], residual_and_norm,
                    extras=[(x2, "mn"), (g_ffn, "n")], tm=512, tn=D, sub=256)
    W["w_gate"], W["w_up"] = weights(("w_gate", "w_up"), f)

    def swiglu(accs, extras):
        gt, up = accs
        return gt, up, (gt * jax.nn.sigmoid(gt)) * up

    gate, up, act = _matmul("mm_gate_up", [[(f, W["w_gate"])], [(f, W["w_up"])]], "nn", [BF16, BF16, BF16],
                            swiglu, tm=1024, tn=512, sub=256,
                            deps=fetch([("w_down",), ("w_pgate", "w_ple")], W["w_up"]))
    W["w_down"], = weights(("w_down",), act)
    h2 = _matmul("mm_down", [[(act, W["w_down"])]], "nn", [F32], _add_residual, extras=[(h1, "mn")], tm=512, tk=5632)[0]
    n2 = _rms_fwd("rms_ple", h2, g_ple)
    W["w_pgate"], W["w_ple"] = weights(("w_pgate", "w_ple"), n2)
    sent, joined = {}, {}

    def send_grads(keys, gs):
        sent[keys] = _scatter_start("scatter_start_" + keys[0], gs, [axis_of[k] for k in keys])
        return sent[keys][-1]

    def reduce_grads(groups, after):
        keys_all, halves = (), []
        for keys in groups:
            axes = [axis_of[k] for k in keys]
            g_thru, lands = _scatter_wait("scatter_wait_" + keys[0], sent[keys], axes, after)
            halves += [_sum_pieces("sum_" + k, g, ax, land)
                       for k, g, ax, land in zip(keys, g_thru, axes, lands)]
            keys_all += keys
        joined[keys_all] = _join_start("join_start_" + keys_all[0], halves)
        return joined[keys_all][-1]

    dh2, dh2_16, de, du4, dg_final, db_pgate, loss_part, dg_ple = _ple_loss_head(
        n2, W["w_pgate"], b_pgate, h2, p2, W["w_ple"], tgt, g_final2, g_ple)
    t_ple = send_grads(("w_ple", "w_pgate"),
                       [_matmul("mm_dw_ple", [[(p2, de)]], "tn", [BF16], _plain, tk=1024)[0],
                        _matmul("mm_dw_pgate", [[(n2, du4)]], "tn", [BF16], _plain, tk=2048)[0]])
    t_down = send_grads(("w_down",), [_matmul("mm_dw_down", [[(act, dh2_16)]], "tn", [BF16], _plain, tm=1408, tn=2048,
                                              tk=1024, deps=[t_ple])[0]])

    def swiglu_bwd(accs, extras):
        gt, up = extras[0].astype(F32), extras[1].astype(F32)
        sg = jax.nn.sigmoid(gt)
        dact = accs[0]
        return dact * up * (sg * (1.0 + gt * (1.0 - sg))), dact * (gt * sg)

    dgate, dup = _matmul("mm_dact", [[(dh2_16, W["w_down"])]], "nt", [BF16, BF16], swiglu_bwd,
                         extras=[(gate, "mn"), (up, "mn")], tm=2048, tn=512, sub=256, deps=[t_down])
    t_up = send_grads(("w_gate", "w_up"),
                      [_matmul("mm_dw_gate", [[(f, dgate)]], "tn", [BF16], _plain, tn=1408, tk=2048)[0],
                       _matmul("mm_dw_up", [[(f, dup)]], "tn", [BF16], _plain, tn=1408, tk=2048)[0]])
    df = _matmul("mm_df", [[(dgate, W["w_gate"]), (dup, W["w_up"])]], "nt", [BF16], _plain, tm=1024, tn=256,
                 tk=5632, deps=[t_up], vmem=VMEM_LIMIT_LARGE)[0]
    dh1, dh1_16, dg_ffn = _rms_bwd("rms_ffn_bwd", df, h1, g_ffn, dh2)
    t_out = send_grads(("w_out",), [_matmul("mm_dw_out", [[(cat, dh1_16)]], "tn", [BF16], _plain, tk=2048)[0]])
    dcat = _matmul("mm_dcat", [[(dh1_16, W["w_out"])]], "nt", [F32], _plain, deps=[t_out])[0]
    j1 = reduce_grads([("w_ple", "w_pgate"), ("w_down",)], dcat)
    dc, dg_ln, db_ln, db_dw = _conv_ln_bwd(dcat, conv_c, g_conv_ln, b_conv_ln, deps=[j1])
    dcv, dcg, dw_dw = _conv_bwd(z, dc, w_dw_full, 3 * DA // DC)
    dq, dk, dv = _attn_bwd(z, dcat, o_attn, lse, H)
    dz = [dq, dk, dv, dcv, dcg]
    dw_in = jnp.concatenate([_matmul("mm_dw_in%d" % n, [[(a, part)]], "tn", [BF16], _plain, tk=2048)[0]
                             for n, part in enumerate(dz)], axis=1)
    t_in = send_grads(("w_in",), [dw_in])
    j2 = reduce_grads([("w_gate", "w_up"), ("w_out",)], dw_in)
    da = _matmul("mm_da", [[(part, W["w_in"], n) for n, part in enumerate(dz)]], "nt", [BF16], _plain, tm=512,
                 tk=DA, deps=[t_in, j2])[0]
    grad_x, _, dg_mix = _rms_bwd("rms_mix_bwd", da, x2, g_mix, dh1)

    wide = [dg_mix, dg_ffn, dg_ple, db_pgate, dg_final,
            jnp.concatenate([db_dw, dg_ln], axis=1), jnp.concatenate([db_ln, jnp.zeros_like(db_ln)], axis=1),
            jnp.pad(loss_part, ((0, 0), (0, D - 128))),
            dw_dw.reshape(HALO * DC // D, D)]
    small = jnp.concatenate(wide, axis=0)
    small = jnp.pad(small, ((0, -small.shape[0] % 8), (0, 0)))
    small_sum = _sum_slots("sum_small", _exchange_small(small))
    j3 = reduce_grads([("w_in",)], small_sum)

    grads, deltas, new_m, new_v = {}, {}, {}, {}
    moments = dict(w_in=(m_w_in, v_w_in), w_out=(m_w_out, v_w_out), w_gate=(m_w_gate, v_w_gate),
                   w_up=(m_w_up, v_w_up), w_down=(m_w_down, v_w_down), w_pgate=(m_w_pgate, v_w_pgate),
                   w_ple=(m_w_ple, v_w_ple))
    last, deps = small_sum, [j3]
    for keys in list(joined):
        for k, g_k in zip(keys, _join_wait("join_wait_" + keys[0], joined[keys], last)):
            d_, m_, v_, g_ = _adamw("adamw_" + k, big[k][0], g_k, moments[k][0][0], moments[k][1][0], deps=deps)
            grads[k], deltas[k], new_m[k], new_v[k] = g_[None], d_[None], m_[None], v_[None]
            last, deps = d_, ()

    half = lambda r, lo: small_sum[r:r + 1, lo * DC:(lo + 1) * DC]
    vec = dict(g_mix=small_sum[0:1], g_ffn=small_sum[1:2], g_ple=small_sum[2:3], b_pgate=small_sum[3:4],
               g_final=small_sum[4:5], b_dw=half(5, 0), g_conv_ln=half(5, 1), b_conv_ln=half(6, 0))
    loss = small_sum[7, 0]
    dw_dw_sum = small_sum[8:8 + HALO * DC // D].reshape(HALO, DC)
    s_me = 2 * lax.axis_index("x") + lax.axis_index("y")
    cs = w_dw.shape[3]
    vec["w_dw"] = lax.dynamic_slice(dw_dw_sum, (0, s_me * cs), (CONV_W, cs))
    small_w = dict(g_mix=(g_mix, m_g_mix, v_g_mix), g_ffn=(g_ffn, m_g_ffn, v_g_ffn), g_ple=(g_ple, m_g_ple, v_g_ple),
                   b_pgate=(b_pgate, m_b_pgate, v_b_pgate), g_final=(g_final, m_g_final, v_g_final),
                   b_dw=(b_dw, m_b_dw, v_b_dw), g_conv_ln=(g_conv_ln, m_g_conv_ln, v_g_conv_ln),
                   b_conv_ln=(b_conv_ln, m_b_conv_ln, v_b_conv_ln), w_dw=(w_dw, m_w_dw, v_w_dw))
    for k, (w_, m_, v_) in small_w.items():
        shape = w_.shape
        g2 = vec[k]
        to2 = lambda t: t.reshape(g2.shape)
        d_, nm_, nv_, g_ = _adamw("adamw_" + k, to2(w_), g2, to2(m_), to2(v_))
        grads[k], deltas[k], new_m[k], new_v[k] = (t.reshape(shape) for t in (g_, d_, nm_, nv_))

    order = ["g_mix", "w_in", "w_dw", "b_dw", "g_conv_ln", "b_conv_ln", "w_out", "g_ffn", "w_gate", "w_up", "w_down",
             "g_ple", "w_pgate", "b_pgate", "w_ple", "g_final"]
    return (loss, grad_x.reshape(x.shape), *[grads[k] for k in order], *[deltas[k] for k in order],
            *[new_m[k] for k in order], *[new_v[k] for k in order])
```

```python
import functools

import jax
import jax.numpy as jnp
from jax import lax
from jax.experimental import pallas as pl
from jax.experimental.pallas import tpu as pltpu

F32 = jnp.float32
BF16 = jnp.bfloat16

EPS = 1e-6
HEAD_DIM = 128
BLK = 128
DILATIONS = (1, 4, 16)
SUPER = BLK * DILATIONS[-1]
CONV_W = 31
HALO = 32
ADAM_LR, ADAM_B1, ADAM_B2, ADAM_EPS, ADAM_WD, ADAM_STEP = 0.001, 0.9, 0.999, 1e-08, 0.01, 10

V7X_VMEM_BYTES = 64 * 1024 * 1024
VMEM_LIMIT = V7X_VMEM_BYTES * 3 // 4
VMEM_LIMIT_LARGE = V7X_VMEM_BYTES * 15 // 16
MESH = pl.DeviceIdType.MESH
ANY = pl.BlockSpec(memory_space=pl.ANY)
HBM = pl.BlockSpec(memory_space=pltpu.HBM)
SEM = pl.BlockSpec(memory_space=pltpu.SEMAPHORE)
EFFECT = pltpu.SideEffectType.DATAFLOW_SIDE_EFFECTING
TOKEN = (8, 128)


def _params(semantics=None, vmem=VMEM_LIMIT, **kw):
    return pltpu.CompilerParams(dimension_semantics=semantics, vmem_limit_bytes=vmem, **kw)


def _tile(n, want, mult=128):
    if n <= want:
        return n
    for t in range(want - want % mult, 0, -mult):
        if n % t == 0:
            return t
    raise ValueError((n, want, mult))


_DIMS = {"nn": ((1,), (0,)), "nt": ((1,), (1,)), "tn": ((0,), (0,))}


def _matmul(name, groups, mode, out_dtypes, epilogue, extras=(), tm=1024, tn=1024, tk=2048, sub=None, deps=(),
            vmem=VMEM_LIMIT):
    a0, b0 = groups[0][0][:2]
    if mode == "nn":
        (M, K), N = a0.shape, b0.shape[1]
    elif mode == "nt":
        (M, K), N = a0.shape, b0.shape[0]
    else:
        (K, M), N = a0.shape, b0.shape[1]
    tm, tn, tk = _tile(M, tm), _tile(N, tn), _tile(K, tk)
    nk = K // tk
    if mode == "tn":
        a_spec = pl.BlockSpec((tk, tm), lambda i, j, k: (k, i))
    else:
        a_spec = pl.BlockSpec((tm, tk), lambda i, j, k: (i, k))
    operands, in_specs = [], []
    for grp in groups:
        for a, b, *k0 in grp:
            k0 = k0[0] if k0 else 0
            if mode == "nt":
                b_spec = pl.BlockSpec((tn, tk), functools.partial(lambda i, j, k, k0: (j, k + k0), k0=k0))
            else:
                b_spec = pl.BlockSpec((tk, tn), functools.partial(lambda i, j, k, k0: (k + k0, j), k0=k0))
            operands += [a, b]
            in_specs += [a_spec, b_spec]
    for arr, kind in extras:
        operands.append(arr)
        if kind == "mn":
            in_specs.append(pl.BlockSpec((tm, tn), lambda i, j, k: (i, j)))
        else:
            in_specs.append(pl.BlockSpec((1, tn), lambda i, j, k: (0, j)))
    for tok in deps:
        operands.append(tok)
        in_specs.append(pl.BlockSpec(TOKEN, lambda i, j, k: (0, 0)))
    n_pairs = [len(g) for g in groups]
    n_ex, n_out, n_grp, n_dep = len(extras), len(out_dtypes), len(groups), len(deps)
    out_shape = [jax.ShapeDtypeStruct((M, N), dt) for dt in out_dtypes]
    out_specs = [pl.BlockSpec((tm, tn), lambda i, j, k: (i, j)) for _ in out_dtypes]

    kinds = [kind for _, kind in extras]
    sub = tm if (sub is None or nk > 1 or mode == "tn") else sub
    assert tm % sub == 0

    def body(*refs):
        ex_refs = refs[2 * sum(n_pairs):2 * sum(n_pairs) + n_ex]
        pos = 2 * sum(n_pairs) + n_ex + n_dep
        out_refs = refs[pos:pos + n_out]
        acc_refs = refs[pos + n_out:]

        def products(rows):
            pos, parts = 0, []
            for g in range(n_grp):
                part = None
                for _ in range(n_pairs[g]):
                    a_ref, b_ref = refs[pos], refs[pos + 1]
                    pos += 2
                    d = lax.dot_general(a_ref[rows].astype(BF16), b_ref[...].astype(BF16),
                                        (_DIMS[mode], ((), ())), preferred_element_type=F32)
                    part = d if part is None else part + d
                parts.append(part)
            return parts

        def finish(accs, rows):
            outs = epilogue(accs, [e[rows] if kind == "mn" else e[...] for e, kind in zip(ex_refs, kinds)])
            for o_ref, o in zip(out_refs, outs):
                o_ref[rows] = o.astype(o_ref.dtype)

        if nk == 1:
            for r in range(tm // sub):
                rows = (pl.ds(r * sub, sub), slice(None)) if sub < tm else (slice(None), slice(None))
                finish(products(rows), rows)
        else:
            k = pl.program_id(2)
            parts = products((slice(None), slice(None)))

            @pl.when(k == 0)
            def _():
                for acc, part in zip(acc_refs, parts):
                    acc[...] = part

            @pl.when(k > 0)
            def _():
                for acc, part in zip(acc_refs, parts):
                    acc[...] += part

            @pl.when(k == nk - 1)
            def _():
                finish([acc[...] for acc in acc_refs], (slice(None), slice(None)))

    scratch = [pltpu.VMEM((tm, tn), F32) for _ in range(n_grp)] if nk > 1 else []
    return pl.pallas_call(
        body, name=name, grid=(M // tm, N // tn, nk),
        in_specs=in_specs, out_specs=out_specs, out_shape=out_shape, scratch_shapes=scratch,
        compiler_params=_params(("parallel", "parallel", "arbitrary"), vmem),
    )(*operands)


def _plain(accs, extras):
    return (accs[0],)


def _add_residual(accs, extras):
    return (accs[0] + extras[0],)


def _rowwise(name, fn, rows, consts, row_outs, acc_outs, tb, deps=(), vmem=VMEM_LIMIT):
    T = rows[0][0].shape[0]
    tb = _tile(T, tb, 16)
    operands = [r[0] for r in rows] + list(consts) + list(deps)
    in_specs = [pl.BlockSpec((tb, c), functools.partial(lambda i, cb: (i, cb), cb=cb)) for _, c, cb in rows]
    in_specs += [pl.BlockSpec(c.shape, functools.partial(lambda i, nd: (0,) * nd, nd=c.ndim)) for c in consts]
    in_specs += [pl.BlockSpec(TOKEN, lambda i: (0, 0)) for _ in deps]
    out_shape = [jax.ShapeDtypeStruct(s, dt) for s, dt, _, _ in row_outs]
    out_specs = [pl.BlockSpec((tb, c), functools.partial(lambda i, cb: (i, cb), cb=cb)) for _, _, c, cb in row_outs]
    out_shape += [jax.ShapeDtypeStruct(s, F32) for s in acc_outs]
    out_specs += [pl.BlockSpec(s, functools.partial(lambda i, nd: (0,) * nd, nd=len(s))) for s in acc_outs]
    n_rows, n_consts, n_ro, n_dep = len(rows), len(consts), len(row_outs), len(deps)

    def body(*refs):
        row_refs = refs[:n_rows]
        const_refs = refs[n_rows:n_rows + n_consts]
        out_refs = refs[n_rows + n_consts + n_dep:]
        ro, ao = fn([r[...] for r in row_refs], [c[...] for c in const_refs])
        for o_ref, o in zip(out_refs[:n_ro], ro):
            o_ref[...] = o.astype(o_ref.dtype)
        if acc_outs:
            i = pl.program_id(0)

            @pl.when(i == 0)
            def _():
                for a_ref, a in zip(out_refs[n_ro:], ao):
                    a_ref[...] = a

            @pl.when(i > 0)
            def _():
                for a_ref, a in zip(out_refs[n_ro:], ao):
                    a_ref[...] += a

    return pl.pallas_call(
        body, name=name, grid=(T // tb,), in_specs=in_specs, out_specs=out_specs, out_shape=out_shape,
        compiler_params=_params(("arbitrary",) if acc_outs else ("parallel",), vmem),
    )(*operands)


def _colsum(v):
    return jnp.sum(v, axis=0, keepdims=True)


def _rms_fwd(name, x, g, tb=512, deps=()):
    T, D = x.shape

    def fn(rows, consts):
        xv, gv = rows[0], consts[0]
        r = lax.rsqrt(jnp.mean(xv * xv, axis=-1, keepdims=True) + EPS)
        return ((xv * r) * gv,), ()

    return _rowwise(name, fn, [(x, D, 0)], [g], [((T, D), BF16, D, 0)], [], tb, deps=deps)[0]


def _rms_bwd(name, dy, x, g, resid, tb=512):
    T, D = x.shape

    def fn(rows, consts):
        dyv, xv, rv = rows
        gv = consts[0]
        r = lax.rsqrt(jnp.mean(xv * xv, axis=-1, keepdims=True) + EPS)
        n = xv * r
        dn = dyv * gv
        dx = r * (dn - n * jnp.mean(dn * n, axis=-1, keepdims=True))
        tot = rv + dx
        return (tot, tot), (_colsum(dyv * n),)

    return _rowwise(name, fn, [(dy, D, 0), (x, D, 0), (resid, D, 0)], [g],
                    [((T, D), F32, D, 0), ((T, D), BF16, D, 0)], [(1, D)], tb)


NBLK = SUPER // BLK


def _classes(ref, dil, rows, start=0, dtype=None):
    parts = [ref[pl.ds(start + r, rows, stride=dil), :] if dil > 1 else ref[pl.ds(start, rows), :]
             for r in range(dil)]
    if dtype is not None:
        parts = [p.astype(dtype) for p in parts]
    return parts


def _keys_with_prev(ref, prev_ref, dil):
    L = SUPER // dil
    own = _classes(ref, dil, L, dtype=BF16)
    last = _classes(prev_ref, dil, BLK, start=SUPER - BLK * dil, dtype=BF16)
    blocks = []
    for r in range(dil):
        ext = jnp.concatenate([last[r], own[r]], axis=0)
        blocks += [ext[j * BLK:(j + 2) * BLK] for j in range(L // BLK)]
    return jnp.stack(blocks, axis=0)


def _band_mask(dil, has_prev):
    qi = lax.broadcasted_iota(jnp.int32, (BLK, 2 * BLK), 0)
    kj = lax.broadcasted_iota(jnp.int32, (BLK, 2 * BLK), 1)
    own = jnp.logical_and(kj >= BLK, kj - BLK <= qi)
    prev = jnp.logical_and(kj < BLK, kj >= qi)
    b = lax.broadcasted_iota(jnp.int32, (NBLK, 1, 1), 0)
    first = (b & (SUPER // (BLK * dil) - 1)) == 0
    prev_ok = jnp.logical_or(jnp.logical_not(first), has_prev)
    return jnp.logical_or(own[None], jnp.logical_and(prev[None], prev_ok))


def _bdot(a, b, ca, cb):
    return lax.dot_general(a, b, (((ca,), (cb,)), ((0,), (0,))), preferred_element_type=F32)


def _put_classes(dst, value, dil, rows, start=0, add=False, src_start=0, src_stride=None):
    src_stride = rows if src_stride is None else src_stride
    for r in range(dil):
        idx = (pl.ds(start + r, rows, stride=dil) if dil > 1 else pl.ds(start, rows), slice(None))
        part = value[src_start + r * src_stride:src_start + r * src_stride + rows]
        dst[idx] = dst[idx] + part if add else part


def _attn_fwd(z, cat_width, n_heads):
    T = z.shape[0]
    H = n_heads
    DA = H * HEAD_DIM
    nb = T // SUPER
    scale = HEAD_DIM ** -0.5

    def body(q_ref, k_ref, v_ref, kp_ref, vp_ref, cat_ref, o_ref, lse_ref, ob, lb):
        has_prev = pl.program_id(1) > 0
        for b, dil in enumerate(DILATIONS):
            L = SUPER // dil
            q3 = jnp.concatenate(_classes(q_ref, dil, L, dtype=BF16), axis=0).reshape(NBLK, BLK, HEAD_DIM)
            k3 = _keys_with_prev(k_ref, kp_ref, dil)
            v3 = _keys_with_prev(v_ref, vp_ref, dil)
            s = jnp.where(_band_mask(dil, has_prev), _bdot(q3, k3, 2, 2) * scale, -jnp.inf)
            m = jnp.max(s, axis=-1, keepdims=True)
            e = jnp.exp(s - m)
            den = jnp.sum(e, axis=-1, keepdims=True)
            o3 = _bdot((e * (1.0 / den)).astype(BF16), v3, 2, 1)
            lse3 = jnp.broadcast_to(m + jnp.log(den), (NBLK, BLK, HEAD_DIM))
            _put_classes(ob.at[b], o3.reshape(SUPER, HEAD_DIM), dil, L)
            _put_classes(lb.at[b], lse3.reshape(SUPER, HEAD_DIM), dil, L)
        l0, l1, l2 = lb[0], lb[1], lb[2]
        mx = jnp.maximum(jnp.maximum(l0, l1), l2)
        tot = mx + jnp.log(jnp.exp(l0 - mx) + jnp.exp(l1 - mx) + jnp.exp(l2 - mx))
        o = jnp.exp(l0 - tot) * ob[0] + jnp.exp(l1 - tot) * ob[1] + jnp.exp(l2 - tot) * ob[2]
        o_ref[...] = o
        cat_ref[...] = o.astype(BF16)
        lse_ref[...] = tot

    blk = (SUPER, HEAD_DIM)
    in_specs = [
        pl.BlockSpec(blk, lambda h, n: (n, h)),
        pl.BlockSpec(blk, lambda h, n: (n, H + h)),
        pl.BlockSpec(blk, lambda h, n: (n, 2 * H + h)),
        pl.BlockSpec(blk, lambda h, n: (jnp.maximum(n - 1, 0), H + h)),
        pl.BlockSpec(blk, lambda h, n: (jnp.maximum(n - 1, 0), 2 * H + h)),
    ]
    out_spec = pl.BlockSpec(blk, lambda h, n: (n, h))
    return pl.pallas_call(
        body, name="attn_fwd", grid=(H, nb), in_specs=in_specs, out_specs=[out_spec] * 3,
        out_shape=[jax.ShapeDtypeStruct((T, cat_width), BF16), jax.ShapeDtypeStruct((T, DA), F32),
                   jax.ShapeDtypeStruct((T, DA), F32)],
        scratch_shapes=[pltpu.VMEM((3, SUPER, HEAD_DIM), F32), pltpu.VMEM((3, SUPER, HEAD_DIM), F32)],
        compiler_params=_params(("parallel", "parallel")),
    )(z, z, z, z, z)


def _attn_bwd(z, dcat, o, lse, n_heads):
    T = z.shape[0]
    H = n_heads
    nb = T // SUPER
    scale = HEAD_DIM ** -0.5

    def body(q_ref, k_ref, v_ref, kp_ref, vp_ref, do_ref, o_ref, lse_ref, dz_q, dz_k, dz_v,
             dq_acc, dk_acc, dv_acc, dkp_acc, dvp_acc, dsum):
        i = pl.program_id(1)
        has_prev = i < nb - 1

        @pl.when(i == 0)
        def _():
            dk_acc[...] = jnp.zeros_like(dk_acc)
            dv_acc[...] = jnp.zeros_like(dv_acc)

        @pl.when(i > 0)
        def _():
            dk_acc[...] = dkp_acc[...]
            dv_acc[...] = dvp_acc[...]

        dq_acc[...] = jnp.zeros_like(dq_acc)
        dkp_acc[...] = jnp.zeros_like(dkp_acc)
        dvp_acc[...] = jnp.zeros_like(dvp_acc)
        dsum[...] = jnp.broadcast_to(jnp.sum(do_ref[...] * o_ref[...], axis=-1, keepdims=True), (SUPER, HEAD_DIM))
        for dil in DILATIONS:
            L = SUPER // dil
            wide = lambda ref: jnp.tile(jnp.concatenate(_classes(ref, dil, L), axis=0).reshape(NBLK, BLK, HEAD_DIM),
                                        (1, 1, 2))
            q3 = jnp.concatenate(_classes(q_ref, dil, L, dtype=BF16), axis=0).reshape(NBLK, BLK, HEAD_DIM)
            do3 = jnp.concatenate(_classes(do_ref, dil, L, dtype=BF16), axis=0).reshape(NBLK, BLK, HEAD_DIM)
            k3 = _keys_with_prev(k_ref, kp_ref, dil)
            v3 = _keys_with_prev(v_ref, vp_ref, dil)
            p = jnp.where(_band_mask(dil, has_prev), jnp.exp(_bdot(q3, k3, 2, 2) * scale - wide(lse_ref)), 0.0)
            ds = (p * (_bdot(do3, v3, 2, 2) - wide(dsum)) * scale).astype(BF16)
            dq = _bdot(ds, k3, 2, 1).reshape(SUPER, HEAD_DIM)
            dk = _bdot(ds, q3, 1, 1)
            dv = _bdot(p.astype(BF16), do3, 1, 1)
            _put_classes(dq_acc, dq, dil, L, add=True)
            for acc, prev_acc, g in ((dk_acc, dkp_acc, dk), (dv_acc, dvp_acc, dv)):
                _put_classes(acc, g[:, BLK:, :].reshape(SUPER, HEAD_DIM), dil, L, add=True)
                to_prev = g[:, :BLK, :].reshape(SUPER, HEAD_DIM)
                if L > BLK:
                    _put_classes(acc, to_prev, dil, L - BLK, add=True, src_start=BLK, src_stride=L)
                _put_classes(prev_acc, to_prev, dil, BLK, start=SUPER - BLK * dil, add=True, src_stride=L)
        dz_q[...] = dq_acc[...].astype(BF16)
        dz_k[...] = dk_acc[...].astype(BF16)
        dz_v[...] = dv_acc[...].astype(BF16)

    blk = (SUPER, HEAD_DIM)
    row = lambda i: nb - 1 - i
    in_specs = [
        pl.BlockSpec(blk, lambda h, i: (row(i), h)),
        pl.BlockSpec(blk, lambda h, i: (row(i), H + h)),
        pl.BlockSpec(blk, lambda h, i: (row(i), 2 * H + h)),
        pl.BlockSpec(blk, lambda h, i: (jnp.maximum(row(i) - 1, 0), H + h)),
        pl.BlockSpec(blk, lambda h, i: (jnp.maximum(row(i) - 1, 0), 2 * H + h)),
        pl.BlockSpec(blk, lambda h, i: (row(i), h)),
        pl.BlockSpec(blk, lambda h, i: (row(i), h)),
        pl.BlockSpec(blk, lambda h, i: (row(i), h)),
    ]
    out_spec = pl.BlockSpec(blk, lambda h, i: (row(i), h))
    return pl.pallas_call(
        body, name="attn_bwd", grid=(H, nb), in_specs=in_specs, out_specs=[out_spec] * 3,
        out_shape=[jax.ShapeDtypeStruct((T, H * HEAD_DIM), BF16)] * 3,
        scratch_shapes=[pltpu.VMEM(blk, F32) for _ in range(6)],
        compiler_params=_params(("parallel", "arbitrary")),
    )(z, z, z, z, z, dcat, o, lse)


def _glu(cv, cg):
    return cv * jax.nn.sigmoid(cg)


SUBLANES = 8


def _fill_shifted(sh, ext, rows):
    for b in range(1, SUBLANES):
        sh[b - 1, pl.ds(0, rows - SUBLANES), :] = ext[pl.ds(b, rows - SUBLANES), :]


def _window(sh, ext, offset, rows):
    b = offset % SUBLANES
    if b == 0:
        return ext[pl.ds(offset, rows), :]
    return sh[b - 1, pl.ds(offset - b, rows), :]


def _conv_fwd(z, cat, w_dw, b_dw, g_ln, b_ln, col0, tb=512):
    T = z.shape[0]
    DC = w_dw.shape[1]
    tb = _tile(T, tb, HALO)
    hb = tb // HALO
    cat_cb = cat.shape[1] // DC - 1

    def body(cv_ref, cg_ref, cvh_ref, cgh_ref, w_ref, bdw_ref, g_ref, b_ref, cat_in, cat_ref, c_ref, u_ext, sh):
        i = pl.program_id(0)
        halo = _glu(cvh_ref[...], cgh_ref[...])
        u_ext[pl.ds(0, HALO), :] = jnp.where(i > 0, halo, 0.0)
        u_ext[pl.ds(HALO, tb), :] = _glu(cv_ref[...], cg_ref[...])
        _fill_shifted(sh, u_ext, tb + HALO)
        acc = jnp.broadcast_to(bdw_ref[...], (tb, DC))
        for j in range(CONV_W):
            acc = acc + w_ref[pl.ds(j, 1), :] * _window(sh, u_ext, HALO - (CONV_W - 1) + j, tb)
        c_ref[...] = acc
        mu = jnp.mean(acc, axis=-1, keepdims=True)
        var = jnp.mean(jnp.square(acc - mu), axis=-1, keepdims=True)
        y = (acc - mu) * lax.rsqrt(var + EPS) * g_ref[...] + b_ref[...]
        cat_ref[...] = (y * jax.nn.sigmoid(y)).astype(BF16)

    cur = lambda cb: pl.BlockSpec((tb, DC), lambda i: (i, cb))
    halo = lambda cb: pl.BlockSpec((HALO, DC), lambda i: (jnp.maximum(i * hb - 1, 0), cb))
    whole = lambda a: pl.BlockSpec(a.shape, lambda i: (0, 0))
    return pl.pallas_call(
        body, name="conv_fwd", grid=(T // tb,),
        in_specs=[cur(col0), cur(col0 + 1), halo(col0), halo(col0 + 1), whole(w_dw), whole(b_dw), whole(g_ln),
                  whole(b_ln), ANY],
        out_specs=[pl.BlockSpec((tb, DC), lambda i: (i, cat_cb)), pl.BlockSpec((tb, DC), lambda i: (i, 0))],
        out_shape=[jax.ShapeDtypeStruct(cat.shape, cat.dtype), jax.ShapeDtypeStruct((T, DC), F32)],
        scratch_shapes=[pltpu.VMEM((tb + HALO, DC), F32), pltpu.VMEM((SUBLANES - 1, tb + HALO, DC), F32)],
        input_output_aliases={8: 0},
        compiler_params=_params(("parallel",)),
    )(z, z, z, z, w_dw, b_dw, g_ln, b_ln, cat)


def _conv_ln_bwd(dcat, c, g_ln, b_ln, tb=256, deps=()):
    T, DC = c.shape
    d_cb = dcat.shape[1] // DC - 1

    def fn(rows, consts):
        dov, cv_ = rows
        gv, bv = consts
        mu = jnp.mean(cv_, axis=-1, keepdims=True)
        xc = cv_ - mu
        rstd = lax.rsqrt(jnp.mean(jnp.square(xc), axis=-1, keepdims=True) + EPS)
        ln = xc * rstd
        y = ln * gv + bv
        sg = jax.nn.sigmoid(y)
        dy = dov * (sg * (1.0 + y * (1.0 - sg)))
        dln = dy * gv
        dc = rstd * (dln - jnp.mean(dln, axis=-1, keepdims=True) - ln * jnp.mean(dln * ln, axis=-1, keepdims=True))
        return (dc,), (_colsum(dy * ln), _colsum(dy), _colsum(dc))

    return _rowwise("conv_ln_bwd", fn, [(dcat, DC, d_cb), (c, DC, 0)], [g_ln, b_ln],
                    [((T, DC), F32, DC, 0)], [(1, DC)] * 3, tb, deps=deps)


def _conv_bwd(z, dc, w_dw, col0, tb=512):
    T, DC = dc.shape
    tb = _tile(T, tb, HALO)
    hb = tb // HALO
    nblk = T // tb

    def body(cv_ref, cg_ref, dc_ref, dcn_ref, w_ref, dcv_ref, dcg_ref, dw_ref, dc_ext, sh):
        i = pl.program_id(0)
        cv, cg = cv_ref[...], cg_ref[...]
        sg = jax.nn.sigmoid(cg)
        u = cv * sg
        dc_ext[pl.ds(0, tb), :] = dc_ref[...]
        dc_ext[pl.ds(tb, HALO), :] = jnp.where(i < nblk - 1, dcn_ref[...], 0.0)
        _fill_shifted(sh, dc_ext, tb + HALO)

        @pl.when(i == 0)
        def _():
            dw_ref[...] = jnp.zeros_like(dw_ref)

        du = jnp.zeros((tb, DC), F32)
        for j in range(CONV_W):
            d_j = _window(sh, dc_ext, CONV_W - 1 - j, tb)
            du = du + w_ref[pl.ds(j, 1), :] * d_j
            dw_ref[pl.ds(j, 1), :] += _colsum(u * d_j)
        dcv_ref[...] = (du * sg).astype(BF16)
        dcg_ref[...] = (du * cv * sg * (1.0 - sg)).astype(BF16)

    cur = lambda cb: pl.BlockSpec((tb, DC), lambda i: (i, cb))
    nxt = pl.BlockSpec((HALO, DC), lambda i: (jnp.minimum((i + 1) * hb, T // HALO - 1), 0))
    return pl.pallas_call(
        body, name="conv_bwd", grid=(nblk,),
        in_specs=[cur(col0), cur(col0 + 1), cur(0), nxt, pl.BlockSpec(w_dw.shape, lambda i: (0, 0))],
        out_specs=[cur(0), cur(0), pl.BlockSpec((HALO, DC), lambda i: (0, 0))],
        out_shape=[jax.ShapeDtypeStruct((T, DC), BF16), jax.ShapeDtypeStruct((T, DC), BF16),
                   jax.ShapeDtypeStruct((HALO, DC), F32)],
        scratch_shapes=[pltpu.VMEM((tb + HALO, DC), F32), pltpu.VMEM((SUBLANES - 1, tb + HALO, DC), F32)],
        compiler_params=_params(("arbitrary",)),
    )(z, z, dc, dc, w_dw)


def _ple_loss_head(n2, w_pgate, b_pgate, h2, p, w_ple, target, g_final, g_ple, tb=256):
    T, D = h2.shape

    def fn(rows, consts):
        n2v, h2v, pv, tv = rows
        wp, bp, wple, gv, gp = consts
        gt = jax.nn.sigmoid(lax.dot_general(n2v, wp, (((1,), (0,)), ((), ())), preferred_element_type=F32) + bp)
        ev = lax.dot_general(pv.astype(BF16), wple, (((1,), (0,)), ((), ())), preferred_element_type=F32)
        hv = h2v + ev * gt
        r = lax.rsqrt(jnp.mean(hv * hv, axis=-1, keepdims=True) + EPS)
        n = hv * r
        diff = n * gv - tv
        loss = 0.5 * jnp.sum(jnp.mean(jnp.square(diff), axis=-1, keepdims=True), axis=0, keepdims=True)
        dy = diff * (1.0 / D)
        dn = dy * gv
        dh = r * (dn - n * jnp.mean(dn * n, axis=-1, keepdims=True))
        du4f = dh * ev * gt * (1.0 - gt)
        du4 = du4f.astype(BF16)
        dn2 = lax.dot_general(du4, wp, (((1,), (1,)), ((), ())), preferred_element_type=F32)
        r2 = lax.rsqrt(jnp.mean(h2v * h2v, axis=-1, keepdims=True) + EPS)
        nn = h2v * r2
        dnn = dn2 * gp
        tot = dh + r2 * (dnn - nn * jnp.mean(dnn * nn, axis=-1, keepdims=True))
        return ((tot, tot, dh * gt, du4),
                (_colsum(dy * n), _colsum(du4f), jnp.broadcast_to(loss, (1, 128)), _colsum(dn2 * nn)))

    return _rowwise("ple_loss_head", fn, [(n2, D, 0), (h2, D, 0), (p, p.shape[1], 0), (target, D, 0)],
                    [w_pgate, b_pgate, w_ple, g_final, g_ple],
                    [((T, D), F32, D, 0), ((T, D), BF16, D, 0), ((T, D), BF16, D, 0), ((T, D), BF16, D, 0)],
                    [(1, D), (1, D), (1, 128), (1, D)], tb, vmem=VMEM_LIMIT_LARGE)


def _me():
    return lax.axis_index("x"), lax.axis_index("y"), lax.axis_index("c")


def _chips3(x, y):
    return [(1 - x, y), (x, 1 - y), (1 - x, 1 - y)]


def _peers7(x, y, c):
    for m in range(1, 8):
        yield m - 1, (x ^ (m >> 2), y ^ ((m >> 1) & 1), c ^ (m & 1))


def _shard_of(ref, axis, s):
    R, C = ref.shape
    if axis == 1:
        return ref.at[:, pl.ds(s * (C // 4), C // 4)]
    return ref.at[pl.ds(s * (R // 4), R // 4), :]


def _region(ref, axis, shard, half):
    R, C = ref.shape
    if axis == 1:
        cs, hr = C // 4, R // 2
        return ref.at[pl.ds(half * hr, hr), pl.ds(shard * cs, cs)]
    hr = R // 8
    return ref.at[pl.ds(shard * 2 * hr + half * hr, hr), :]


def _gather_now(shard, small):
    R, C = shard.shape
    cs = small.shape[1]

    def body(shard_ref, small_ref, out_ref, small_out, send, recv, fsend, frecv, lsem):
        x, y, c = _me()
        me_s = 2 * x + y
        sibling = (x, y, 1 - c)
        chips = _chips3(x, y)
        half = shard_ref.at[pl.ds(c * (R // 2), R // 2), :]
        locals_ = [pltpu.make_async_copy(shard_ref, _shard_of(out_ref, 1, me_s), lsem.at[0]),
                   pltpu.make_async_copy(small_ref, _shard_of(small_out, 1, me_s), lsem.at[1])]
        for cp in locals_:
            cp.start()
        firsts = []
        for j, (px, py) in enumerate(chips):
            firsts.append(pltpu.make_async_remote_copy(
                src_ref=half, dst_ref=_region(out_ref, 1, me_s, c), send_sem=send.at[0, j], recv_sem=recv.at[0, j],
                device_id=(px, py, c), device_id_type=MESH))
            firsts.append(pltpu.make_async_remote_copy(
                src_ref=small_ref, dst_ref=_shard_of(small_out, 1, me_s), send_sem=send.at[1, j],
                recv_sem=recv.at[1, j], device_id=(px, py, c), device_id_type=MESH))
        for cp in firsts:
            cp.start()
        relays = []
        for j, (px, py) in enumerate(chips):
            landed = _region(out_ref, 1, 2 * px + py, c)
            pltpu.make_async_remote_copy(
                src_ref=half, dst_ref=landed, send_sem=send.at[0, j], recv_sem=recv.at[0, j],
                device_id=(px, py, c), device_id_type=MESH).wait_recv()
            relay = pltpu.make_async_remote_copy(
                src_ref=landed, dst_ref=landed, send_sem=fsend.at[j], recv_sem=frecv.at[j],
                device_id=sibling, device_id_type=MESH)
            relay.start()
            relays.append(relay)
        for j, (px, py) in enumerate(chips):
            pltpu.make_async_remote_copy(
                src_ref=small_ref, dst_ref=_shard_of(small_out, 1, 2 * px + py), send_sem=send.at[1, j],
                recv_sem=recv.at[1, j], device_id=(px, py, c), device_id_type=MESH).wait_recv()
            theirs = _region(out_ref, 1, 2 * px + py, 1 - c)
            pltpu.make_async_remote_copy(
                src_ref=theirs, dst_ref=theirs, send_sem=fsend.at[j], recv_sem=frecv.at[j],
                device_id=sibling, device_id_type=MESH).wait_recv()
        for cp in firsts + relays:
            cp.wait_send()
        for cp in locals_:
            cp.wait()

    return pl.pallas_call(
        body, name="gather_now", in_specs=[ANY, ANY], out_specs=[ANY, ANY],
        out_shape=[jax.ShapeDtypeStruct((R, 4 * C), shard.dtype),
                   jax.ShapeDtypeStruct((small.shape[0], 4 * cs), small.dtype)],
        scratch_shapes=[pltpu.SemaphoreType.DMA((2, 3)), pltpu.SemaphoreType.DMA((2, 3)),
                        pltpu.SemaphoreType.DMA((3,)), pltpu.SemaphoreType.DMA((3,)), pltpu.SemaphoreType.DMA((2,))],
        compiler_params=_params(),
    )(shard, small)


def _exchange_small(small):
    def body(small_ref, out_ref, send, recv, lsem):
        x, y, c = _me()
        me = 4 * x + 2 * y + c
        own = pltpu.make_async_copy(small_ref, out_ref.at[me], lsem)
        own.start()
        sends = [pltpu.make_async_remote_copy(
            src_ref=small_ref, dst_ref=out_ref.at[me], send_sem=send.at[m], recv_sem=recv.at[m],
            device_id=peer, device_id_type=MESH) for m, peer in _peers7(x, y, c)]
        for cp in sends:
            cp.start()
        for m, (px, py, pc) in _peers7(x, y, c):
            pltpu.make_async_remote_copy(
                src_ref=small_ref, dst_ref=out_ref.at[4 * px + 2 * py + pc], send_sem=send.at[m], recv_sem=recv.at[m],
                device_id=(px, py, pc), device_id_type=MESH).wait_recv()
        for cp in sends:
            cp.wait_send()
        own.wait()

    return pl.pallas_call(
        body, name="exchange_small", in_specs=[ANY], out_specs=ANY,
        out_shape=jax.ShapeDtypeStruct((8,) + small.shape, small.dtype),
        scratch_shapes=[pltpu.SemaphoreType.DMA((7,)), pltpu.SemaphoreType.DMA((7,)), pltpu.SemaphoreType.DMA(())],
        compiler_params=_params(),
    )(small)


def _sum_slots(name, slots, tr=256):
    S, R, C = slots.shape
    tr = _tile(R, tr, 16)

    def body(s_ref, o_ref):
        acc = s_ref[0].astype(F32)
        for s in range(1, S):
            acc = acc + s_ref[s].astype(F32)
        o_ref[...] = acc

    return pl.pallas_call(
        body, name=name, grid=(R // tr,), in_specs=[pl.BlockSpec((S, tr, C), lambda i: (0, i, 0))],
        out_specs=pl.BlockSpec((tr, C), lambda i: (i, 0)), out_shape=jax.ShapeDtypeStruct((R, C), F32),
        compiler_params=_params(("parallel",)),
    )(slots)


def _place_shard(name, shard, axis):
    R, C = shard.shape
    full_shape = (R, 4 * C) if axis == 1 else (4 * R, C)

    def body(shard_ref, full_ref, sem):
        x, y, c = _me()
        cp = pltpu.make_async_copy(shard_ref, _shard_of(full_ref, axis, 2 * x + y), sem)
        cp.start()
        cp.wait()

    return pl.pallas_call(
        body, name=name, in_specs=[ANY], out_specs=ANY, out_shape=jax.ShapeDtypeStruct(full_shape, shard.dtype),
        scratch_shapes=[pltpu.SemaphoreType.DMA(())], compiler_params=_params(),
    )(shard)


def _split_start(name, body, arrays, sem_shape, after=None):
    n = len(arrays)
    extra = [] if after is None else [after]

    def kernel_body(*refs):
        body(refs[:n], refs[n + len(extra)], refs[n + len(extra) + 1])
        token = refs[-1]
        token[...] = jnp.zeros_like(token)

    sems = pltpu.SemaphoreType.DMA(sem_shape)
    return pl.pallas_call(
        kernel_body, name=name,
        out_shape=(sems, sems, *[pltpu.HBM(a.shape, a.dtype) for a in arrays], jax.ShapeDtypeStruct(TOKEN, F32)),
        in_specs=(HBM,) * n + (ANY,) * len(extra),
        out_specs=(SEM, SEM) + (HBM,) * n + (pl.BlockSpec(memory_space=pltpu.VMEM),),
        input_output_aliases={k: 2 + k for k in range(n)}, compiler_params=pltpu.CompilerParams(has_side_effects=EFFECT),
    )(*[pltpu.with_memory_space_constraint(a, pltpu.HBM) for a in arrays], *extra)


def _split_wait(name, body, started, after):
    send, recv, *arrays = started[:-1]
    n = len(arrays)

    def kernel_body(*refs):
        body(refs[:n], refs[n], refs[n + 1])

    return pl.pallas_call(
        kernel_body, name=name, out_shape=tuple(pltpu.HBM(a.shape, a.dtype) for a in arrays),
        in_specs=(HBM,) * n + (SEM, SEM, ANY), out_specs=(HBM,) * n, input_output_aliases={k: k for k in range(n)},
        compiler_params=pltpu.CompilerParams(has_side_effects=EFFECT),
    )(*arrays, send, recv, after)


def _gather_copies(refs, send, recv, axes, landing):
    n = len(axes)
    x, y, c = _me()
    copies = []
    for k in range(n):
        for j, (px, py) in enumerate(_chips3(x, y)):
            s = 2 * px + py if landing else 2 * x + y
            copies.append(pltpu.make_async_remote_copy(
                src_ref=refs[k], dst_ref=_shard_of(refs[n + k], axes[k], s), send_sem=send.at[3 * k + j],
                recv_sem=recv.at[3 * k + j], device_id=(px, py, c), device_id_type=MESH))
    return copies


def _gather_start(name, shards, fulls, axes, after):
    def body(refs, send, recv):
        for cp in _gather_copies(refs, send, recv, axes, False):
            cp.start()

    return _split_start(name, body, list(shards) + list(fulls), (3 * len(axes),), after)


def _gather_wait(name, started, axes, after):
    def body(refs, send, recv):
        for cp in _gather_copies(refs, send, recv, axes, True):
            cp.wait_send()
            cp.wait_recv()

    return _split_wait(name, body, started, after)[len(axes):]


def _piece_shape(shape, axis):
    R, C = shape
    return (R // 2, C // 4) if axis == 1 else (R // 8, C)


def _scatter_copies(refs, send, recv, axes):
    n = len(axes)
    x, y, c = _me()
    return [pltpu.make_async_remote_copy(
        src_ref=_region(refs[k], axes[k], 2 * px + py, pc), dst_ref=refs[n + k].at[m], send_sem=send.at[7 * k + m],
        recv_sem=recv.at[7 * k + m], device_id=(px, py, pc), device_id_type=MESH)
        for k in range(n) for m, (px, py, pc) in _peers7(x, y, c)]


def _scatter_start(name, gs, axes):
    def body(refs, send, recv):
        for cp in _scatter_copies(refs, send, recv, axes):
            cp.start()

    lands = [lax.empty((7,) + _piece_shape(g.shape, ax), g.dtype) for g, ax in zip(gs, axes)]
    return _split_start(name, body, list(gs) + lands, (7 * len(axes),))


def _scatter_wait(name, started, axes, after):
    def body(refs, send, recv):
        for cp in _scatter_copies(refs, send, recv, axes):
            cp.wait_send()
            cp.wait_recv()

    out = _split_wait(name, body, started, after)
    return out[:len(axes)], out[len(axes):]


def _sum_pieces(name, g, axis, slots, tr=256):
    S, R, C = slots.shape
    tr = _tile(R, tr, 16)
    nblk = R // tr
    place = jnp.stack([lax.axis_index("c"), 2 * lax.axis_index("x") + lax.axis_index("y")]).astype(jnp.int32)

    def body(at_ref, own_ref, s_ref, o_ref):
        acc = own_ref[...].astype(F32)
        for s in range(S):
            acc = acc + s_ref[s].astype(F32)
        o_ref[...] = acc

    if axis == 1:
        own_map = lambda i, at: (at[0] * nblk + i, at[1])
    else:
        own_map = lambda i, at: ((2 * at[1] + at[0]) * nblk + i, 0)
    grid_spec = pltpu.PrefetchScalarGridSpec(
        num_scalar_prefetch=1, grid=(nblk,),
        in_specs=[pl.BlockSpec((tr, C), own_map), pl.BlockSpec((S, tr, C), lambda i, at: (0, i, 0))],
        out_specs=pl.BlockSpec((tr, C), lambda i, at: (at[0] * nblk + i, 0)))
    return pl.pallas_call(
        body, name=name, grid_spec=grid_spec, out_shape=jax.ShapeDtypeStruct((2 * R, C), F32),
        compiler_params=_params(("parallel",)),
    )(place, g, slots)


def _half_copies(refs, send, recv, mine):
    x, y, c = _me()
    copies = []
    for k, ref in enumerate(refs):
        hr = ref.shape[0] // 2
        rows = ref.at[pl.ds((c if mine else 1 - c) * hr, hr), :]
        copies.append(pltpu.make_async_remote_copy(
            src_ref=rows, dst_ref=rows, send_sem=send.at[k], recv_sem=recv.at[k], device_id=(x, y, 1 - c),
            device_id_type=MESH))
    return copies


def _join_start(name, bufs):
    def body(refs, send, recv):
        for cp in _half_copies(refs, send, recv, True):
            cp.start()

    return _split_start(name, body, list(bufs), (len(bufs),))


def _join_wait(name, started, after):
    def body(refs, send, recv):
        for cp in _half_copies(refs, send, recv, False):
            cp.wait_send()
            cp.wait_recv()

    return _split_wait(name, body, started, after)


def _adamw(name, w, g, m, v, tr=256, deps=()):
    R, C = w.shape
    tr = _tile(R, tr, 8)
    c1 = 1.0 - ADAM_B1 ** ADAM_STEP
    c2 = 1.0 - ADAM_B2 ** ADAM_STEP

    def body(w_ref, g_ref, m_ref, v_ref, *rest):
        d_ref, nm_ref, nv_ref, go_ref = rest[len(deps):]
        gv = g_ref[...]
        go_ref[...] = gv
        nm = ADAM_B1 * m_ref[...] + (1.0 - ADAM_B1) * gv
        nv = ADAM_B2 * v_ref[...] + (1.0 - ADAM_B2) * jnp.square(gv)
        d_ref[...] = -ADAM_LR * ((nm / c1) / (jnp.sqrt(nv / c2) + ADAM_EPS) + ADAM_WD * w_ref[...])
        nm_ref[...] = nm
        nv_ref[...] = nv

    spec = pl.BlockSpec((tr, C), lambda i: (i, 0))
    return pl.pallas_call(
        body, name=name, grid=(R // tr,), in_specs=[spec] * 4 + [pl.BlockSpec(TOKEN, lambda i: (0, 0))] * len(deps),
        out_specs=[spec] * 4, out_shape=[jax.ShapeDtypeStruct((R, C), F32)] * 4,
        compiler_params=_params(("parallel",)),
    )(w, g, m, v, *deps)


def kernel(x, p, g_mix, w_in, w_dw, b_dw, g_conv_ln, b_conv_ln, w_out, g_ffn, w_gate, w_up, w_down, g_ple, w_pgate, b_pgate, w_ple, g_final, loss_target, m_g_mix, m_w_in, m_w_dw, m_b_dw, m_g_conv_ln, m_b_conv_ln, m_w_out, m_g_ffn, m_w_gate, m_w_up, m_w_down, m_g_ple, m_w_pgate, m_b_pgate, m_w_ple, m_g_final, v_g_mix, v_w_in, v_w_dw, v_b_dw, v_g_conv_ln, v_b_conv_ln, v_w_out, v_g_ffn, v_w_gate, v_w_up, v_w_down, v_g_ple, v_w_pgate, v_b_pgate, v_w_ple, v_g_final):
    T, D = x.shape[1], x.shape[2]
    DC = b_dw.shape[1]
    DA = D - DC
    H = DA // HEAD_DIM
    DP = p.shape[3]
    assert T % SUPER == 0 and DA == DC
    x2 = x.reshape(T, D)
    p2 = p.reshape(T, DP)
    tgt = loss_target.reshape(T, D)
    g_final2 = g_final.reshape(1, D)

    big = dict(w_in=(w_in[0], 1), w_out=(w_out[0], 0), w_gate=(w_gate[0], 1), w_up=(w_up[0], 1),
               w_down=(w_down[0], 0), w_pgate=(w_pgate[0], 0), w_ple=(w_ple[0], 1))
    axis_of = {k: big[k][1] for k in big}
    dw_shard = jnp.pad(w_dw.reshape(CONV_W, -1), ((0, HALO - CONV_W), (0, 0)))
    w_in_full, w_dw_full = _gather_now(big["w_in"][0].astype(BF16), dw_shard)
    later = ["w_out", "w_gate", "w_up", "w_down", "w_pgate", "w_ple"]
    shard16 = {k: big[k][0].astype(BF16) for k in later}
    placed = {k: _place_shard("place_" + k, shard16[k], axis_of[k]) for k in later}
    travelling = {}

    def fetch(groups, after):
        for keys in groups:
            travelling[keys] = _gather_start("gather_start_" + keys[0], [shard16[k] for k in keys],
                                             [placed[k] for k in keys], [axis_of[k] for k in keys], after)
        return [travelling[keys][-1] for keys in groups]

    def weights(keys, after):
        return _gather_wait("gather_wait_" + keys[0], travelling[keys], [axis_of[k] for k in keys], after)

    issued = fetch([("w_out",), ("w_gate", "w_up")], w_in_full)

    a = _rms_fwd("rms_mix", x2, g_mix, deps=issued)
    z = _matmul("mm_in", [[(a, w_in_full)]], "nn", [F32], _plain)[0]
    cat, o_attn, lse = _attn_fwd(z, D, H)
    cat, conv_c = _conv_fwd(z, cat, w_dw_full, b_dw, g_conv_ln, b_conv_ln, 3 * DA // DC)
    W = dict(w_in=w_in_full)
    W["w_out"], = weights(("w_out",), cat)
    def residual_and_norm(accs, extras):
        h = accs[0] + extras[0]
        r = lax.rsqrt(jnp.mean(h * h, axis=-1, keepdims=True) + EPS)
        return h, (h * r) * extras[1]

    h1, f = _matmul("mm_out", [[(cat, W["w_out"])]], "nn", [F32, BF16], residual_and_norm,
                    extras=[(x2, "mn"), (g_ffn, "n")], tm=512, tn=D, sub=256)
    W["w_gate"], W["w_up"] = weights(("w_gate", "w_up"), f)

    def swiglu(accs, extras):
        gt, up = accs
        return gt, up, (gt * jax.nn.sigmoid(gt)) * up

    gate, up, act = _matmul("mm_gate_up", [[(f, W["w_gate"])], [(f, W["w_up"])]], "nn", [BF16, BF16, BF16],
                            swiglu, tm=1024, tn=512, sub=256,
                            deps=fetch([("w_down",), ("w_pgate", "w_ple")], W["w_up"]))
    W["w_down"], = weights(("w_down",), act)
    h2 = _matmul("mm_down", [[(act, W["w_down"])]], "nn", [F32], _add_residual, extras=[(h1, "mn")], tm=512, tk=5632)[0]
    n2 = _rms_fwd("rms_ple", h2, g_ple)
    W["w_pgate"], W["w_ple"] = weights(("w_pgate", "w_ple"), n2)
    sent, joined = {}, {}

    def send_grads(keys, gs):
        sent[keys] = _scatter_start("scatter_start_" + keys[0], gs, [axis_of[k] for k in keys])
        return sent[keys][-1]

    def reduce_grads(groups, after):
        keys_all, halves = (), []
        for keys in groups:
            axes = [axis_of[k] for k in keys]
            g_thru, lands = _scatter_wait("scatter_wait_" + keys[0], sent[keys], axes, after)
            halves += [_sum_pieces("sum_" + k, g, ax, land)
                       for k, g, ax, land in zip(keys, g_thru, axes, lands)]
            keys_all += keys
        joined[keys_all] = _join_start("join_start_" + keys_all[0], halves)
        return joined[keys_all][-1]

    dh2, dh2_16, de, du4, dg_final, db_pgate, loss_part, dg_ple = _ple_loss_head(
        n2, W["w_pgate"], b_pgate, h2, p2, W["w_ple"], tgt, g_final2, g_ple)
    t_ple = send_grads(("w_ple", "w_pgate"),
                       [_matmul("mm_dw_ple", [[(p2, de)]], "tn", [BF16], _plain, tk=1024)[0],
                        _matmul("mm_dw_pgate", [[(n2, du4)]], "tn", [BF16], _plain, tk=2048)[0]])
    t_down = send_grads(("w_down",), [_matmul("mm_dw_down", [[(act, dh2_16)]], "tn", [BF16], _plain, tm=1408, tn=2048,
                                              tk=1024, deps=[t_ple])[0]])

    def swiglu_bwd(accs, extras):
        gt, up = extras[0].astype(F32), extras[1].astype(F32)
        sg = jax.nn.sigmoid(gt)
        dact = accs[0]
        return dact * up * (sg * (1.0 + gt * (1.0 - sg))), dact * (gt * sg)

    dgate, dup = _matmul("mm_dact", [[(dh2_16, W["w_down"])]], "nt", [BF16, BF16], swiglu_bwd,
                         extras=[(gate, "mn"), (up, "mn")], tm=2048, tn=512, sub=256, deps=[t_down])
    t_up = send_grads(("w_gate", "w_up"),
                      [_matmul("mm_dw_gate", [[(f, dgate)]], "tn", [BF16], _plain, tn=1408, tk=2048)[0],
                       _matmul("mm_dw_up", [[(f, dup)]], "tn", [BF16], _plain, tn=1408, tk=2048)[0]])
    df = _matmul("mm_df", [[(dgate, W["w_gate"]), (dup, W["w_up"])]], "nt", [BF16], _plain, tm=1024, tn=256,
                 tk=5632, deps=[t_up], vmem=VMEM_LIMIT_LARGE)[0]
    dh1, dh1_16, dg_ffn = _rms_bwd("rms_ffn_bwd", df, h1, g_ffn, dh2)
    t_out = send_grads(("w_out",), [_matmul("mm_dw_out", [[(cat, dh1_16)]], "tn", [BF16], _plain, tk=2048)[0]])
    dcat = _matmul("mm_dcat", [[(dh1_16, W["w_out"])]], "nt", [F32], _plain, deps=[t_out])[0]
    j1 = reduce_grads([("w_ple", "w_pgate"), ("w_down",)], dcat)
    dc, dg_ln, db_ln, db_dw = _conv_ln_bwd(dcat, conv_c, g_conv_ln, b_conv_ln, deps=[j1])
    dcv, dcg, dw_dw = _conv_bwd(z, dc, w_dw_full, 3 * DA // DC)
    dq, dk, dv = _attn_bwd(z, dcat, o_attn, lse, H)
    dz = [dq, dk, dv, dcv, dcg]
    dw_in = jnp.concatenate([_matmul("mm_dw_in%d" % n, [[(a, part)]], "tn", [BF16], _plain, tk=2048)[0]
                             for n, part in enumerate(dz)], axis=1)
    t_in = send_grads(("w_in",), [dw_in])
    j2 = reduce_grads([("w_gate", "w_up"), ("w_out",)], dw_in)
    da = _matmul("mm_da", [[(part, W["w_in"], n) for n, part in enumerate(dz)]], "nt", [BF16], _plain, tm=512,
                 tk=DA, deps=[t_in, j2])[0]
    grad_x, _, dg_mix = _rms_bwd("rms_mix_bwd", da, x2, g_mix, dh1)

    wide = [dg_mix, dg_ffn, dg_ple, db_pgate, dg_final,
            jnp.concatenate([db_dw, dg_ln], axis=1), jnp.concatenate([db_ln, jnp.zeros_like(db_ln)], axis=1),
            jnp.pad(loss_part, ((0, 0), (0, D - 128))),
            dw_dw.reshape(HALO * DC // D, D)]
    small = jnp.concatenate(wide, axis=0)
    small = jnp.pad(small, ((0, -small.shape[0] % 8), (0, 0)))
    small_sum = _sum_slots("sum_small", _exchange_small(small))
    j3 = reduce_grads([("w_in",)], small_sum)

    grads, deltas, new_m, new_v = {}, {}, {}, {}
    moments = dict(w_in=(m_w_in, v_w_in), w_out=(m_w_out, v_w_out), w_gate=(m_w_gate, v_w_gate),
                   w_up=(m_w_up, v_w_up), w_down=(m_w_down, v_w_down), w_pgate=(m_w_pgate, v_w_pgate),
                   w_ple=(m_w_ple, v_w_ple))
    last, deps = small_sum, [j3]
    for keys in list(joined):
        for k, g_k in zip(keys, _join_wait("join_wait_" + keys[0], joined[keys], last)):
            d_, m_, v_, g_ = _adamw("adamw_" + k, big[k][0], g_k, moments[k][0][0], moments[k][1][0], deps=deps)
            grads[k], deltas[k], new_m[k], new_v[k] = g_[None], d_[None], m_[None], v_[None]
            last, deps = d_, ()

    half = lambda r, lo: small_sum[r:r + 1, lo * DC:(lo + 1) * DC]
    vec = dict(g_mix=small_sum[0:1], g_ffn=small_sum[1:2], g_ple=small_sum[2:3], b_pgate=small_sum[3:4],
               g_final=small_sum[4:5], b_dw=half(5, 0), g_conv_ln=half(5, 1), b_conv_ln=half(6, 0))
    loss = small_sum[7, 0]
    dw_dw_sum = small_sum[8:8 + HALO * DC // D].reshape(HALO, DC)
    s_me = 2 * lax.axis_index("x") + lax.axis_index("y")
    cs = w_dw.shape[3]
    vec["w_dw"] = lax.dynamic_slice(dw_dw_sum, (0, s_me * cs), (CONV_W, cs))
    small_w = dict(g_mix=(g_mix, m_g_mix, v_g_mix), g_ffn=(g_ffn, m_g_ffn, v_g_ffn), g_ple=(g_ple, m_g_ple, v_g_ple),
                   b_pgate=(b_pgate, m_b_pgate, v_b_pgate), g_final=(g_final, m_g_final, v_g_final),
                   b_dw=(b_dw, m_b_dw, v_b_dw), g_conv_ln=(g_conv_ln, m_g_conv_ln, v_g_conv_ln),
                   b_conv_ln=(b_conv_ln, m_b_conv_ln, v_b_conv_ln), w_dw=(w_dw, m_w_dw, v_w_dw))
    for k, (w_, m_, v_) in small_w.items():
        shape = w_.shape
        g2 = vec[k]
        to2 = lambda t: t.reshape(g2.shape)
        d_, nm_, nv_, g_ = _adamw("adamw_" + k, to2(w_), g2, to2(m_), to2(v_))
        grads[k], deltas[k], new_m[k], new_v[k] = (t.reshape(shape) for t in (g_, d_, nm_, nv_))

    order = ["g_mix", "w_in", "w_dw", "b_dw", "g_conv_ln", "b_conv_ln", "w_out", "g_ffn", "w_gate", "w_up", "w_down",
             "g_ple", "w_pgate", "b_pgate", "w_ple", "g_final"]
    return (loss, grad_x.reshape(x.shape), *[grads[k] for k in order], *[deltas[k] for k in order],
            *[new_m[k] for k in order], *[new_v[k] for k in order])
```
